```python
import jax, jax.numpy as jnp
from jax import lax
import numpy as np

D_MODEL = 2048
BATCH = 32
SEQ = 256
DEPTH = 4
DEC_BATCH = 4
DEC_SEQ = 2048
PAST_LEN = 512

GRID_W = 64
N_EVEN = (DEPTH + 1) // 2
N_ODD = DEPTH // 2
N_MOD = 9
D_FF = 5632
EPS = 1e-6
CHUNK = 64
CONV_W = 4
D_A = D_MODEL // 2
NB_A = 8
BS_A = D_A // NB_A
LRU_C = 8.0
H_B = 4
DK_B = D_MODEL // 8
DV_B = D_MODEL // 8
H_C = 4
DK_C = D_MODEL // 16
DV_C = D_MODEL // 8
H_D = 4
DK_D = D_MODEL // 16
DV_D = D_MODEL // 8
GLA_RANK = 16
GLA_TAU = 16.0
ROPE_BASE = 10000.0
D_MIX_AB = D_A + H_B * DV_B
D_MIX_CD = H_C * DV_C + H_D * DV_D
N_IN_AB = 2 * D_A + 2 * H_B * DK_B + 2 * H_B * DV_B + 4 * H_B
N_IN_CD = 2 * H_C * DK_C + 2 * H_C * DV_C + 2 * H_D * DK_D + 2 * H_D * DV_D + 2 * GLA_RANK

kernel_name = 'hybrid_bidir_recurrent_diffusion_step'


def _offsets(sizes):
    return [int(s) for s in np.cumsum(sizes)[:-1]]


def rmsnorm(x, w):
    xf = x.astype(jnp.float32)
    y = xf * lax.rsqrt(jnp.mean(xf * xf, axis=-1, keepdims=True) + EPS)
    return (y * w.astype(jnp.float32)).astype(x.dtype)


def head_norm(h):
    hf = h.astype(jnp.float32)
    mu = jnp.mean(hf, axis=-1, keepdims=True)
    var = jnp.mean(jnp.square(hf - mu), axis=-1, keepdims=True)
    return (hf - mu) * lax.rsqrt(var + EPS)


def modulate(h, shift, scale):
    return h * (1 + scale[:, None]) + shift[:, None]


def swiglu(h, wg, wu, wd):
    return (jax.nn.silu(h @ wg) * (h @ wu)) @ wd


def rev(t):
    return jnp.flip(t, axis=1)


def dwconv(x, w, b):
    T = x.shape[1]
    xp = jnp.pad(x, ((0, 0), (CONV_W // 2, CONV_W - 1 - CONV_W // 2), (0, 0)))
    y = b
    for k in range(CONV_W):
        y = y + w[k] * xp[:, k:k + T]
    return y


def to_chunks(t):
    Bsz, T = t.shape[:2]
    t = t.reshape((Bsz, T // CHUNK, CHUNK) + t.shape[2:])
    return jnp.moveaxis(jnp.moveaxis(t, 1, 0), 2, 3)


def from_chunks(t):
    t = jnp.moveaxis(jnp.moveaxis(t, 3, 2), 0, 1)
    return t.reshape((t.shape[0], t.shape[1] * t.shape[2]) + t.shape[3:])


def rope_tables(T):
    rows = T // GRID_W
    t = jnp.arange(rows * GRID_W)
    row = (t // GRID_W).astype(jnp.float32)
    col = (t % GRID_W).astype(jnp.float32)
    nf = DK_C // 4
    freqs = ROPE_BASE ** (-jnp.arange(nf, dtype=jnp.float32) / nf)
    ang = jnp.concatenate([row[:, None] * freqs, col[:, None] * freqs], axis=-1)
    return jnp.cos(ang), jnp.sin(ang)


def apply_rope(x, cos, sin):
    x1, x2 = x[..., :DK_C // 2], x[..., DK_C // 2:]
    cs, sn = cos[:, None], sin[:, None]
    return jnp.concatenate([x1 * cs - x2 * sn, x1 * sn + x2 * cs], axis=-1)


def lru_scan(a, b, h0):
    def comb(l, r):
        return (l[0] * r[0], r[0] * l[1] + r[1])
    A, Bc = lax.associative_scan(comb, (a, b), axis=1)
    h = Bc + A * h0[:, None]
    return h, h[:, -1]


def rglru_dir(xc, w_a, b_a, w_i, b_i, lam, h0):
    Bsz, T, _ = xc.shape
    xb = xc.reshape(Bsz, T, NB_A, BS_A)
    r = jax.nn.sigmoid(jnp.einsum('btnc,ncd->btnd', xb, w_a).reshape(Bsz, T, D_A) + b_a)
    i = jax.nn.sigmoid(jnp.einsum('btnc,ncd->btnd', xb, w_i).reshape(Bsz, T, D_A) + b_i)
    log_a = -LRU_C * r * jax.nn.softplus(-lam)
    a = jnp.exp(log_a)
    mult = jnp.sqrt(-jnp.expm1(2.0 * log_a))
    return lru_scan(a, mult * i * xc, h0)


def mlstm_dir(q, k, v, ig, fg, C0, n0, m0):
    causal = jnp.tril(jnp.ones((CHUNK, CHUNK), bool))

    def step(carry, xs):
        C, n, m = carry
        qc, kc, vc, ic, fc = xs
        F = jnp.cumsum(fc, axis=-1)
        b = F + m[..., None]
        dlog = jnp.where(causal, F[..., :, None] - F[..., None, :] + ic[..., None, :], -jnp.inf)
        m_t = jnp.maximum(b, jnp.max(dlog, axis=-1))
        w = jnp.exp(dlog - m_t[..., None])
        inter = jnp.exp(b - m_t)
        s = jnp.einsum('bhjd,bhid->bhji', qc, kc) * w
        num = jnp.einsum('bhji,bhie->bhje', s, vc) + inter[..., None] * jnp.einsum('bhjd,bhde->bhje', qc, C)
        den = jnp.sum(s, axis=-1) + inter * jnp.einsum('bhjd,bhd->bhj', qc, n)
        h = num / jnp.maximum(jnp.abs(den), jnp.exp(-m_t))[..., None]
        m_new = m_t[..., -1]
        kw = jnp.exp(F[..., -1:] - F + ic - m_new[..., None])
        decay = jnp.exp(F[..., -1] + m - m_new)
        C = decay[..., None, None] * C + jnp.einsum('bhid,bhie->bhde', kc * kw[..., None], vc)
        n = decay[..., None] * n + jnp.einsum('bhid,bhi->bhd', kc, kw)
        return (C, n, m_new), h

    (C, n, m), h = lax.scan(step, (C0, n0, m0), (to_chunks(q), to_chunks(k), to_chunks(v), to_chunks(ig), to_chunks(fg)))
    return from_chunks(h), C, n, m


def retention_dir(q, k, v, log_gamma, S0):
    pos = jnp.arange(CHUNK, dtype=jnp.float32)
    diff = pos[:, None] - pos[None, :]
    causal = diff >= 0
    dmat = jnp.where(causal, jnp.exp(jnp.where(causal, diff, 0.0) * log_gamma[:, None, None]), 0.0)
    q_in = jnp.exp((pos + 1.0) * log_gamma[:, None])
    k_out = jnp.exp((CHUNK - 1.0 - pos) * log_gamma[:, None])
    g_chunk = jnp.exp(CHUNK * log_gamma)

    def step(S, xs):
        qc, kc, vc = xs
        s = jnp.einsum('bhjd,bhid->bhji', qc, kc) * dmat
        o = jnp.einsum('bhji,bhie->bhje', s, vc) + jnp.einsum('bhjd,bhde->bhje', qc * q_in[..., None], S)
        S = g_chunk[:, None, None] * S + jnp.einsum('bhid,bhie->bhde', kc * k_out[..., None], vc)
        return S, o

    S, o = lax.scan(step, S0, (to_chunks(q), to_chunks(k), to_chunks(v)))
    return from_chunks(o), S


def gla_dir(q, k, v, g, S0):
    causal = jnp.tril(jnp.ones((CHUNK, CHUNK), bool))

    def step(S, xs):
        qc, kc, vc, gc = xs
        G = jnp.cumsum(gc, axis=2)
        diff = G[:, :, :, None, :] - G[:, :, None, :, :]
        dec = jnp.exp(jnp.where(causal[:, :, None], diff, -jnp.inf))
        s = jnp.einsum('bhjd,bhid,bhjid->bhji', qc, kc, dec)
        o = jnp.einsum('bhji,bhie->bhje', s, vc) + jnp.einsum('bhjd,bhde->bhje', qc * jnp.exp(G), S)
        GL = G[:, :, -1:]
        S = jnp.exp(GL[:, :, 0])[..., None] * S + jnp.einsum('bhid,bhie->bhde', kc * jnp.exp(GL - G), vc)
        return S, o

    S, o = lax.scan(step, S0, (to_chunks(q), to_chunks(k), to_chunks(v), to_chunks(g)))
    return from_chunks(o), S


def mixer_ab(h, lru_h0, C0, n0, m0, w_in, w_out, conv_w, conv_b, w_a, b_a, w_i, b_i, lam, mconv_w, mconv_b, i_bias, f_bias):
    f32 = jnp.float32
    Bsz, T, _ = h.shape
    z = (h @ w_in).astype(f32)
    xa, ya, qk, v, og, ig, fg = jnp.split(z, _offsets([D_A, D_A, 2 * H_B * DK_B, H_B * DV_B, H_B * DV_B, 2 * H_B, 2 * H_B]), axis=-1)
    xc = dwconv(xa, conv_w.astype(f32), conv_b.astype(f32))
    w_a, b_a, w_i, b_i, lam = w_a.astype(f32), b_a.astype(f32), w_i.astype(f32), b_i.astype(f32), lam.astype(f32)
    lru_h0 = lru_h0.astype(f32)
    hf, sf = rglru_dir(xc, w_a[0], b_a[0], w_i[0], b_i[0], lam[0], lru_h0[:, 0])
    hb, sb = rglru_dir(rev(xc), w_a[1], b_a[1], w_i[1], b_i[1], lam[1], lru_h0[:, 1])
    out_a = (hf + rev(hb)) * jax.nn.gelu(ya)
    qk = jax.nn.silu(dwconv(qk, mconv_w.astype(f32), mconv_b.astype(f32)))
    q, k = jnp.split(qk, 2, axis=-1)
    q = q.reshape(Bsz, T, H_B, DK_B) * (DK_B ** -0.5)
    k = k.reshape(Bsz, T, H_B, DK_B)
    v = v.reshape(Bsz, T, H_B, DV_B)
    ig = ig.reshape(Bsz, T, 2, H_B) + i_bias.astype(f32)
    fg = jax.nn.log_sigmoid(fg.reshape(Bsz, T, 2, H_B) + f_bias.astype(f32))
    C0, n0, m0 = C0.astype(f32), n0.astype(f32), m0.astype(f32)
    of, Cf, nf, mf = mlstm_dir(q, k, v, ig[:, :, 0], fg[:, :, 0], C0[:, 0], n0[:, 0], m0[:, 0])
    ob, Cb, nb, mb = mlstm_dir(rev(q), rev(k), rev(v), rev(ig[:, :, 1]), rev(fg[:, :, 1]), C0[:, 1], n0[:, 1], m0[:, 1])
    out_b = head_norm(of + rev(ob)) * jax.nn.silu(og.reshape(Bsz, T, H_B, DV_B))
    mix = jnp.concatenate([out_a, out_b.reshape(Bsz, T, H_B * DV_B)], axis=-1).astype(h.dtype)
    states = (jnp.stack([sf, sb], axis=1), jnp.stack([Cf, Cb], axis=1), jnp.stack([nf, nb], axis=1), jnp.stack([mf, mb], axis=1))
    return mix @ w_out, states


def mixer_cd(h, S_ret0, S_gla0, w_in, w_out, ret_lg, gla_w_up, gla_b, rope):
    f32 = jnp.float32
    Bsz, T, _ = h.shape
    z = (h @ w_in).astype(f32)
    qc, kc, vc, gc, qd, kd, vd, rd, ad = jnp.split(z, _offsets([H_C * DK_C, H_C * DK_C, H_C * DV_C, H_C * DV_C, H_D * DK_D, H_D * DK_D, H_D * DV_D, H_D * DV_D, 2 * GLA_RANK]), axis=-1)
    q = qc.reshape(Bsz, T, H_C, DK_C)
    k = kc.reshape(Bsz, T, H_C, DK_C) * (DK_C ** -0.5)
    if rope is not None:
        q = apply_rope(q, rope[0], rope[1])
        k = apply_rope(k, rope[0], rope[1])
    v = vc.reshape(Bsz, T, H_C, DV_C)
    lg = ret_lg.astype(f32)
    S_ret0 = S_ret0.astype(f32)
    of, Sf = retention_dir(q, k, v, lg[0], S_ret0[:, 0])
    ob, Sb = retention_dir(rev(q), rev(k), rev(v), lg[1], S_ret0[:, 1])
    out_c = head_norm(of + rev(ob)) * jax.nn.silu(gc.reshape(Bsz, T, H_C, DV_C))
    qg = qd.reshape(Bsz, T, H_D, DK_D) * (DK_D ** -0.5)
    kg = kd.reshape(Bsz, T, H_D, DK_D)
    vg = vd.reshape(Bsz, T, H_D, DV_D)
    a_low = ad.reshape(Bsz, T, 2, GLA_RANK)
    g = jax.nn.log_sigmoid(jnp.einsum('btzr,zre->btze', a_low, gla_w_up.astype(f32)) + gla_b.astype(f32)) / GLA_TAU
    g = g.reshape(Bsz, T, 2, H_D, DK_D)
    S_gla0 = S_gla0.astype(f32)
    pf, Gf = gla_dir(qg, kg, vg, g[:, :, 0], S_gla0[:, 0])
    pb, Gb = gla_dir(rev(qg), rev(kg), rev(vg), rev(g[:, :, 1]), S_gla0[:, 1])
    out_d = head_norm(pf + rev(pb)) * jax.nn.silu(rd.reshape(Bsz, T, H_D, DV_D))
    mix = jnp.concatenate([out_c.reshape(Bsz, T, H_C * DV_C), out_d.reshape(Bsz, T, H_D * DV_D)], axis=-1).astype(h.dtype)
    states = (jnp.stack([Sf, Sb], axis=1), jnp.stack([Gf, Gb], axis=1))
    return mix @ w_out, states


def trunk_layer(x, mod, nw, wg, wu, wd, mixer):
    sh1, sc1, g1, sh2, sc2, g2, sh3, sc3, g3 = jnp.split(mod, N_MOD, axis=-1)
    h = modulate(rmsnorm(x, nw[0]), sh1, sc1)
    x = x + 0.5 * g1[:, None] * swiglu(h, wg[0], wu[0], wd[0])
    h = modulate(rmsnorm(x, nw[1]), sh2, sc2)
    out, states = mixer(h)
    x = x + g2[:, None] * out
    h = modulate(rmsnorm(x, nw[2]), sh3, sc3)
    x = x + 0.5 * g3[:, None] * swiglu(h, wg[1], wu[1], wd[1])
    return x, states


def setup_inputs(seed: int = 0) -> dict:
    key = jax.random.key(seed)
    ks = iter(jax.random.split(key, 48))
    f32 = jnp.float32

    def nrm(shape, s):
        return jax.random.normal(next(ks), shape, f32) * s

    def unif(shape, lo, hi):
        return jax.random.uniform(next(ks), shape, f32, lo, hi)

    D = D_MODEL
    a0 = unif((N_EVEN, 2, D_A), 0.9, 0.999)
    base = jnp.stack([jnp.log1p(-jnp.exp2(-5.0 - jnp.arange(H_C, dtype=f32))),
                      jnp.log1p(-jnp.exp2(-5.5 - jnp.arange(H_C, dtype=f32)))], axis=0)
    return {
        'x_prompt': nrm((BATCH, SEQ, D), 1.0),
        'x_sample': nrm((DEC_BATCH, DEC_SEQ, D), 1.0),
        'c': nrm((DEC_BATCH, D), 1.0),
        'state_lru': nrm((DEC_BATCH, N_EVEN, 2, D_A), 0.5),
        'state_mlstm_C': nrm((DEC_BATCH, N_EVEN, 2, H_B, DK_B, DV_B), 0.1),
        'state_mlstm_n': nrm((DEC_BATCH, N_EVEN, 2, H_B, DK_B), 0.1),
        'state_mlstm_m': nrm((DEC_BATCH, N_EVEN, 2, H_B), 0.5),
        'state_ret': nrm((DEC_BATCH, N_ODD, 2, H_C, DK_C, DV_C), 0.5),
        'state_gla': nrm((DEC_BATCH, N_ODD, 2, H_D, DK_D, DV_D), 0.5),
        'c_ctx': nrm((D,), 1.0),
        'w_mod': nrm((DEPTH, D, N_MOD * D), 0.5 * D ** -0.5),
        'b_mod': nrm((DEPTH, N_MOD * D), 0.01),
        'norm_w': 1.0 + nrm((DEPTH, 3, D), 0.02),
        'ffn_w_gate': nrm((DEPTH, 2, D, D_FF), D ** -0.5),
        'ffn_w_up': nrm((DEPTH, 2, D, D_FF), D ** -0.5),
        'ffn_w_down': nrm((DEPTH, 2, D_FF, D), D_FF ** -0.5),
        'w_in_ab': nrm((N_EVEN, D, N_IN_AB), D ** -0.5),
        'w_out_ab': nrm((N_EVEN, D_MIX_AB, D), D_MIX_AB ** -0.5),
        'lru_conv_w': nrm((N_EVEN, CONV_W, D_A), CONV_W ** -0.5),
        'lru_conv_b': nrm((N_EVEN, D_A), 0.01),
        'lru_w_a': nrm((N_EVEN, 2, NB_A, BS_A, BS_A), BS_A ** -0.5),
        'lru_b_a': nrm((N_EVEN, 2, D_A), 0.01),
        'lru_w_i': nrm((N_EVEN, 2, NB_A, BS_A, BS_A), BS_A ** -0.5),
        'lru_b_i': nrm((N_EVEN, 2, D_A), 0.01),
        'lru_lambda': jnp.log(a0) - jnp.log1p(-a0),
        'mlstm_conv_w': nrm((N_EVEN, CONV_W, 2 * H_B * DK_B), CONV_W ** -0.5),
        'mlstm_conv_b': nrm((N_EVEN, 2 * H_B * DK_B), 0.01),
        'mlstm_i_bias': nrm((N_EVEN, 2, H_B), 0.1),
        'mlstm_f_bias': unif((N_EVEN, 2, H_B), 3.0, 6.0),
        'w_in_cd': nrm((N_ODD, D, N_IN_CD), D ** -0.5),
        'w_out_cd': nrm((N_ODD, D_MIX_CD, D), D_MIX_CD ** -0.5),
        'ret_decay_log': base[None] * (1.0 + nrm((N_ODD, 2, H_C), 0.05)),
        'gla_w_up': nrm((N_ODD, 2, GLA_RANK, H_D * DK_D), GLA_RANK ** -0.5),
        'gla_b': nrm((N_ODD, 2, H_D * DK_D), 0.01),
        'final_norm_w': 1.0 + nrm((D,), 0.02),
    }


def reference(x_prompt, x_sample, c, state_lru, state_mlstm_C, state_mlstm_n, state_mlstm_m, state_ret, state_gla,
              c_ctx, w_mod, b_mod, norm_w, ffn_w_gate, ffn_w_up, ffn_w_down,
              w_in_ab, w_out_ab, lru_conv_w, lru_conv_b, lru_w_a, lru_b_a, lru_w_i, lru_b_i, lru_lambda,
              mlstm_conv_w, mlstm_conv_b, mlstm_i_bias, mlstm_f_bias,
              w_in_cd, w_out_cd, ret_decay_log, gla_w_up, gla_b, final_norm_w):
    f32 = jnp.float32
    Bp = x_prompt.shape[0]
    Ts = x_sample.shape[1]
    rope = rope_tables(Ts)
    zero_lru = jnp.zeros((Bp, 2, D_A), f32)
    zero_C = jnp.zeros((Bp, 2, H_B, DK_B, DV_B), f32)
    zero_n = jnp.zeros((Bp, 2, H_B, DK_B), f32)
    zero_m = jnp.zeros((Bp, 2, H_B), f32)
    zero_ret = jnp.zeros((Bp, 2, H_C, DK_C, DV_C), f32)
    zero_gla = jnp.zeros((Bp, 2, H_D, DK_D, DV_D), f32)
    xp, xs = x_prompt, x_sample
    st_lru, st_C, st_n, st_m, st_ret, st_gla = [], [], [], [], [], []
    for l in range(DEPTH):
        mod_ctx = (jax.nn.silu(c_ctx) @ w_mod[l] + b_mod[l])[None]
        mod_lat = jax.nn.silu(c) @ w_mod[l] + b_mod[l]
        ffn = (norm_w[l], ffn_w_gate[l], ffn_w_up[l], ffn_w_down[l])
        if l % 2 == 0:
            e = l // 2
            ab = (w_in_ab[e], w_out_ab[e], lru_conv_w[e], lru_conv_b[e], lru_w_a[e], lru_b_a[e], lru_w_i[e], lru_b_i[e],
                  lru_lambda[e], mlstm_conv_w[e], mlstm_conv_b[e], mlstm_i_bias[e], mlstm_f_bias[e])
            xp, (s_lru, s_C, s_n, s_m) = trunk_layer(xp, mod_ctx, *ffn, lambda h: mixer_ab(h, zero_lru, zero_C, zero_n, zero_m, *ab))
            xs, _ = trunk_layer(xs, mod_lat, *ffn, lambda h: mixer_ab(h, state_lru[:, e], state_mlstm_C[:, e], state_mlstm_n[:, e], state_mlstm_m[:, e], *ab))
            st_lru.append(s_lru)
            st_C.append(s_C)
            st_n.append(s_n)
            st_m.append(s_m)
        else:
            o = l // 2
            cd = (w_in_cd[o], w_out_cd[o], ret_decay_log[o], gla_w_up[o], gla_b[o])
            xp, (s_ret, s_gla) = trunk_layer(xp, mod_ctx, *ffn, lambda h: mixer_cd(h, zero_ret, zero_gla, *cd, None))
            xs, _ = trunk_layer(xs, mod_lat, *ffn, lambda h: mixer_cd(h, state_ret[:, o], state_gla[:, o], *cd, rope))
            st_ret.append(s_ret)
            st_gla.append(s_gla)
    y_prompt = rmsnorm(xp, final_norm_w)
    y_sample = rmsnorm(xs, final_norm_w)
    dt = x_prompt.dtype
    new_lru = jnp.stack(st_lru, axis=1).astype(dt)
    new_C = jnp.stack(st_C, axis=1).astype(dt)
    new_n = jnp.stack(st_n, axis=1).astype(dt)
    new_m = jnp.stack(st_m, axis=1).astype(dt)
    new_ret = jnp.stack(st_ret, axis=1).astype(dt)
    new_gla = jnp.stack(st_gla, axis=1).astype(dt)
    return (y_prompt, y_sample, new_lru, new_C, new_n, new_m, new_ret, new_gla)
```

```python
from functools import partial

import jax
import jax.numpy as jnp
import numpy as np
from jax import lax
from jax.experimental import pallas as pl
from jax.experimental.pallas import tpu as pltpu

F32 = jnp.float32
BF16 = jnp.bfloat16

D_MODEL = 2048
DEPTH = 4
N_MOD = 9
D_FF = 5632
EPS = 1e-6
CHUNK = 64
CONV_W = 4
D_A = D_MODEL // 2
NB_A = 8
BS_A = D_A // NB_A
LRU_C = 8.0
H_B = 4
DK_B = D_MODEL // 8
DV_B = D_MODEL // 8
H_C = 4
DK_C = D_MODEL // 16
DV_C = D_MODEL // 8
H_D = 4
DK_D = D_MODEL // 16
DV_D = D_MODEL // 8
GLA_RANK = 16
GLA_TAU = 16.0
ROPE_BASE = 10000.0
GRID_W = 64

MOD_ROWS = 8
LANE = 128
VMEM_LIMIT = 56 * 1024 * 1024


def _params(sem):
    return pltpu.CompilerParams(dimension_semantics=sem, vmem_limit_bytes=VMEM_LIMIT)


def _mod_kernel(c_ref, w_ref, b_ref, o_ref):
    c = c_ref[...]
    s = (c * jax.nn.sigmoid(c)).astype(BF16)
    o_ref[...] = jnp.dot(s, w_ref[...].astype(BF16), preferred_element_type=F32) + b_ref[...]


def _modulation(cond, w_mod, b_mod, tn=1024):
    n = w_mod.shape[-1]
    return pl.pallas_call(
        _mod_kernel,
        grid=(DEPTH, n // tn),
        in_specs=[
            pl.BlockSpec((MOD_ROWS, D_MODEL), lambda l, j: (0, 0)),
            pl.BlockSpec((None, D_MODEL, tn), lambda l, j: (l, 0, j)),
            pl.BlockSpec((None, 1, tn), lambda l, j: (l, 0, j)),
        ],
        out_specs=pl.BlockSpec((None, MOD_ROWS, tn), lambda l, j: (l, 0, j)),
        out_shape=jax.ShapeDtypeStruct((DEPTH, MOD_ROWS, n), F32),
        compiler_params=_params(("arbitrary", "arbitrary")),
    )(cond, w_mod, b_mod.reshape(DEPTH, 1, n))


def _norm_mod(x, nw, shift, scale):
    ms = jnp.mean(x * x, axis=-1, keepdims=True)
    y = x * lax.rsqrt(ms + EPS) * nw
    return y * (1.0 + scale) + shift


def _mod_row_map(n_ctx_tiles, tiles_per_req):
    def row(i):
        return jnp.where(i < n_ctx_tiles, 0, 1 + (i - n_ctx_tiles) // tiles_per_req)
    return row


def _ffn_kernel(x_ref, mod_ref, nw_ref, wg_ref, wu_ref, wd_ref, o_ref, h_scr, *, k, nf):
    f = pl.program_id(1)

    @pl.when(f == 0)
    def _():
        h = _norm_mod(x_ref[...], nw_ref[...], mod_ref[3 * k:3 * k + 1, :], mod_ref[3 * k + 1:3 * k + 2, :])
        h_scr[...] = h.astype(BF16)
        o_ref[...] = jnp.zeros_like(o_ref)

    h = h_scr[...]
    g = jnp.dot(h, wg_ref[...], preferred_element_type=F32)
    u = jnp.dot(h, wu_ref[...], preferred_element_type=F32)
    a = (g * jax.nn.sigmoid(g) * u).astype(BF16)
    o_ref[...] += jnp.dot(a, wd_ref[...], preferred_element_type=F32)

    @pl.when(f == nf - 1)
    def _():
        o_ref[...] = x_ref[...] + (0.5 * mod_ref[3 * k + 2:3 * k + 3, :]) * o_ref[...]


def _ffn(x, mod_l, nw, wg, wu, wd, k, row_of_tile, tm, tf=512):
    t = x.shape[0]
    nf = D_FF // tf
    return pl.pallas_call(
        partial(_ffn_kernel, k=k, nf=nf),
        grid=(t // tm, nf),
        in_specs=[
            pl.BlockSpec((tm, D_MODEL), lambda i, f: (i, 0)),
            pl.BlockSpec((None, N_MOD, D_MODEL), lambda i, f: (row_of_tile(i), 0, 0)),
            pl.BlockSpec((None, 1, D_MODEL), lambda i, f: (k, 0, 0)),
            pl.BlockSpec((D_MODEL, tf), lambda i, f: (0, f)),
            pl.BlockSpec((D_MODEL, tf), lambda i, f: (0, f)),
            pl.BlockSpec((tf, D_MODEL), lambda i, f: (f, 0)),
        ],
        out_specs=pl.BlockSpec((tm, D_MODEL), lambda i, f: (i, 0)),
        out_shape=jax.ShapeDtypeStruct((t, D_MODEL), F32),
        scratch_shapes=[pltpu.VMEM((tm, D_MODEL), BF16)],
        compiler_params=_params(("arbitrary", "arbitrary")),
    )(x, mod_l, nw, wg, wu, wd)


def _inproj_kernel(x_ref, mod_ref, nw_ref, w_ref, wt_ref, z_ref, zt_ref, h_scr):
    j = pl.program_id(1)

    @pl.when(j == 0)
    def _():
        h = _norm_mod(x_ref[...], nw_ref[...], mod_ref[3:4, :], mod_ref[4:5, :])
        h_scr[...] = h.astype(BF16)
        zt_ref[...] = jnp.dot(h_scr[...], wt_ref[...], preferred_element_type=F32)

    z_ref[...] = jnp.dot(h_scr[...], w_ref[...], preferred_element_type=F32)


def _inproj(x, mod_l, nw, w_main, w_tail, row_of_tile, tm, tn=1024):
    t = x.shape[0]
    n = w_main.shape[1]
    return pl.pallas_call(
        _inproj_kernel,
        grid=(t // tm, n // tn),
        in_specs=[
            pl.BlockSpec((tm, D_MODEL), lambda i, j: (i, 0)),
            pl.BlockSpec((None, N_MOD, D_MODEL), lambda i, j: (row_of_tile(i), 0, 0)),
            pl.BlockSpec((None, 1, D_MODEL), lambda i, j: (1, 0, 0)),
            pl.BlockSpec((D_MODEL, tn), lambda i, j: (0, j)),
            pl.BlockSpec((D_MODEL, LANE), lambda i, j: (0, 0)),
        ],
        out_specs=[
            pl.BlockSpec((tm, tn), lambda i, j: (i, j)),
            pl.BlockSpec((tm, LANE), lambda i, j: (i, 0)),
        ],
        out_shape=[jax.ShapeDtypeStruct((t, n), F32), jax.ShapeDtypeStruct((t, LANE), F32)],
        scratch_shapes=[pltpu.VMEM((tm, D_MODEL), BF16)],
        compiler_params=_params(("arbitrary", "arbitrary")),
    )(x, mod_l, nw, w_main, w_tail)


def _outproj_kernel(x_ref, mix_ref, mod_ref, w_ref, o_ref):
    y = jnp.dot(mix_ref[...], w_ref[...], preferred_element_type=F32)
    o_ref[...] = x_ref[...] + mod_ref[5:6, :] * y


def _outproj(x, mix, mod_l, w_out, row_of_tile, tm):
    t = x.shape[0]
    return pl.pallas_call(
        _outproj_kernel,
        grid=(t // tm,),
        in_specs=[
            pl.BlockSpec((tm, D_MODEL), lambda i: (i, 0)),
            pl.BlockSpec((tm, D_MODEL), lambda i: (i, 0)),
            pl.BlockSpec((None, N_MOD, D_MODEL), lambda i: (row_of_tile(i), 0, 0)),
            pl.BlockSpec((D_MODEL, D_MODEL), lambda i: (0, 0)),
        ],
        out_specs=pl.BlockSpec((tm, D_MODEL), lambda i: (i, 0)),
        out_shape=jax.ShapeDtypeStruct((t, D_MODEL), F32),
        compiler_params=_params(("arbitrary",)),
    )(x, mix, mod_l, w_out)


def _final_norm_kernel(x_ref, w_ref, o_ref):
    x = x_ref[...]
    ms = jnp.mean(x * x, axis=-1, keepdims=True)
    o_ref[...] = x * lax.rsqrt(ms + EPS) * w_ref[...]


def _final_norm(x, w, tm=512):
    t = x.shape[0]
    return pl.pallas_call(
        _final_norm_kernel,
        grid=(t // tm,),
        in_specs=[pl.BlockSpec((tm, D_MODEL), lambda i: (i, 0)), pl.BlockSpec((1, D_MODEL), lambda i: (0, 0))],
        out_specs=pl.BlockSpec((tm, D_MODEL), lambda i: (i, 0)),
        out_shape=jax.ShapeDtypeStruct((t, D_MODEL), F32),
        compiler_params=_params(("arbitrary",)),
    )(x, w.reshape(1, D_MODEL))


def _offsets(sizes):
    return [int(s) for s in np.cumsum(sizes)[:-1]]


def _head_norm(h):
    mu = jnp.mean(h, axis=-1, keepdims=True)
    var = jnp.mean(jnp.square(h - mu), axis=-1, keepdims=True)
    return (h - mu) * lax.rsqrt(var + EPS)


def _rev(t):
    return jnp.flip(t, axis=1)


def _dwconv(x, w, b):
    T = x.shape[1]
    xp = jnp.pad(x, ((0, 0), (CONV_W // 2, CONV_W - 1 - CONV_W // 2), (0, 0)))
    y = b
    for k in range(CONV_W):
        y = y + w[k] * xp[:, k:k + T]
    return y


def _to_chunks(t):
    Bsz, T = t.shape[:2]
    t = t.reshape((Bsz, T // CHUNK, CHUNK) + t.shape[2:])
    return jnp.moveaxis(jnp.moveaxis(t, 1, 0), 2, 3)


def _from_chunks(t):
    t = jnp.moveaxis(jnp.moveaxis(t, 3, 2), 0, 1)
    return t.reshape((t.shape[0], t.shape[1] * t.shape[2]) + t.shape[3:])


def _rope_tables(T):
    rows = T // GRID_W
    t = jnp.arange(rows * GRID_W)
    row = (t // GRID_W).astype(F32)
    col = (t % GRID_W).astype(F32)
    nf = DK_C // 4
    freqs = ROPE_BASE ** (-jnp.arange(nf, dtype=F32) / nf)
    ang = jnp.concatenate([row[:, None] * freqs, col[:, None] * freqs], axis=-1)
    return jnp.cos(ang), jnp.sin(ang)


def _apply_rope(x, cos, sin):
    x1, x2 = x[..., :DK_C // 2], x[..., DK_C // 2:]
    cs, sn = cos[:, None], sin[:, None]
    return jnp.concatenate([x1 * cs - x2 * sn, x1 * sn + x2 * cs], axis=-1)


def _lru_scan(a, b, h0):
    def comb(l, r):
        return (l[0] * r[0], r[0] * l[1] + r[1])
    A, Bc = lax.associative_scan(comb, (a, b), axis=1)
    h = Bc + A * h0[:, None]
    return h, h[:, -1]


def _rglru_dir(xc, w_a, b_a, w_i, b_i, lam, h0):
    Bsz, T, _ = xc.shape
    xb = xc.reshape(Bsz, T, NB_A, BS_A)
    r = jax.nn.sigmoid(jnp.einsum('btnc,ncd->btnd', xb, w_a).reshape(Bsz, T, D_A) + b_a)
    i = jax.nn.sigmoid(jnp.einsum('btnc,ncd->btnd', xb, w_i).reshape(Bsz, T, D_A) + b_i)
    log_a = -LRU_C * r * jax.nn.softplus(-lam)
    a = jnp.exp(log_a)
    mult = jnp.sqrt(-jnp.expm1(2.0 * log_a))
    return _lru_scan(a, mult * i * xc, h0)


def _mlstm_dir(q, k, v, ig, fg, C0, n0, m0):
    causal = jnp.tril(jnp.ones((CHUNK, CHUNK), bool))

    def step(carry, xs):
        C, n, m = carry
        qc, kc, vc, ic, fc = xs
        F = jnp.cumsum(fc, axis=-1)
        b = F + m[..., None]
        dlog = jnp.where(causal, F[..., :, None] - F[..., None, :] + ic[..., None, :], -jnp.inf)
        m_t = jnp.maximum(b, jnp.max(dlog, axis=-1))
        w = jnp.exp(dlog - m_t[..., None])
        inter = jnp.exp(b - m_t)
        s = jnp.einsum('bhjd,bhid->bhji', qc, kc) * w
        num = jnp.einsum('bhji,bhie->bhje', s, vc) + inter[..., None] * jnp.einsum('bhjd,bhde->bhje', qc, C)
        den = jnp.sum(s, axis=-1) + inter * jnp.einsum('bhjd,bhd->bhj', qc, n)
        h = num / jnp.maximum(jnp.abs(den), jnp.exp(-m_t))[..., None]
        m_new = m_t[..., -1]
        kw = jnp.exp(F[..., -1:] - F + ic - m_new[..., None])
        decay = jnp.exp(F[..., -1] + m - m_new)
        C = decay[..., None, None] * C + jnp.einsum('bhid,bhie->bhde', kc * kw[..., None], vc)
        n = decay[..., None] * n + jnp.einsum('bhid,bhi->bhd', kc, kw)
        return (C, n, m_new), h

    (C, n, m), h = lax.scan(step, (C0, n0, m0), (_to_chunks(q), _to_chunks(k), _to_chunks(v), _to_chunks(ig), _to_chunks(fg)))
    return _from_chunks(h), C, n, m


def _retention_dir(q, k, v, log_gamma, S0):
    pos = jnp.arange(CHUNK, dtype=F32)
    diff = pos[:, None] - pos[None, :]
    causal = diff >= 0
    dmat = jnp.where(causal, jnp.exp(jnp.where(causal, diff, 0.0) * log_gamma[:, None, None]), 0.0)
    q_in = jnp.exp((pos + 1.0) * log_gamma[:, None])
    k_out = jnp.exp((CHUNK - 1.0 - pos) * log_gamma[:, None])
    g_chunk = jnp.exp(CHUNK * log_gamma)

    def step(S, xs):
        qc, kc, vc = xs
        s = jnp.einsum('bhjd,bhid->bhji', qc, kc) * dmat
        o = jnp.einsum('bhji,bhie->bhje', s, vc) + jnp.einsum('bhjd,bhde->bhje', qc * q_in[..., None], S)
        S = g_chunk[:, None, None] * S + jnp.einsum('bhid,bhie->bhde', kc * k_out[..., None], vc)
        return S, o

    S, o = lax.scan(step, S0, (_to_chunks(q), _to_chunks(k), _to_chunks(v)))
    return _from_chunks(o), S


def _gla_dir(q, k, v, g, S0):
    causal = jnp.tril(jnp.ones((CHUNK, CHUNK), bool))

    def step(S, xs):
        qc, kc, vc, gc = xs
        G = jnp.cumsum(gc, axis=2)
        diff = G[:, :, :, None, :] - G[:, :, None, :, :]
        dec = jnp.exp(jnp.where(causal[:, :, None], diff, -jnp.inf))
        s = jnp.einsum('bhjd,bhid,bhjid->bhji', qc, kc, dec)
        o = jnp.einsum('bhji,bhie->bhje', s, vc) + jnp.einsum('bhjd,bhde->bhje', qc * jnp.exp(G), S)
        GL = G[:, :, -1:]
        S = jnp.exp(GL[:, :, 0])[..., None] * S + jnp.einsum('bhid,bhie->bhde', kc * jnp.exp(GL - G), vc)
        return S, o

    S, o = lax.scan(step, S0, (_to_chunks(q), _to_chunks(k), _to_chunks(v), _to_chunks(g)))
    return _from_chunks(o), S


def _mixer_ab_core(z, lru_h0, C0, n0, m0, conv_w, conv_b, w_a, b_a, w_i, b_i, lam, mconv_w, mconv_b, i_bias, f_bias):
    Bsz, T, _ = z.shape
    xa, ya, qk, v, og, ig, fg = jnp.split(z, _offsets([D_A, D_A, 2 * H_B * DK_B, H_B * DV_B, H_B * DV_B, 2 * H_B, 2 * H_B]), axis=-1)
    xc = _dwconv(xa, conv_w, conv_b)
    hf, sf = _rglru_dir(xc, w_a[0], b_a[0], w_i[0], b_i[0], lam[0], lru_h0[:, 0])
    hb, sb = _rglru_dir(_rev(xc), w_a[1], b_a[1], w_i[1], b_i[1], lam[1], lru_h0[:, 1])
    out_a = (hf + _rev(hb)) * jax.nn.gelu(ya)
    qk = jax.nn.silu(_dwconv(qk, mconv_w, mconv_b))
    q, k = jnp.split(qk, 2, axis=-1)
    q = q.reshape(Bsz, T, H_B, DK_B) * (DK_B ** -0.5)
    k = k.reshape(Bsz, T, H_B, DK_B)
    v = v.reshape(Bsz, T, H_B, DV_B)
    ig = ig.reshape(Bsz, T, 2, H_B) + i_bias
    fg = jax.nn.log_sigmoid(fg.reshape(Bsz, T, 2, H_B) + f_bias)
    of, Cf, nf, mf = _mlstm_dir(q, k, v, ig[:, :, 0], fg[:, :, 0], C0[:, 0], n0[:, 0], m0[:, 0])
    ob, Cb, nb, mb = _mlstm_dir(_rev(q), _rev(k), _rev(v), _rev(ig[:, :, 1]), _rev(fg[:, :, 1]), C0[:, 1], n0[:, 1], m0[:, 1])
    out_b = _head_norm(of + _rev(ob)) * jax.nn.silu(og.reshape(Bsz, T, H_B, DV_B))
    mix = jnp.concatenate([out_a, out_b.reshape(Bsz, T, H_B * DV_B)], axis=-1)
    states = (jnp.stack([sf, sb], axis=1), jnp.stack([Cf, Cb], axis=1), jnp.stack([nf, nb], axis=1), jnp.stack([mf, mb], axis=1))
    return mix, states


def _mixer_cd_core(z, S_ret0, S_gla0, ret_lg, gla_w_up, gla_b, rope):
    Bsz, T, _ = z.shape
    qc, kc, vc, gc, qd, kd, vd, rd, ad = jnp.split(z, _offsets([H_C * DK_C, H_C * DK_C, H_C * DV_C, H_C * DV_C, H_D * DK_D, H_D * DK_D, H_D * DV_D, H_D * DV_D, 2 * GLA_RANK]), axis=-1)
    q = qc.reshape(Bsz, T, H_C, DK_C)
    k = kc.reshape(Bsz, T, H_C, DK_C) * (DK_C ** -0.5)
    if rope is not None:
        q = _apply_rope(q, rope[0], rope[1])
        k = _apply_rope(k, rope[0], rope[1])
    v = vc.reshape(Bsz, T, H_C, DV_C)
    of, Sf = _retention_dir(q, k, v, ret_lg[0], S_ret0[:, 0])
    ob, Sb = _retention_dir(_rev(q), _rev(k), _rev(v), ret_lg[1], S_ret0[:, 1])
    out_c = _head_norm(of + _rev(ob)) * jax.nn.silu(gc.reshape(Bsz, T, H_C, DV_C))
    qg = qd.reshape(Bsz, T, H_D, DK_D) * (DK_D ** -0.5)
    kg = kd.reshape(Bsz, T, H_D, DK_D)
    vg = vd.reshape(Bsz, T, H_D, DV_D)
    a_low = ad.reshape(Bsz, T, 2, GLA_RANK)
    g = jax.nn.log_sigmoid(jnp.einsum('btzr,zre->btze', a_low, gla_w_up) + gla_b) / GLA_TAU
    g = g.reshape(Bsz, T, 2, H_D, DK_D)
    pf, Gf = _gla_dir(qg, kg, vg, g[:, :, 0], S_gla0[:, 0])
    pb, Gb = _gla_dir(_rev(qg), _rev(kg), _rev(vg), _rev(g[:, :, 1]), S_gla0[:, 1])
    out_d = _head_norm(pf + _rev(pb)) * jax.nn.silu(rd.reshape(Bsz, T, H_D, DV_D))
    mix = jnp.concatenate([out_c.reshape(Bsz, T, H_C * DV_C), out_d.reshape(Bsz, T, H_D * DV_D)], axis=-1)
    states = (jnp.stack([Sf, Sb], axis=1), jnp.stack([Gf, Gb], axis=1))
    return mix, states


def kernel(x_prompt, x_sample, c, state_lru, state_mlstm_C, state_mlstm_n, state_mlstm_m, state_ret, state_gla, c_ctx, w_mod, b_mod, norm_w, ffn_w_gate, ffn_w_up, ffn_w_down, w_in_ab, w_out_ab, lru_conv_w, lru_conv_b, lru_w_a, lru_b_a, lru_w_i, lru_b_i, lru_lambda, mlstm_conv_w, mlstm_conv_b, mlstm_i_bias, mlstm_f_bias, w_in_cd, w_out_cd, ret_decay_log, gla_w_up, gla_b, final_norm_w):
    Bp, Tp, D = x_prompt.shape
    Bs, Ts, _ = x_sample.shape
    n_ctx = Bp * Tp
    tm = 512
    assert n_ctx % tm == 0 and Ts % tm == 0 and 1 + Bs <= MOD_ROWS
    row_of_tile = _mod_row_map(n_ctx // tm, Ts // tm)

    cond = jnp.concatenate([c_ctx[None], c, jnp.zeros((MOD_ROWS - 1 - Bs, D), F32)], axis=0)
    mod = _modulation(cond, w_mod, b_mod).reshape(DEPTH, MOD_ROWS, N_MOD, D)

    x = jnp.concatenate([x_prompt.reshape(n_ctx, D), x_sample.reshape(Bs * Ts, D)], axis=0)
    rope = _rope_tables(Ts)

    zero_lru = jnp.zeros((Bp, 2, D_A), F32)
    zero_C = jnp.zeros((Bp, 2, H_B, DK_B, DV_B), F32)
    zero_n = jnp.zeros((Bp, 2, H_B, DK_B), F32)
    zero_m = jnp.zeros((Bp, 2, H_B), F32)
    zero_ret = jnp.zeros((Bp, 2, H_C, DK_C, DV_C), F32)
    zero_gla = jnp.zeros((Bp, 2, H_D, DK_D, DV_D), F32)

    st_lru, st_C, st_n, st_m, st_ret, st_gla = [], [], [], [], [], []
    for l in range(DEPTH):
        mod_l = mod[l]
        nw = norm_w[l].reshape(3, 1, D)
        wg, wu, wd = ffn_w_gate[l].astype(BF16), ffn_w_up[l].astype(BF16), ffn_w_down[l].astype(BF16)
        x = _ffn(x, mod_l, nw, wg[0], wu[0], wd[0], 0, row_of_tile, tm)
        if l % 2 == 0:
            e = l // 2
            w_in, w_out = w_in_ab[e].astype(BF16), w_out_ab[e].astype(BF16)
        else:
            o = l // 2
            w_in, w_out = w_in_cd[o].astype(BF16), w_out_cd[o].astype(BF16)
        n_in = w_in.shape[1]
        n_main = (n_in // 1024) * 1024
        w_tail = jnp.pad(w_in[:, n_main:], ((0, 0), (0, LANE - (n_in - n_main))))
        z_main, z_tail = _inproj(x, mod_l, nw, w_in[:, :n_main], w_tail, row_of_tile, tm)
        z = jnp.concatenate([z_main, z_tail[:, :n_in - n_main]], axis=-1)
        zp = z[:n_ctx].reshape(Bp, Tp, n_in)
        zs = z[n_ctx:].reshape(Bs, Ts, n_in)
        if l % 2 == 0:
            ab = (lru_conv_w[e], lru_conv_b[e], lru_w_a[e], lru_b_a[e], lru_w_i[e], lru_b_i[e], lru_lambda[e],
                  mlstm_conv_w[e], mlstm_conv_b[e], mlstm_i_bias[e], mlstm_f_bias[e])
            mix_p, (s_lru, s_C, s_n, s_m) = _mixer_ab_core(zp, zero_lru, zero_C, zero_n, zero_m, *ab)
            mix_s, _ = _mixer_ab_core(zs, state_lru[:, e], state_mlstm_C[:, e], state_mlstm_n[:, e], state_mlstm_m[:, e], *ab)
            st_lru.append(s_lru)
            st_C.append(s_C)
            st_n.append(s_n)
            st_m.append(s_m)
        else:
            cd = (ret_decay_log[o], gla_w_up[o], gla_b[o])
            mix_p, (s_ret, s_gla) = _mixer_cd_core(zp, zero_ret, zero_gla, *cd, None)
            mix_s, _ = _mixer_cd_core(zs, state_ret[:, o], state_gla[:, o], *cd, rope)
            st_ret.append(s_ret)
            st_gla.append(s_gla)
        mix = jnp.concatenate([mix_p.reshape(n_ctx, D), mix_s.reshape(Bs * Ts, D)], axis=0).astype(BF16)
        x = _outproj(x, mix, mod_l, w_out, row_of_tile, tm)
        x = _ffn(x, mod_l, nw, wg[1], wu[1], wd[1], 2, row_of_tile, tm)

    y = _final_norm(x, final_norm_w)
    y_prompt = y[:n_ctx].reshape(Bp, Tp, D)
    y_sample = y[n_ctx:].reshape(Bs, Ts, D)
    return (y_prompt, y_sample,
            jnp.stack(st_lru, axis=1), jnp.stack(st_C, axis=1), jnp.stack(st_n, axis=1), jnp.stack(st_m, axis=1),
            jnp.stack(st_ret, axis=1), jnp.stack(st_gla, axis=1))
```

```python
from functools import partial

import jax
import jax.numpy as jnp
from jax import lax
from jax.experimental import pallas as pl
from jax.experimental.pallas import tpu as pltpu

F32 = jnp.float32
BF16 = jnp.bfloat16

D_MODEL = 2048
DEPTH = 4
N_MOD = 9
D_FF = 5632
EPS = 1e-6
CHUNK = 64
CONV_W = 4
D_A = D_MODEL // 2
NB_A = 8
BS_A = D_A // NB_A
LRU_C = 8.0
H_B = 4
DK_B = D_MODEL // 8
DV_B = D_MODEL // 8
H_C = 4
DK_C = D_MODEL // 16
DV_C = D_MODEL // 8
H_D = 4
DK_D = D_MODEL // 16
DV_D = D_MODEL // 8
GLA_RANK = 16
GLA_TAU = 16.0
ROPE_BASE = 10000.0
GRID_W = 64

MOD_ROWS = 8
LANE = 128
SUBLANE = 8
SUB = 16
VMEM_LIMIT = 56 * 1024 * 1024
NEG_INF = float("-inf")
HIGHEST = lax.Precision.HIGHEST
NT = (((1,), (1,)), ((), ()))
TN = (((0,), (0,)), ((), ()))


def _params(sem):
    return pltpu.CompilerParams(dimension_semantics=sem, vmem_limit_bytes=VMEM_LIMIT)


def _silu(x):
    return x * jax.nn.sigmoid(x)


def _rows(i, n):
    return pl.ds(pl.multiple_of(i * n, n), n)


def _mod_kernel(c_ref, w_ref, b_ref, o_ref):
    s = _silu(c_ref[...]).astype(BF16)
    o_ref[...] = jnp.dot(s, w_ref[...].astype(BF16), preferred_element_type=F32) + b_ref[...]


def _modulation(cond, w_mod, b_mod, tn=1024):
    n = w_mod.shape[-1]
    return pl.pallas_call(
        _mod_kernel,
        grid=(DEPTH, n // tn),
        in_specs=[
            pl.BlockSpec((MOD_ROWS, D_MODEL), lambda l, j: (0, 0)),
            pl.BlockSpec((None, D_MODEL, tn), lambda l, j: (l, 0, j)),
            pl.BlockSpec((None, 1, tn), lambda l, j: (l, 0, j)),
        ],
        out_specs=pl.BlockSpec((None, MOD_ROWS, tn), lambda l, j: (l, 0, j)),
        out_shape=jax.ShapeDtypeStruct((DEPTH, MOD_ROWS, n), F32),
        compiler_params=_params(("arbitrary", "arbitrary")),
        name="modulation",
    )(cond, w_mod, b_mod.reshape(DEPTH, 1, n))


def _norm_mod(x, nw, shift, scale):
    ms = jnp.mean(x * x, axis=-1, keepdims=True)
    y = x * lax.rsqrt(ms + EPS) * nw
    return y * (1.0 + scale) + shift


def _mod_row_map(n_ctx_tiles, tiles_per_req):
    def row(i):
        return jnp.where(i < n_ctx_tiles, 0, 1 + (i - n_ctx_tiles) // tiles_per_req)
    return row


def _ffn_kernel(x_ref, mod_ref, nw_ref, wg_ref, wu_ref, wd_ref, o_ref, h_scr, *, k, nf):
    f = pl.program_id(1)

    @pl.when(f == 0)
    def _():
        h = _norm_mod(x_ref[...], nw_ref[...], mod_ref[3 * k:3 * k + 1, :], mod_ref[3 * k + 1:3 * k + 2, :])
        h_scr[...] = h.astype(BF16)
        o_ref[...] = jnp.zeros_like(o_ref)

    h = h_scr[...]
    g = jnp.dot(h, wg_ref[...], preferred_element_type=F32)
    u = jnp.dot(h, wu_ref[...], preferred_element_type=F32)
    a = (_silu(g) * u).astype(BF16)
    o_ref[...] += jnp.dot(a, wd_ref[...], preferred_element_type=F32)

    @pl.when(f == nf - 1)
    def _():
        o_ref[...] = x_ref[...] + (0.5 * mod_ref[3 * k + 2:3 * k + 3, :]) * o_ref[...]


def _ffn(x, mod_l, nw, wg, wu, wd, k, row_of_tile, tm, tf=512):
    t = x.shape[0]
    nf = D_FF // tf
    return pl.pallas_call(
        partial(_ffn_kernel, k=k, nf=nf),
        grid=(t // tm, nf),
        in_specs=[
            pl.BlockSpec((tm, D_MODEL), lambda i, f: (i, 0)),
            pl.BlockSpec((None, N_MOD, D_MODEL), lambda i, f: (row_of_tile(i), 0, 0)),
            pl.BlockSpec((None, 1, D_MODEL), lambda i, f: (k, 0, 0)),
            pl.BlockSpec((D_MODEL, tf), lambda i, f: (0, f)),
            pl.BlockSpec((D_MODEL, tf), lambda i, f: (0, f)),
            pl.BlockSpec((tf, D_MODEL), lambda i, f: (f, 0)),
        ],
        out_specs=pl.BlockSpec((tm, D_MODEL), lambda i, f: (i, 0)),
        out_shape=jax.ShapeDtypeStruct((t, D_MODEL), F32),
        scratch_shapes=[pltpu.VMEM((tm, D_MODEL), BF16)],
        compiler_params=_params(("arbitrary", "arbitrary")),
        name="ffn",
    )(x, mod_l, nw, wg, wu, wd)


def _inproj_kernel(x_ref, mod_ref, nw_ref, w_ref, wt_ref, z_ref, zt_ref, h_scr):
    j = pl.program_id(1)

    @pl.when(j == 0)
    def _():
        h = _norm_mod(x_ref[...], nw_ref[...], mod_ref[3:4, :], mod_ref[4:5, :])
        h_scr[...] = h.astype(BF16)
        zt_ref[...] = jnp.dot(h_scr[...], wt_ref[...], preferred_element_type=F32)

    z_ref[...] = jnp.dot(h_scr[...], w_ref[...], preferred_element_type=F32)


def _inproj(x, mod_l, nw, w_main, w_tail, row_of_tile, tm, tn=1024):
    t = x.shape[0]
    n = w_main.shape[1]
    tw = w_tail.shape[1]
    return pl.pallas_call(
        _inproj_kernel,
        grid=(t // tm, n // tn),
        in_specs=[
            pl.BlockSpec((tm, D_MODEL), lambda i, j: (i, 0)),
            pl.BlockSpec((None, N_MOD, D_MODEL), lambda i, j: (row_of_tile(i), 0, 0)),
            pl.BlockSpec((None, 1, D_MODEL), lambda i, j: (1, 0, 0)),
            pl.BlockSpec((D_MODEL, tn), lambda i, j: (0, j)),
            pl.BlockSpec((D_MODEL, tw), lambda i, j: (0, 0)),
        ],
        out_specs=[
            pl.BlockSpec((tm, tn), lambda i, j: (i, j)),
            pl.BlockSpec((tm, tw), lambda i, j: (i, 0)),
        ],
        out_shape=[jax.ShapeDtypeStruct((t, n), F32), jax.ShapeDtypeStruct((t, tw), F32)],
        scratch_shapes=[pltpu.VMEM((tm, D_MODEL), BF16)],
        compiler_params=_params(("arbitrary", "arbitrary")),
        name="inproj",
    )(x, mod_l, nw, w_main, w_tail)


def _outproj_kernel(x_ref, ma_ref, mb_ref, mod_ref, wa_ref, wb_ref, o_ref):
    y = jnp.dot(ma_ref[...], wa_ref[...], preferred_element_type=F32)
    y = y + jnp.dot(mb_ref[...], wb_ref[...], preferred_element_type=F32)
    o_ref[...] = x_ref[...] + mod_ref[5:6, :] * y


def _outproj(x, mix_a, mix_b, mod_l, w_out, row_of_tile, tm):
    t = x.shape[0]
    half = D_MODEL // 2
    return pl.pallas_call(
        _outproj_kernel,
        grid=(t // tm,),
        in_specs=[
            pl.BlockSpec((tm, D_MODEL), lambda i: (i, 0)),
            pl.BlockSpec((tm, half), lambda i: (i, 0)),
            pl.BlockSpec((tm, half), lambda i: (i, 0)),
            pl.BlockSpec((None, N_MOD, D_MODEL), lambda i: (row_of_tile(i), 0, 0)),
            pl.BlockSpec((half, D_MODEL), lambda i: (0, 0)),
            pl.BlockSpec((half, D_MODEL), lambda i: (1, 0)),
        ],
        out_specs=pl.BlockSpec((tm, D_MODEL), lambda i: (i, 0)),
        out_shape=jax.ShapeDtypeStruct((t, D_MODEL), F32),
        compiler_params=_params(("arbitrary",)),
        name="outproj",
    )(x, mix_a, mix_b, mod_l, w_out, w_out)


def _final_norm_kernel(x_ref, w_ref, o_ref):
    x = x_ref[...]
    ms = jnp.mean(x * x, axis=-1, keepdims=True)
    o_ref[...] = x * lax.rsqrt(ms + EPS) * w_ref[...]


def _final_norm(x, w, tm=512):
    t = x.shape[0]
    return pl.pallas_call(
        _final_norm_kernel,
        grid=(t // tm,),
        in_specs=[pl.BlockSpec((tm, D_MODEL), lambda i: (i, 0)), pl.BlockSpec((1, D_MODEL), lambda i: (0, 0))],
        out_specs=pl.BlockSpec((tm, D_MODEL), lambda i: (i, 0)),
        out_shape=jax.ShapeDtypeStruct((t, D_MODEL), F32),
        compiler_params=_params(("arbitrary",)),
        name="final_norm",
    )(x, w.reshape(1, D_MODEL))


def _finalize_heads(acc_scr, gate_ref, mix_ref, t):
    tile = min(t, 256)

    def body(i, carry):
        r = _rows(i, tile)
        o = acc_scr[r, :]
        d = o - jnp.mean(o, axis=-1, keepdims=True)
        var = jnp.mean(d * d, axis=-1, keepdims=True)
        mix_ref[r, :] = (d * lax.rsqrt(var + EPS) * _silu(gate_ref[r, :])).astype(BF16)
        return carry

    lax.fori_loop(0, t // tile, body, 0)


def _dwconv_to(dst_ref, src_ref, pad_scr, w_ref, b_ref, t, post):
    c = src_ref.shape[1]
    pad_scr[pl.ds(0, SUBLANE), :] = jnp.zeros((SUBLANE, c), F32)
    pad_scr[pl.ds(t + SUBLANE, SUBLANE), :] = jnp.zeros((SUBLANE, c), F32)
    pad_scr[pl.ds(SUBLANE, t), :] = src_ref[...]
    tile = min(t, 256)
    for r0 in range(0, t, tile):
        y = b_ref[...]
        for k in range(CONV_W):
            y = y + w_ref[k:k + 1, :] * pad_scr[pl.ds(r0 + SUBLANE - CONV_W // 2 + k, tile), :]
        dst_ref[pl.ds(r0, tile), :] = post(y).astype(dst_ref.dtype)


def _tri(reverse):
    row = lax.broadcasted_iota(jnp.int32, (CHUNK, CHUNK), 0)
    col = lax.broadcasted_iota(jnp.int32, (CHUNK, CHUNK), 1)
    return jnp.where((col >= row) if reverse else (col <= row), 1.0, 0.0).astype(F32)


def _scan_block(a, b, rowid, reverse):
    for s in (1, 2, 4):
        if reverse:
            valid = rowid < SUBLANE - s
            shift = SUBLANE - s
        else:
            valid = rowid >= s
            shift = s
        a_sh = jnp.where(valid, pltpu.roll(a, shift, 0), 1.0)
        b_sh = jnp.where(valid, pltpu.roll(b, shift, 0), 0.0)
        b = b + a * b_sh
        a = a * a_sh
    return a, b


def _lru_kernel(*refs, t, cb, has_init):
    it = iter(refs)
    x_ref, y_ref, cw_ref, cbias_ref, wa_ref, ba_ref, wi_ref, bi_ref, lam_ref = (next(it) for _ in range(9))
    h0_ref = next(it) if has_init else None
    out_ref, sout_ref, pad_scr, xc_scr, a_scr, b_scr = (next(it) for _ in range(6))
    nb = cb // BS_A

    _dwconv_to(xc_scr, x_ref, pad_scr, cw_ref, cbias_ref, t, lambda v: v)

    lam = lam_ref[...]
    sp = jnp.maximum(-lam, 0.0) + jnp.log1p(jnp.exp(-jnp.abs(lam)))
    tile = min(t, 256)

    def gates(i, carry):
        r = _rows(i, tile)
        xc = xc_scr[r, :]
        xcb = xc.astype(BF16)
        for d in range(2):
            ra = jnp.concatenate([jnp.dot(xcb[:, n * BS_A:(n + 1) * BS_A], wa_ref[d, n], preferred_element_type=F32)
                                  for n in range(nb)], axis=1) + ba_ref[d:d + 1, :]
            ia = jnp.concatenate([jnp.dot(xcb[:, n * BS_A:(n + 1) * BS_A], wi_ref[d, n], preferred_element_type=F32)
                                  for n in range(nb)], axis=1) + bi_ref[d:d + 1, :]
            log_a = -LRU_C * jax.nn.sigmoid(ra) * sp[d:d + 1, :]
            th = jnp.tanh(log_a)
            mult = jnp.sqrt(-2.0 * th / (1.0 - th))
            a_scr[d, r, :] = jnp.exp(log_a)
            b_scr[d, r, :] = mult * jax.nn.sigmoid(ia) * xc
        return carry

    lax.fori_loop(0, t // tile, gates, 0)

    rowid = lax.broadcasted_iota(jnp.int32, (SUBLANE, cb), 0)
    nblk = t // SUBLANE

    def scan(i, carry):
        hf, hb = carry
        rf = _rows(i, SUBLANE)
        rb = _rows(nblk - 1 - i, SUBLANE)
        af, bf = _scan_block(a_scr[0, rf, :], b_scr[0, rf, :], rowid, False)
        ab, bb = _scan_block(a_scr[1, rb, :], b_scr[1, rb, :], rowid, True)
        hf_blk = bf + af * hf
        hb_blk = bb + ab * hb
        a_scr[0, rf, :] = hf_blk
        a_scr[1, rb, :] = hb_blk
        return hf_blk[SUBLANE - 1:SUBLANE, :], hb_blk[0:1, :]

    if has_init:
        init = (h0_ref[0:1, :], h0_ref[1:2, :])
    else:
        init = (jnp.zeros((1, cb), F32), jnp.zeros((1, cb), F32))
    hf, hb = lax.fori_loop(0, nblk, scan, init, unroll=2)
    sout_ref[0:1, :] = hf
    sout_ref[1:2, :] = hb

    def fin(i, carry):
        r = _rows(i, tile)
        out_ref[r, :] = ((a_scr[0, r, :] + a_scr[1, r, :]) * jax.nn.gelu(y_ref[r, :])).astype(BF16)
        return carry

    lax.fori_loop(0, t // tile, fin, 0)


def _lru(z, base, nseq, t, h0, conv_w, conv_b, w_a, b_a, w_i, b_i, lam, cb=256):
    nj = D_A // cb
    nb = cb // BS_A
    has_init = h0 is not None
    in_specs = [
        pl.BlockSpec((t, cb), lambda s, j: (base + s, j)),
        pl.BlockSpec((t, cb), lambda s, j: (base + s, nj + j)),
        pl.BlockSpec((CONV_W, cb), lambda s, j: (0, j)),
        pl.BlockSpec((1, cb), lambda s, j: (0, j)),
        pl.BlockSpec((2, nb, BS_A, BS_A), lambda s, j: (0, j, 0, 0)),
        pl.BlockSpec((2, cb), lambda s, j: (0, j)),
        pl.BlockSpec((2, nb, BS_A, BS_A), lambda s, j: (0, j, 0, 0)),
        pl.BlockSpec((2, cb), lambda s, j: (0, j)),
        pl.BlockSpec((2, cb), lambda s, j: (0, j)),
    ]
    args = [z, z, conv_w, conv_b.reshape(1, D_A), w_a.astype(BF16), b_a, w_i.astype(BF16), b_i, lam]
    if has_init:
        in_specs.append(pl.BlockSpec((None, 2, cb), lambda s, j: (s, 0, j)))
        args.append(h0)
    return pl.pallas_call(
        partial(_lru_kernel, t=t, cb=cb, has_init=has_init),
        grid=(nseq, nj),
        in_specs=in_specs,
        out_specs=[
            pl.BlockSpec((t, cb), lambda s, j: (s, j)),
            pl.BlockSpec((None, 2, cb), lambda s, j: (s, 0, j)),
        ],
        out_shape=[jax.ShapeDtypeStruct((nseq * t, D_A), BF16), jax.ShapeDtypeStruct((nseq, 2, D_A), F32)],
        scratch_shapes=[
            pltpu.VMEM((t + 2 * SUBLANE, cb), F32),
            pltpu.VMEM((t, cb), F32),
            pltpu.VMEM((2, t, cb), F32),
            pltpu.VMEM((2, t, cb), F32),
        ],
        compiler_params=_params(("arbitrary", "arbitrary")),
        name="rglru",
    )(*args)


def _mlstm_kernel(*refs, t, has_init):
    it = iter(refs)
    q_ref, k_ref, v_ref, og_ref, zt_ref, bt_ref, cwq_ref, cbq_ref, cwk_ref, cbk_ref = (next(it) for _ in range(10))
    if has_init:
        c0_ref, n0_ref, m0_ref = next(it), next(it), next(it)
    mix_ref, cout_ref, nout_ref, mout_ref = (next(it) for _ in range(4))
    pad_scr, q_scr, k_scr, gate_scr, acc_scr, c_scr, n_scr = (next(it) for _ in range(7))
    nchunk = t // CHUNK

    _dwconv_to(q_scr, q_ref, pad_scr, cwq_ref, cbq_ref, t, lambda v: _silu(v) * (DK_B ** -0.5))
    _dwconv_to(k_scr, k_ref, pad_scr, cwk_ref, cbk_ref, t, _silu)

    tile = min(t, 256)
    lane_t = lax.broadcasted_iota(jnp.int32, (tile, LANE), 1)

    def gates(i, carry):
        r = _rows(i, tile)
        x = zt_ref[r, :] + bt_ref[...]
        logsig = jnp.minimum(x, 0.0) - jnp.log1p(jnp.exp(-jnp.abs(x)))
        gate_scr[r, :] = jnp.where(lane_t < 2, x, logsig)
        acc_scr[r, :] = jnp.zeros((tile, DV_B), F32)
        return carry

    lax.fori_loop(0, t // tile, gates, 0)

    if has_init:
        c_scr[...] = c0_ref[...]
        n_scr[...] = n0_ref[...]
        m_init = (m0_ref[0, :, 0:1], m0_ref[1, :, 0:1])
    else:
        c_scr[...] = jnp.zeros_like(c_scr)
        n_scr[...] = jnp.zeros_like(n_scr)
        m_init = (jnp.zeros((1, 1), F32), jnp.zeros((1, 1), F32))

    row = lax.broadcasted_iota(jnp.int32, (CHUNK, CHUNK), 0)
    col = lax.broadcasted_iota(jnp.int32, (CHUNK, CHUNK), 1)
    lane_c = lax.broadcasted_iota(jnp.int32, (CHUNK, LANE), 1)

    def chunk(cidx, d, m):
        reverse = d == 1
        li, lf = d, 2 + d
        last = 0 if reverse else CHUNK - 1
        r = _rows(cidx, CHUNK)
        gt = gate_scr[r, :]
        fsum = jnp.dot(_tri(reverse), gt, precision=HIGHEST, preferred_element_type=F32)
        y = jnp.where(lane_c == lf, fsum, gt)
        yt = y.T
        f_col, i_col = y[:, lf:lf + 1], y[:, li:li + 1]
        f_row, i_row = yt[lf:lf + 1, :], yt[li:li + 1, :]
        b = f_col + m
        mask = (col >= row) if reverse else (col <= row)
        dlog = jnp.where(mask, f_col - f_row + i_row, NEG_INF)
        m_t = jnp.maximum(b, jnp.max(dlog, axis=-1, keepdims=True))
        w = jnp.exp(dlog - m_t)
        inter = jnp.exp(b - m_t)
        q = q_scr[r, :]
        k32 = k_scr[r, :]
        v = v_ref[r, :].astype(BF16)
        s = lax.dot_general(q, k32.astype(BF16), NT, preferred_element_type=F32) * w
        c_old = c_scr[d]
        n_old = n_scr[d]
        num = jnp.dot(s.astype(BF16), v, preferred_element_type=F32)
        num = num + inter * jnp.dot(q, c_old.astype(BF16), preferred_element_type=F32)
        den = jnp.sum(s, axis=-1, keepdims=True) + inter * jnp.sum(q.astype(F32) * n_old, axis=-1, keepdims=True)
        acc_scr[r, :] += num / jnp.maximum(jnp.abs(den), jnp.exp(-m_t))
        m_new = m_t[last:last + 1, :]
        f_last = f_col[last:last + 1, :]
        kw = jnp.exp(f_last - f_col + i_col - m_new)
        decay = jnp.exp(f_last + m - m_new)
        kk = k32 * kw
        c_scr[d] = decay * c_old + lax.dot_general(kk.astype(BF16), v, TN, preferred_element_type=F32)
        n_scr[d] = decay * n_old + jnp.sum(kk, axis=0, keepdims=True)
        return m_new

    def body(c, carry):
        mf, mb = carry
        return chunk(c, 0, mf), chunk(nchunk - 1 - c, 1, mb)

    mf, mb = lax.fori_loop(0, nchunk, body, m_init)

    _finalize_heads(acc_scr, og_ref, mix_ref, t)
    cout_ref[...] = c_scr[...]
    nout_ref[...] = n_scr[...]
    mout_ref[0] = jnp.broadcast_to(mf, (1, LANE))
    mout_ref[1] = jnp.broadcast_to(mb, (1, LANE))


def _mlstm(z, zt, bias_t, base, nseq, t, init, conv_w, conv_b):
    has_init = init is not None
    qoff, koff, voff, goff = (2 * D_A // DK_B, 2 * D_A // DK_B + H_B, 2 * D_A // DK_B + 2 * H_B, 2 * D_A // DK_B + 3 * H_B)
    cb2 = conv_b.reshape(1, 2 * H_B * DK_B)
    in_specs = [
        pl.BlockSpec((t, DK_B), lambda s, h: (base + s, qoff + h)),
        pl.BlockSpec((t, DK_B), lambda s, h: (base + s, koff + h)),
        pl.BlockSpec((t, DV_B), lambda s, h: (base + s, voff + h)),
        pl.BlockSpec((t, DV_B), lambda s, h: (base + s, goff + h)),
        pl.BlockSpec((t, LANE), lambda s, h: (base + s, h)),
        pl.BlockSpec((1, LANE), lambda s, h: (0, h)),
        pl.BlockSpec((CONV_W, DK_B), lambda s, h: (0, h)),
        pl.BlockSpec((1, DK_B), lambda s, h: (0, h)),
        pl.BlockSpec((CONV_W, DK_B), lambda s, h: (0, H_B + h)),
        pl.BlockSpec((1, DK_B), lambda s, h: (0, H_B + h)),
    ]
    args = [z, z, z, z, zt, bias_t, conv_w, cb2, conv_w, cb2]
    if has_init:
        c0, n0, m0 = init
        in_specs += [
            pl.BlockSpec((None, 2, None, DK_B, DV_B), lambda s, h: (s, 0, h, 0, 0)),
            pl.BlockSpec((None, 2, None, 1, DK_B), lambda s, h: (s, 0, h, 0, 0)),
            pl.BlockSpec((None, 2, None, 1, LANE), lambda s, h: (s, 0, h, 0, 0)),
        ]
        args += [c0, n0.reshape(nseq, 2, H_B, 1, DK_B),
                 jnp.broadcast_to(m0[..., None, None], (nseq, 2, H_B, 1, LANE))]
    return pl.pallas_call(
        partial(_mlstm_kernel, t=t, has_init=has_init),
        grid=(nseq, H_B),
        in_specs=in_specs,
        out_specs=[
            pl.BlockSpec((t, DV_B), lambda s, h: (s, h)),
            pl.BlockSpec((None, 2, None, DK_B, DV_B), lambda s, h: (s, 0, h, 0, 0)),
            pl.BlockSpec((None, 2, None, 1, DK_B), lambda s, h: (s, 0, h, 0, 0)),
            pl.BlockSpec((None, 2, None, 1, LANE), lambda s, h: (s, 0, h, 0, 0)),
        ],
        out_shape=[
            jax.ShapeDtypeStruct((nseq * t, H_B * DV_B), BF16),
            jax.ShapeDtypeStruct((nseq, 2, H_B, DK_B, DV_B), F32),
            jax.ShapeDtypeStruct((nseq, 2, H_B, 1, DK_B), F32),
            jax.ShapeDtypeStruct((nseq, 2, H_B, 1, LANE), F32),
        ],
        scratch_shapes=[
            pltpu.VMEM((t + 2 * SUBLANE, DK_B), F32),
            pltpu.VMEM((t, DK_B), BF16),
            pltpu.VMEM((t, DK_B), F32),
            pltpu.VMEM((t, LANE), F32),
            pltpu.VMEM((t, DV_B), F32),
            pltpu.VMEM((2, DK_B, DV_B), F32),
            pltpu.VMEM((2, 1, DK_B), F32),
        ],
        compiler_params=_params(("arbitrary", "arbitrary")),
        name="mlstm",
    )(*args)


def _ret_kernel(*refs, t, use_rope, has_init):
    it = iter(refs)
    lg_ref, q_ref, k_ref, v_ref, g_ref = (next(it) for _ in range(5))
    if use_rope:
        cos_ref, sin_ref = next(it), next(it)
    s0_ref = next(it) if has_init else None
    mix_ref, sout_ref, acc_scr, st_scr = (next(it) for _ in range(4))
    nchunk = t // CHUNK
    h = pl.program_id(1)

    row = lax.broadcasted_iota(jnp.int32, (CHUNK, CHUNK), 0)
    col = lax.broadcasted_iota(jnp.int32, (CHUNK, CHUNK), 1)
    dist = (row - col).astype(F32)
    pos = lax.broadcasted_iota(jnp.int32, (CHUNK, DK_C), 0).astype(F32)
    consts = []
    for d in range(2):
        lg = lg_ref[d, h]
        if d == 0:
            dm = jnp.where(dist >= 0, jnp.exp(jnp.maximum(dist, 0.0) * lg), 0.0)
            qin = jnp.exp((pos + 1.0) * lg)
            kout = jnp.exp((CHUNK - 1.0 - pos) * lg)
        else:
            dm = jnp.where(dist <= 0, jnp.exp(jnp.maximum(-dist, 0.0) * lg), 0.0)
            qin = jnp.exp((CHUNK - pos) * lg)
            kout = jnp.exp(pos * lg)
        gch = jnp.exp(jnp.full((1, DV_C), float(CHUNK), F32) * lg)
        consts.append((dm, qin, kout, gch))

    acc_scr[...] = jnp.zeros_like(acc_scr)
    if has_init:
        st_scr[...] = s0_ref[...]
    else:
        st_scr[...] = jnp.zeros_like(st_scr)

    def chunk(cidx, d):
        dm, qin, kout, gch = consts[d]
        r = _rows(cidx, CHUNK)
        q = q_ref[r, :]
        k = k_ref[r, :] * (DK_C ** -0.5)
        if use_rope:
            cs, sn = cos_ref[r, :], sin_ref[r, :]
            q = q * cs + pltpu.roll(q, DK_C // 2, 1) * sn
            k = k * cs + pltpu.roll(k, DK_C // 2, 1) * sn
        v = v_ref[r, :].astype(BF16)
        st = st_scr[d]
        s = lax.dot_general(q.astype(BF16), k.astype(BF16), NT, preferred_element_type=F32) * dm
        o = jnp.dot(s.astype(BF16), v, preferred_element_type=F32)
        o = o + jnp.dot((q * qin).astype(BF16), st.astype(BF16), preferred_element_type=F32)
        st_scr[d] = gch * st + lax.dot_general((k * kout).astype(BF16), v, TN, preferred_element_type=F32)
        acc_scr[r, :] += o

    def body(c, carry):
        chunk(c, 0)
        chunk(nchunk - 1 - c, 1)
        return carry

    lax.fori_loop(0, nchunk, body, 0)
    _finalize_heads(acc_scr, g_ref, mix_ref, t)
    sout_ref[...] = st_scr[...]


def _retention(z, base, nseq, t, s0, lg, rope):
    has_init = s0 is not None
    use_rope = rope is not None
    nq = DK_C // LANE
    qoff, koff = 0, H_C * nq
    voff = 2 * H_C * DK_C // DV_C
    goff = voff + H_C
    in_specs = [
        pl.BlockSpec(memory_space=pltpu.SMEM),
        pl.BlockSpec((t, DK_C), lambda s, h: (base + s, qoff + h)),
        pl.BlockSpec((t, DK_C), lambda s, h: (base + s, koff + h)),
        pl.BlockSpec((t, DV_C), lambda s, h: (base + s, voff + h)),
        pl.BlockSpec((t, DV_C), lambda s, h: (base + s, goff + h)),
    ]
    args = [lg, z, z, z, z]
    if use_rope:
        in_specs += [pl.BlockSpec((t, DK_C), lambda s, h: (0, 0)), pl.BlockSpec((t, DK_C), lambda s, h: (0, 0))]
        args += list(rope)
    if has_init:
        in_specs.append(pl.BlockSpec((None, 2, None, DK_C, DV_C), lambda s, h: (s, 0, h, 0, 0)))
        args.append(s0)
    return pl.pallas_call(
        partial(_ret_kernel, t=t, use_rope=use_rope, has_init=has_init),
        grid=(nseq, H_C),
        in_specs=in_specs,
        out_specs=[
            pl.BlockSpec((t, DV_C), lambda s, h: (s, h)),
            pl.BlockSpec((None, 2, None, DK_C, DV_C), lambda s, h: (s, 0, h, 0, 0)),
        ],
        out_shape=[
            jax.ShapeDtypeStruct((nseq * t, H_C * DV_C), BF16),
            jax.ShapeDtypeStruct((nseq, 2, H_C, DK_C, DV_C), F32),
        ],
        scratch_shapes=[pltpu.VMEM((t, DV_C), F32), pltpu.VMEM((2, DK_C, DV_C), F32)],
        compiler_params=_params(("arbitrary", "arbitrary")),
        name="retention",
    )(*args)


def _gla_intra(q, k, g, ones, reverse):
    nsub = CHUNK // SUB
    lane = lax.broadcasted_iota(jnp.int32, (SUB, CHUNK), 1)
    rowi = lax.broadcasted_iota(jnp.int32, (SUB, CHUNK), 0)
    out = []
    for jb in range(nsub):
        lo, hi = jb * SUB, (jb + 1) * SUB
        qj, gj = q[lo:hi], g[lo:hi]
        if reverse and jb < nsub - 1:
            gref = g[hi:hi + 1]
            kt = (k[hi:] * jnp.exp(gref - g[hi:])).astype(BF16)
            kt = jnp.concatenate([jnp.zeros((hi, DK_D), BF16), kt], axis=0)
        elif not reverse and jb > 0:
            gref = g[lo - 1:lo]
            kt = (k[:lo] * jnp.exp(gref - g[:lo])).astype(BF16)
            kt = jnp.concatenate([kt, jnp.zeros((CHUNK - lo, DK_D), BF16)], axis=0)
        else:
            kt = None
        if kt is None:
            off = jnp.zeros((SUB, CHUNK), F32)
        else:
            qt = (qj * jnp.exp(gj - gref)).astype(BF16)
            off = lax.dot_general(qt, kt, NT, preferred_element_type=F32)
        prods = [qj * jnp.exp(jnp.minimum(gj - g[i:i + 1], 0.0)) * k[i:i + 1] for i in range(lo, hi)]
        sums = jnp.dot(jnp.concatenate(prods, axis=0).astype(BF16), ones, preferred_element_type=F32)
        diag = jnp.zeros((SUB, CHUNK), F32)
        for ii in range(SUB):
            diag = jnp.where(lane == lo + ii, sums[ii * SUB:(ii + 1) * SUB, :CHUNK], diag)
        mask = (rowi + lo <= lane) if reverse else (rowi + lo >= lane)
        out.append(off + jnp.where(mask, diag, 0.0))
    return jnp.concatenate(out, axis=0)


def _gla_kernel(*refs, t, has_init):
    it = iter(refs)
    q_ref, k_ref, v_ref, r_ref, a_ref, wup_ref, gb_ref = (next(it) for _ in range(7))
    s0_ref = next(it) if has_init else None
    mix_ref, sout_ref, acc_scr, st_scr, g_scr = (next(it) for _ in range(5))
    nchunk = t // CHUNK
    tile = min(t, 256)

    def gates(i, carry):
        r = _rows(i, tile)
        a = a_ref[r, :].astype(BF16)
        for d in range(2):
            x = jnp.dot(a, wup_ref[d], preferred_element_type=F32) + gb_ref[d]
            g_scr[d, r, :] = (jnp.minimum(x, 0.0) - jnp.log1p(jnp.exp(-jnp.abs(x)))) * (1.0 / GLA_TAU)
        acc_scr[r, :] = jnp.zeros((tile, DV_D), F32)
        return carry

    lax.fori_loop(0, t // tile, gates, 0)

    for d in range(2):
        st_scr[d] = s0_ref[d].T if has_init else jnp.zeros((DV_D, DK_D), F32)

    ones = jnp.ones((DK_D, LANE), BF16)

    def chunk(cidx, d):
        reverse = d == 1
        last = 0 if reverse else CHUNK - 1
        r = _rows(cidx, CHUNK)
        g = jnp.dot(_tri(reverse), g_scr[d, r, :], precision=HIGHEST, preferred_element_type=F32)
        g_last = g[last:last + 1, :]
        q = q_ref[r, :] * (DK_D ** -0.5)
        k = k_ref[r, :]
        v = v_ref[r, :].astype(BF16)
        st = st_scr[d]
        s = _gla_intra(q, k, g, ones, reverse)
        o = jnp.dot(s.astype(BF16), v, preferred_element_type=F32)
        o = o + lax.dot_general((q * jnp.exp(g)).astype(BF16), st.astype(BF16), NT, preferred_element_type=F32)
        kd = (k * jnp.exp(g_last - g)).astype(BF16)
        st_scr[d] = jnp.exp(g_last) * st + lax.dot_general(v, kd, TN, preferred_element_type=F32)
        acc_scr[r, :] += o

    def body(c, carry):
        chunk(c, 0)
        chunk(nchunk - 1 - c, 1)
        return carry

    lax.fori_loop(0, nchunk, body, 0)
    _finalize_heads(acc_scr, r_ref, mix_ref, t)
    for d in range(2):
        sout_ref[d] = st_scr[d].T


def _gla(z, zt, base, nseq, t, s0, w_up, g_bias):
    has_init = s0 is not None
    nq = DK_D // LANE
    cbase = (2 * H_C * DK_C + 2 * H_C * DV_C)
    qoff = cbase // DK_D
    koff = qoff + H_D * nq
    voff = (cbase + 2 * H_D * DK_D) // DV_D
    roff = voff + H_D
    in_specs = [
        pl.BlockSpec((t, DK_D), lambda s, h: (base + s, qoff + h)),
        pl.BlockSpec((t, DK_D), lambda s, h: (base + s, koff + h)),
        pl.BlockSpec((t, DV_D), lambda s, h: (base + s, voff + h)),
        pl.BlockSpec((t, DV_D), lambda s, h: (base + s, roff + h)),
        pl.BlockSpec((t, LANE), lambda s, h: (base + s, 0)),
        pl.BlockSpec((2, None, LANE, DK_D), lambda s, h: (0, h, 0, 0)),
        pl.BlockSpec((2, None, 1, DK_D), lambda s, h: (0, h, 0, 0)),
    ]
    args = [z, z, z, z, zt, w_up, g_bias]
    if has_init:
        in_specs.append(pl.BlockSpec((None, 2, None, DK_D, DV_D), lambda s, h: (s, 0, h, 0, 0)))
        args.append(s0)
    return pl.pallas_call(
        partial(_gla_kernel, t=t, has_init=has_init),
        grid=(nseq, H_D),
        in_specs=in_specs,
        out_specs=[
            pl.BlockSpec((t, DV_D), lambda s, h: (s, h)),
            pl.BlockSpec((None, 2, None, DK_D, DV_D), lambda s, h: (s, 0, h, 0, 0)),
        ],
        out_shape=[
            jax.ShapeDtypeStruct((nseq * t, H_D * DV_D), BF16),
            jax.ShapeDtypeStruct((nseq, 2, H_D, DK_D, DV_D), F32),
        ],
        scratch_shapes=[
            pltpu.VMEM((t, DV_D), F32),
            pltpu.VMEM((2, DV_D, DK_D), F32),
            pltpu.VMEM((2, t, DK_D), F32),
        ],
        compiler_params=_params(("arbitrary", "arbitrary")),
        name="gla",
    )(*args)


def _rope_tables(t):
    pos = jnp.arange(t)
    row = (pos // GRID_W).astype(F32)
    col = (pos % GRID_W).astype(F32)
    nf = DK_C // 4
    freqs = ROPE_BASE ** (-jnp.arange(nf, dtype=F32) / nf)
    ang = jnp.concatenate([row[:, None] * freqs, col[:, None] * freqs], axis=-1)
    cos, sin = jnp.cos(ang), jnp.sin(ang)
    return jnp.concatenate([cos, cos], axis=-1), jnp.concatenate([-sin, sin], axis=-1)


def _ab_tail(w_in, i_bias, f_bias):
    n_main = w_in.shape[1] - 4 * H_B
    idx = jnp.array([[h, H_B + h, 2 * H_B + h, 3 * H_B + h] for h in range(H_B)])
    cols = w_in[:, n_main:][:, idx]
    w_tail = jnp.pad(cols, ((0, 0), (0, 0), (0, LANE - 4))).reshape(D_MODEL, H_B * LANE)
    bias = jnp.concatenate([i_bias.reshape(-1), f_bias.reshape(-1)])[idx]
    bias_t = jnp.pad(bias, ((0, 0), (0, LANE - 4))).reshape(1, H_B * LANE)
    return w_in[:, :n_main], w_tail, bias_t


def _cd_tail(w_in, w_up, g_bias):
    n_main = w_in.shape[1] - 2 * GLA_RANK
    w_tail = jnp.pad(w_in[:, n_main:], ((0, 0), (0, LANE - 2 * GLA_RANK)))
    up = w_up.reshape(2, GLA_RANK, H_D, DK_D).transpose(0, 2, 1, 3)
    up = jnp.stack([jnp.pad(up[d], ((0, 0), (d * GLA_RANK, LANE - (d + 1) * GLA_RANK), (0, 0))) for d in range(2)])
    return w_in[:, :n_main], w_tail, up.astype(BF16), g_bias.reshape(2, H_D, 1, DK_D)


def kernel(x_prompt, x_sample, c, state_lru, state_mlstm_C, state_mlstm_n, state_mlstm_m, state_ret, state_gla, c_ctx, w_mod, b_mod, norm_w, ffn_w_gate, ffn_w_up, ffn_w_down, w_in_ab, w_out_ab, lru_conv_w, lru_conv_b, lru_w_a, lru_b_a, lru_w_i, lru_b_i, lru_lambda, mlstm_conv_w, mlstm_conv_b, mlstm_i_bias, mlstm_f_bias, w_in_cd, w_out_cd, ret_decay_log, gla_w_up, gla_b, final_norm_w):
    Bp, Tp, D = x_prompt.shape
    Bs, Ts, _ = x_sample.shape
    n_ctx = Bp * Tp
    tm = 512
    assert n_ctx % tm == 0 and Ts % tm == 0 and 1 + Bs <= MOD_ROWS and n_ctx % Ts == 0
    row_of_tile = _mod_row_map(n_ctx // tm, Ts // tm)
    groups = ((0, Bp, Tp), (n_ctx // Ts, Bs, Ts))

    cond = jnp.concatenate([c_ctx[None], c, jnp.zeros((MOD_ROWS - 1 - Bs, D), F32)], axis=0)
    mod = _modulation(cond, w_mod, b_mod).reshape(DEPTH, MOD_ROWS, N_MOD, D)

    x = jnp.concatenate([x_prompt.reshape(n_ctx, D), x_sample.reshape(Bs * Ts, D)], axis=0)
    rope = _rope_tables(Ts)

    st_lru, st_C, st_n, st_m, st_ret, st_gla = [], [], [], [], [], []
    for l in range(DEPTH):
        mod_l = mod[l]
        nw = norm_w[l].reshape(3, 1, D)
        wg, wu, wd = ffn_w_gate[l].astype(BF16), ffn_w_up[l].astype(BF16), ffn_w_down[l].astype(BF16)
        x = _ffn(x, mod_l, nw, wg[0], wu[0], wd[0], 0, row_of_tile, tm)
        if l % 2 == 0:
            e = l // 2
            w_main, w_tail, bias_t = _ab_tail(w_in_ab[e], mlstm_i_bias[e], mlstm_f_bias[e])
            z, zt = _inproj(x, mod_l, nw, w_main.astype(BF16), w_tail.astype(BF16), row_of_tile, tm)
            lru_args = (lru_conv_w[e], lru_conv_b[e], lru_w_a[e], lru_b_a[e], lru_w_i[e], lru_b_i[e], lru_lambda[e])
            (pb, pn, pt), (sb, sn, st) = groups
            a_p, s_lru = _lru(z, pb, pn, pt, None, *lru_args)
            a_s, _ = _lru(z, sb, sn, st, state_lru[:, e], *lru_args)
            b_p, s_C, s_n, s_m = _mlstm(z, zt, bias_t, pb, pn, pt, None, mlstm_conv_w[e], mlstm_conv_b[e])
            b_s, _, _, _ = _mlstm(z, zt, bias_t, sb, sn, st,
                                  (state_mlstm_C[:, e], state_mlstm_n[:, e], state_mlstm_m[:, e]),
                                  mlstm_conv_w[e], mlstm_conv_b[e])
            st_lru.append(s_lru)
            st_C.append(s_C)
            st_n.append(s_n.reshape(Bp, 2, H_B, DK_B))
            st_m.append(s_m[:, :, :, 0, 0])
            mix_a = jnp.concatenate([a_p, a_s], axis=0)
            mix_b = jnp.concatenate([b_p, b_s], axis=0)
            w_out = w_out_ab[e].astype(BF16)
        else:
            o = l // 2
            w_main, w_tail, w_up, g_bias = _cd_tail(w_in_cd[o], gla_w_up[o], gla_b[o])
            z, zt = _inproj(x, mod_l, nw, w_main.astype(BF16), w_tail.astype(BF16), row_of_tile, tm)
            (pb, pn, pt), (sb, sn, st) = groups
            c_p, s_ret = _retention(z, pb, pn, pt, None, ret_decay_log[o], None)
            c_s, _ = _retention(z, sb, sn, st, state_ret[:, o], ret_decay_log[o], rope)
            d_p, s_gla = _gla(z, zt, pb, pn, pt, None, w_up, g_bias)
            d_s, _ = _gla(z, zt, sb, sn, st, state_gla[:, o], w_up, g_bias)
            st_ret.append(s_ret)
            st_gla.append(s_gla)
            mix_a = jnp.concatenate([c_p, c_s], axis=0)
            mix_b = jnp.concatenate([d_p, d_s], axis=0)
            w_out = w_out_cd[o].astype(BF16)
        x = _outproj(x, mix_a, mix_b, mod_l, w_out, row_of_tile, tm)
        x = _ffn(x, mod_l, nw, wg[1], wu[1], wd[1], 2, row_of_tile, tm)

    y = _final_norm(x, final_norm_w)
    y_prompt = y[:n_ctx].reshape(Bp, Tp, D)
    y_sample = y[n_ctx:].reshape(Bs, Ts, D)
    return (y_prompt, y_sample,
            jnp.stack(st_lru, axis=1), jnp.stack(st_C, axis=1), jnp.stack(st_n, axis=1), jnp.stack(st_m, axis=1),
            jnp.stack(st_ret, axis=1), jnp.stack(st_gla, axis=1))
```

```python
from functools import partial

import jax
import jax.numpy as jnp
from jax import lax
from jax.experimental import pallas as pl
from jax.experimental.pallas import tpu as pltpu

F32 = jnp.float32
BF16 = jnp.bfloat16

D_MODEL = 2048
DEPTH = 4
N_MOD = 9
D_FF = 5632
EPS = 1e-6
CHUNK = 64
CONV_W = 4
D_A = D_MODEL // 2
NB_A = 8
BS_A = D_A // NB_A
LRU_C = 8.0
H_B = 4
DK_B = D_MODEL // 8
DV_B = D_MODEL // 8
H_C = 4
DK_C = D_MODEL // 16
DV_C = D_MODEL // 8
H_D = 4
DK_D = D_MODEL // 16
DV_D = D_MODEL // 8
GLA_RANK = 16
GLA_TAU = 16.0
ROPE_BASE = 10000.0
GRID_W = 64

MOD_ROWS = 8
LANE = 128
SUBLANE = 8
SUB = 16
CHUNK_UNROLL = 4
VMEM_LIMIT = 56 * 1024 * 1024
NEG_INF = float("-inf")
HIGHEST = lax.Precision.HIGHEST
NT = (((1,), (1,)), ((), ()))
TN = (((0,), (0,)), ((), ()))


def _params(sem):
    return pltpu.CompilerParams(dimension_semantics=sem, vmem_limit_bytes=VMEM_LIMIT)


def _silu(x):
    return x * jax.nn.sigmoid(x)


def _rows(i, n):
    return pl.ds(pl.multiple_of(i * n, n), n)


def _mod_kernel(c_ref, w_ref, b_ref, o_ref):
    s = _silu(c_ref[...]).astype(BF16)
    o_ref[...] = jnp.dot(s, w_ref[...].astype(BF16), preferred_element_type=F32) + b_ref[...]


def _modulation(cond, w_mod, b_mod, tn=1024):
    n = w_mod.shape[-1]
    return pl.pallas_call(
        _mod_kernel,
        grid=(DEPTH, n // tn),
        in_specs=[
            pl.BlockSpec((MOD_ROWS, D_MODEL), lambda l, j: (0, 0)),
            pl.BlockSpec((None, D_MODEL, tn), lambda l, j: (l, 0, j)),
            pl.BlockSpec((None, 1, tn), lambda l, j: (l, 0, j)),
        ],
        out_specs=pl.BlockSpec((None, MOD_ROWS, tn), lambda l, j: (l, 0, j)),
        out_shape=jax.ShapeDtypeStruct((DEPTH, MOD_ROWS, n), F32),
        compiler_params=_params(("arbitrary", "arbitrary")),
        name="modulation",
    )(cond, w_mod, b_mod.reshape(DEPTH, 1, n))


def _norm_mod(x, nw, shift, scale):
    ms = jnp.mean(x * x, axis=-1, keepdims=True)
    y = x * lax.rsqrt(ms + EPS) * nw
    return y * (1.0 + scale) + shift


def _mod_row_map(n_ctx_tiles, tiles_per_req):
    def row(i):
        return jnp.where(i < n_ctx_tiles, 0, 1 + (i - n_ctx_tiles) // tiles_per_req)
    return row


def _ffn_kernel(x_ref, mod_ref, nw_ref, wg_ref, wu_ref, wd_ref, o_ref, h_scr, *, k, nf):
    f = pl.program_id(1)

    @pl.when(f == 0)
    def _():
        h = _norm_mod(x_ref[...], nw_ref[...], mod_ref[3 * k:3 * k + 1, :], mod_ref[3 * k + 1:3 * k + 2, :])
        h_scr[...] = h.astype(BF16)
        o_ref[...] = jnp.zeros_like(o_ref)

    h = h_scr[...]
    g = jnp.dot(h, wg_ref[...], preferred_element_type=F32)
    u = jnp.dot(h, wu_ref[...], preferred_element_type=F32)
    a = (_silu(g) * u).astype(BF16)
    o_ref[...] += jnp.dot(a, wd_ref[...], preferred_element_type=F32)

    @pl.when(f == nf - 1)
    def _():
        o_ref[...] = x_ref[...] + (0.5 * mod_ref[3 * k + 2:3 * k + 3, :]) * o_ref[...]


def _ffn(x, mod_l, nw, wg, wu, wd, k, row_of_tile, tm, tf=512):
    t = x.shape[0]
    nf = D_FF // tf
    return pl.pallas_call(
        partial(_ffn_kernel, k=k, nf=nf),
        grid=(t // tm, nf),
        in_specs=[
            pl.BlockSpec((tm, D_MODEL), lambda i, f: (i, 0)),
            pl.BlockSpec((None, N_MOD, D_MODEL), lambda i, f: (row_of_tile(i), 0, 0)),
            pl.BlockSpec((None, 1, D_MODEL), lambda i, f: (k, 0, 0)),
            pl.BlockSpec((D_MODEL, tf), lambda i, f: (0, f)),
            pl.BlockSpec((D_MODEL, tf), lambda i, f: (0, f)),
            pl.BlockSpec((tf, D_MODEL), lambda i, f: (f, 0)),
        ],
        out_specs=pl.BlockSpec((tm, D_MODEL), lambda i, f: (i, 0)),
        out_shape=jax.ShapeDtypeStruct((t, D_MODEL), F32),
        scratch_shapes=[pltpu.VMEM((tm, D_MODEL), BF16)],
        compiler_params=_params(("arbitrary", "arbitrary")),
        name="ffn",
    )(x, mod_l, nw, wg, wu, wd)


def _inproj_kernel(x_ref, mod_ref, nw_ref, w_ref, wt_ref, z_ref, zt_ref, h_scr):
    j = pl.program_id(1)

    @pl.when(j == 0)
    def _():
        h = _norm_mod(x_ref[...], nw_ref[...], mod_ref[3:4, :], mod_ref[4:5, :])
        h_scr[...] = h.astype(BF16)
        zt_ref[...] = jnp.dot(h_scr[...], wt_ref[...], preferred_element_type=F32)

    z_ref[...] = jnp.dot(h_scr[...], w_ref[...], preferred_element_type=F32)


def _inproj(x, mod_l, nw, w_in, n, w_tail, row_of_tile, tm, tn=1024):
    t = x.shape[0]
    tw = w_tail.shape[1]
    return pl.pallas_call(
        _inproj_kernel,
        grid=(t // tm, n // tn),
        in_specs=[
            pl.BlockSpec((tm, D_MODEL), lambda i, j: (i, 0)),
            pl.BlockSpec((None, N_MOD, D_MODEL), lambda i, j: (row_of_tile(i), 0, 0)),
            pl.BlockSpec((None, 1, D_MODEL), lambda i, j: (1, 0, 0)),
            pl.BlockSpec((D_MODEL, tn), lambda i, j: (0, j)),
            pl.BlockSpec((D_MODEL, tw), lambda i, j: (0, 0)),
        ],
        out_specs=[
            pl.BlockSpec((tm, tn), lambda i, j: (i, j)),
            pl.BlockSpec((tm, tw), lambda i, j: (i, 0)),
        ],
        out_shape=[jax.ShapeDtypeStruct((t, n), F32), jax.ShapeDtypeStruct((t, tw), F32)],
        scratch_shapes=[pltpu.VMEM((tm, D_MODEL), BF16)],
        compiler_params=_params(("arbitrary", "arbitrary")),
        name="inproj",
    )(x, mod_l, nw, w_in, w_tail)


def _outproj_kernel(x_ref, ap_ref, bp_ref, as_ref, bs_ref, mod_ref, wa_ref, wb_ref, o_ref, *, n_ctx_tiles):
    def emit(ma_ref, mb_ref):
        y = jnp.dot(ma_ref[...], wa_ref[...], preferred_element_type=F32)
        y = y + jnp.dot(mb_ref[...], wb_ref[...], preferred_element_type=F32)
        o_ref[...] = x_ref[...] + mod_ref[5:6, :] * y

    is_ctx = pl.program_id(0) < n_ctx_tiles
    pl.when(is_ctx)(lambda: emit(ap_ref, bp_ref))
    pl.when(jnp.logical_not(is_ctx))(lambda: emit(as_ref, bs_ref))


def _outproj(x, mix_ctx, mix_lat, mod_l, w_out, row_of_tile, tm):
    t = x.shape[0]
    half = D_MODEL // 2
    n_ctx_tiles = mix_ctx[0].shape[0] // tm
    n_lat_tiles = mix_lat[0].shape[0] // tm
    ctx_blk = lambda i: (jnp.minimum(i, n_ctx_tiles - 1), 0)
    lat_blk = lambda i: (jnp.clip(i - n_ctx_tiles, 0, n_lat_tiles - 1), 0)
    return pl.pallas_call(
        partial(_outproj_kernel, n_ctx_tiles=n_ctx_tiles),
        grid=(t // tm,),
        in_specs=[
            pl.BlockSpec((tm, D_MODEL), lambda i: (i, 0)),
            pl.BlockSpec((tm, half), ctx_blk),
            pl.BlockSpec((tm, half), ctx_blk),
            pl.BlockSpec((tm, half), lat_blk),
            pl.BlockSpec((tm, half), lat_blk),
            pl.BlockSpec((None, N_MOD, D_MODEL), lambda i: (row_of_tile(i), 0, 0)),
            pl.BlockSpec((half, D_MODEL), lambda i: (0, 0)),
            pl.BlockSpec((half, D_MODEL), lambda i: (1, 0)),
        ],
        out_specs=pl.BlockSpec((tm, D_MODEL), lambda i: (i, 0)),
        out_shape=jax.ShapeDtypeStruct((t, D_MODEL), F32),
        compiler_params=_params(("arbitrary",)),
        name="outproj",
    )(x, *mix_ctx, *mix_lat, mod_l, w_out, w_out)


def _final_norm_kernel(x_ref, w_ref, o_ref):
    x = x_ref[...]
    ms = jnp.mean(x * x, axis=-1, keepdims=True)
    o_ref[...] = x * lax.rsqrt(ms + EPS) * w_ref[...]


def _final_norm(x, w, row0, rows, tm=512):
    base = row0 // tm
    return pl.pallas_call(
        _final_norm_kernel,
        grid=(rows // tm,),
        in_specs=[pl.BlockSpec((tm, D_MODEL), lambda i: (base + i, 0)), pl.BlockSpec((1, D_MODEL), lambda i: (0, 0))],
        out_specs=pl.BlockSpec((tm, D_MODEL), lambda i: (i, 0)),
        out_shape=jax.ShapeDtypeStruct((rows, D_MODEL), F32),
        compiler_params=_params(("arbitrary",)),
        name="final_norm",
    )(x, w.reshape(1, D_MODEL))


def _finalize_heads(acc_scr, gate_ref, mix_ref, t):
    tile = min(t, 256)

    def body(i, carry):
        r = _rows(i, tile)
        o = acc_scr[r, :]
        d = o - jnp.mean(o, axis=-1, keepdims=True)
        var = jnp.mean(d * d, axis=-1, keepdims=True)
        mix_ref[r, :] = (d * lax.rsqrt(var + EPS) * _silu(gate_ref[r, :])).astype(BF16)
        return carry

    lax.fori_loop(0, t // tile, body, 0)


def _dwconv_to(dst_ref, src_ref, pad_scr, w_ref, b_ref, t, post):
    c = src_ref.shape[1]
    pad_scr[pl.ds(0, SUBLANE), :] = jnp.zeros((SUBLANE, c), F32)
    pad_scr[pl.ds(t + SUBLANE, SUBLANE), :] = jnp.zeros((SUBLANE, c), F32)
    pad_scr[pl.ds(SUBLANE, t), :] = src_ref[...]
    tile = min(t, 256)
    for r0 in range(0, t, tile):
        y = b_ref[...]
        for k in range(CONV_W):
            y = y + w_ref[k:k + 1, :] * pad_scr[pl.ds(r0 + SUBLANE - CONV_W // 2 + k, tile), :]
        dst_ref[pl.ds(r0, tile), :] = post(y).astype(dst_ref.dtype)


def _tri(reverse):
    row = lax.broadcasted_iota(jnp.int32, (CHUNK, CHUNK), 0)
    col = lax.broadcasted_iota(jnp.int32, (CHUNK, CHUNK), 1)
    return jnp.where((col >= row) if reverse else (col <= row), 1.0, 0.0).astype(F32)


def _scan_block(a, b, rowid, reverse):
    for s in (1, 2, 4):
        if reverse:
            valid = rowid < SUBLANE - s
            shift = SUBLANE - s
        else:
            valid = rowid >= s
            shift = s
        a_sh = jnp.where(valid, pltpu.roll(a, shift, 0), 1.0)
        b_sh = jnp.where(valid, pltpu.roll(b, shift, 0), 0.0)
        b = b + a * b_sh
        a = a * a_sh
    return a, b


def _lru_kernel(*refs, t, cb, has_init):
    it = iter(refs)
    x_ref, y_ref, cw_ref, cbias_ref, wa_ref, ba_ref, wi_ref, bi_ref, lam_ref = (next(it) for _ in range(9))
    h0_ref = next(it) if has_init else None
    out_ref, sout_ref, pad_scr, xc_scr, a_scr, b_scr = (next(it) for _ in range(6))
    nb = cb // BS_A

    _dwconv_to(xc_scr, x_ref, pad_scr, cw_ref, cbias_ref, t, lambda v: v)

    lam = lam_ref[...]
    sp = jnp.maximum(-lam, 0.0) + jnp.log1p(jnp.exp(-jnp.abs(lam)))
    tile = min(t, 256)

    def gates(i, carry):
        r = _rows(i, tile)
        xc = xc_scr[r, :]
        xcb = xc.astype(BF16)
        for d in range(2):
            ra = jnp.concatenate([jnp.dot(xcb[:, n * BS_A:(n + 1) * BS_A], wa_ref[d, n], preferred_element_type=F32)
                                  for n in range(nb)], axis=1) + ba_ref[d:d + 1, :]
            ia = jnp.concatenate([jnp.dot(xcb[:, n * BS_A:(n + 1) * BS_A], wi_ref[d, n], preferred_element_type=F32)
                                  for n in range(nb)], axis=1) + bi_ref[d:d + 1, :]
            log_a = -LRU_C * jax.nn.sigmoid(ra) * sp[d:d + 1, :]
            th = jnp.tanh(log_a)
            mult = jnp.sqrt(-2.0 * th / (1.0 - th))
            a_scr[d, r, :] = jnp.exp(log_a)
            b_scr[d, r, :] = mult * jax.nn.sigmoid(ia) * xc
        return carry

    lax.fori_loop(0, t // tile, gates, 0)

    rowid = lax.broadcasted_iota(jnp.int32, (SUBLANE, cb), 0)
    nblk = t // SUBLANE

    def scan(i, carry):
        hf, hb = carry
        rf = _rows(i, SUBLANE)
        rb = _rows(nblk - 1 - i, SUBLANE)
        af, bf = _scan_block(a_scr[0, rf, :], b_scr[0, rf, :], rowid, False)
        ab, bb = _scan_block(a_scr[1, rb, :], b_scr[1, rb, :], rowid, True)
        hf_blk = bf + af * hf
        hb_blk = bb + ab * hb
        a_scr[0, rf, :] = hf_blk
        a_scr[1, rb, :] = hb_blk
        return hf_blk[SUBLANE - 1:SUBLANE, :], hb_blk[0:1, :]

    if has_init:
        init = (h0_ref[0:1, :], h0_ref[1:2, :])
    else:
        init = (jnp.zeros((1, cb), F32), jnp.zeros((1, cb), F32))
    hf, hb = lax.fori_loop(0, nblk, scan, init, unroll=2)
    sout_ref[0:1, :] = hf
    sout_ref[1:2, :] = hb

    def fin(i, carry):
        r = _rows(i, tile)
        out_ref[r, :] = ((a_scr[0, r, :] + a_scr[1, r, :]) * jax.nn.gelu(y_ref[r, :])).astype(BF16)
        return carry

    lax.fori_loop(0, t // tile, fin, 0)


def _lru(z, base, nseq, t, h0, conv_w, conv_b, w_a, b_a, w_i, b_i, lam, cb=256):
    nj = D_A // cb
    nb = cb // BS_A
    has_init = h0 is not None
    in_specs = [
        pl.BlockSpec((t, cb), lambda s, j: (base + s, j)),
        pl.BlockSpec((t, cb), lambda s, j: (base + s, nj + j)),
        pl.BlockSpec((CONV_W, cb), lambda s, j: (0, j)),
        pl.BlockSpec((1, cb), lambda s, j: (0, j)),
        pl.BlockSpec((2, nb, BS_A, BS_A), lambda s, j: (0, j, 0, 0)),
        pl.BlockSpec((2, cb), lambda s, j: (0, j)),
        pl.BlockSpec((2, nb, BS_A, BS_A), lambda s, j: (0, j, 0, 0)),
        pl.BlockSpec((2, cb), lambda s, j: (0, j)),
        pl.BlockSpec((2, cb), lambda s, j: (0, j)),
    ]
    args = [z, z, conv_w, conv_b.reshape(1, D_A), w_a.astype(BF16), b_a, w_i.astype(BF16), b_i, lam]
    if has_init:
        in_specs.append(pl.BlockSpec((None, 2, cb), lambda s, j: (s, 0, j)))
        args.append(h0)
    return pl.pallas_call(
        partial(_lru_kernel, t=t, cb=cb, has_init=has_init),
        grid=(nseq, nj),
        in_specs=in_specs,
        out_specs=[
            pl.BlockSpec((t, cb), lambda s, j: (s, j)),
            pl.BlockSpec((None, 2, cb), lambda s, j: (s, 0, j)),
        ],
        out_shape=[jax.ShapeDtypeStruct((nseq * t, D_A), BF16), jax.ShapeDtypeStruct((nseq, 2, D_A), F32)],
        scratch_shapes=[
            pltpu.VMEM((t + 2 * SUBLANE, cb), F32),
            pltpu.VMEM((t, cb), F32),
            pltpu.VMEM((2, t, cb), F32),
            pltpu.VMEM((2, t, cb), F32),
        ],
        compiler_params=_params(("arbitrary", "arbitrary")),
        name="rglru",
    )(*args)


def _mlstm_kernel(*refs, t, has_init):
    it = iter(refs)
    q_ref, k_ref, v_ref, og_ref, zt_ref, bt_ref, cwq_ref, cbq_ref, cwk_ref, cbk_ref = (next(it) for _ in range(10))
    if has_init:
        c0_ref, n0_ref, m0_ref = next(it), next(it), next(it)
    mix_ref, cout_ref, nout_ref, mout_ref = (next(it) for _ in range(4))
    pad_scr, q_scr, k_scr, gate_scr, acc_scr, c_scr, n_scr = (next(it) for _ in range(7))
    nchunk = t // CHUNK

    _dwconv_to(q_scr, q_ref, pad_scr, cwq_ref, cbq_ref, t, lambda v: _silu(v) * (DK_B ** -0.5))
    _dwconv_to(k_scr, k_ref, pad_scr, cwk_ref, cbk_ref, t, _silu)

    tile = min(t, 256)
    lane_t = lax.broadcasted_iota(jnp.int32, (tile, LANE), 1)

    def gates(i, carry):
        r = _rows(i, tile)
        x = zt_ref[r, :] + bt_ref[...]
        logsig = jnp.minimum(x, 0.0) - jnp.log1p(jnp.exp(-jnp.abs(x)))
        gate_scr[r, :] = jnp.where(lane_t < 2, x, logsig)
        acc_scr[r, :] = jnp.zeros((tile, DV_B), F32)
        return carry

    lax.fori_loop(0, t // tile, gates, 0)

    if has_init:
        c_scr[...] = c0_ref[...]
        n_scr[...] = n0_ref[...]
        m_init = (m0_ref[0, :, 0:1], m0_ref[1, :, 0:1])
    else:
        c_scr[...] = jnp.zeros_like(c_scr)
        n_scr[...] = jnp.zeros_like(n_scr)
        m_init = (jnp.zeros((1, 1), F32), jnp.zeros((1, 1), F32))

    row = lax.broadcasted_iota(jnp.int32, (CHUNK, CHUNK), 0)
    col = lax.broadcasted_iota(jnp.int32, (CHUNK, CHUNK), 1)
    lane_c = lax.broadcasted_iota(jnp.int32, (CHUNK, LANE), 1)

    def chunk(cidx, d, m):
        reverse = d == 1
        li, lf = d, 2 + d
        last = 0 if reverse else CHUNK - 1
        r = _rows(cidx, CHUNK)
        gt = gate_scr[r, :]
        fsum = jnp.dot(_tri(reverse), gt, precision=HIGHEST, preferred_element_type=F32)
        y = jnp.where(lane_c == lf, fsum, gt)
        yt = y.T
        f_col, i_col = y[:, lf:lf + 1], y[:, li:li + 1]
        f_row, i_row = yt[lf:lf + 1, :], yt[li:li + 1, :]
        b = f_col + m
        mask = (col >= row) if reverse else (col <= row)
        dlog = jnp.where(mask, f_col - f_row + i_row, NEG_INF)
        m_t = jnp.maximum(b, jnp.max(dlog, axis=-1, keepdims=True))
        w = jnp.exp(dlog - m_t)
        inter = jnp.exp(b - m_t)
        q = q_scr[r, :]
        k32 = k_scr[r, :]
        v = v_ref[r, :].astype(BF16)
        s = lax.dot_general(q, k32.astype(BF16), NT, preferred_element_type=F32) * w
        c_old = c_scr[d]
        n_old = n_scr[d]
        num = jnp.dot(s.astype(BF16), v, preferred_element_type=F32)
        num = num + inter * jnp.dot(q, c_old.astype(BF16), preferred_element_type=F32)
        den = jnp.sum(s, axis=-1, keepdims=True) + inter * jnp.sum(q.astype(F32) * n_old, axis=-1, keepdims=True)
        acc_scr[r, :] += num / jnp.maximum(jnp.abs(den), jnp.exp(-m_t))
        m_new = m_t[last:last + 1, :]
        f_last = f_col[last:last + 1, :]
        kw = jnp.exp(f_last - f_col + i_col - m_new)
        decay = jnp.exp(f_last + m - m_new)
        kk = k32 * kw
        c_scr[d] = decay * c_old + lax.dot_general(kk.astype(BF16), v, TN, preferred_element_type=F32)
        n_scr[d] = decay * n_old + jnp.sum(kk, axis=0, keepdims=True)
        return m_new

    def body(c, carry):
        mf, mb = carry
        return chunk(c, 0, mf), chunk(nchunk - 1 - c, 1, mb)

    mf, mb = lax.fori_loop(0, nchunk, body, m_init, unroll=CHUNK_UNROLL)

    _finalize_heads(acc_scr, og_ref, mix_ref, t)
    cout_ref[...] = c_scr[...]
    nout_ref[...] = n_scr[...]
    mout_ref[0] = jnp.broadcast_to(mf, (1, LANE))
    mout_ref[1] = jnp.broadcast_to(mb, (1, LANE))


def _mlstm(z, zt, bias_t, base, nseq, t, init, conv_w, conv_b):
    has_init = init is not None
    qoff, koff, voff, goff = (2 * D_A // DK_B, 2 * D_A // DK_B + H_B, 2 * D_A // DK_B + 2 * H_B, 2 * D_A // DK_B + 3 * H_B)
    cb2 = conv_b.reshape(1, 2 * H_B * DK_B)
    in_specs = [
        pl.BlockSpec((t, DK_B), lambda s, h: (base + s, qoff + h)),
        pl.BlockSpec((t, DK_B), lambda s, h: (base + s, koff + h)),
        pl.BlockSpec((t, DV_B), lambda s, h: (base + s, voff + h)),
        pl.BlockSpec((t, DV_B), lambda s, h: (base + s, goff + h)),
        pl.BlockSpec((t, LANE), lambda s, h: (base + s, h)),
        pl.BlockSpec((1, LANE), lambda s, h: (0, h)),
        pl.BlockSpec((CONV_W, DK_B), lambda s, h: (0, h)),
        pl.BlockSpec((1, DK_B), lambda s, h: (0, h)),
        pl.BlockSpec((CONV_W, DK_B), lambda s, h: (0, H_B + h)),
        pl.BlockSpec((1, DK_B), lambda s, h: (0, H_B + h)),
    ]
    args = [z, z, z, z, zt, bias_t, conv_w, cb2, conv_w, cb2]
    if has_init:
        c0, n0, m0 = init
        in_specs += [
            pl.BlockSpec((None, 2, None, DK_B, DV_B), lambda s, h: (s, 0, h, 0, 0)),
            pl.BlockSpec((None, 2, None, 1, DK_B), lambda s, h: (s, 0, h, 0, 0)),
            pl.BlockSpec((None, 2, None, 1, LANE), lambda s, h: (s, 0, h, 0, 0)),
        ]
        args += [c0, n0.reshape(nseq, 2, H_B, 1, DK_B),
                 jnp.broadcast_to(m0[..., None, None], (nseq, 2, H_B, 1, LANE))]
    return pl.pallas_call(
        partial(_mlstm_kernel, t=t, has_init=has_init),
        grid=(nseq, H_B),
        in_specs=in_specs,
        out_specs=[
            pl.BlockSpec((t, DV_B), lambda s, h: (s, h)),
            pl.BlockSpec((None, 2, None, DK_B, DV_B), lambda s, h: (s, 0, h, 0, 0)),
            pl.BlockSpec((None, 2, None, 1, DK_B), lambda s, h: (s, 0, h, 0, 0)),
            pl.BlockSpec((None, 2, None, 1, LANE), lambda s, h: (s, 0, h, 0, 0)),
        ],
        out_shape=[
            jax.ShapeDtypeStruct((nseq * t, H_B * DV_B), BF16),
            jax.ShapeDtypeStruct((nseq, 2, H_B, DK_B, DV_B), F32),
            jax.ShapeDtypeStruct((nseq, 2, H_B, 1, DK_B), F32),
            jax.ShapeDtypeStruct((nseq, 2, H_B, 1, LANE), F32),
        ],
        scratch_shapes=[
            pltpu.VMEM((t + 2 * SUBLANE, DK_B), F32),
            pltpu.VMEM((t, DK_B), BF16),
            pltpu.VMEM((t, DK_B), F32),
            pltpu.VMEM((t, LANE), F32),
            pltpu.VMEM((t, DV_B), F32),
            pltpu.VMEM((2, DK_B, DV_B), F32),
            pltpu.VMEM((2, 1, DK_B), F32),
        ],
        compiler_params=_params(("arbitrary", "arbitrary")),
        name="mlstm",
    )(*args)


def _ret_kernel(*refs, t, use_rope, has_init):
    it = iter(refs)
    lg_ref, q_ref, k_ref, v_ref, g_ref = (next(it) for _ in range(5))
    if use_rope:
        cos_ref, sin_ref = next(it), next(it)
    s0_ref = next(it) if has_init else None
    mix_ref, sout_ref, acc_scr, st_scr = (next(it) for _ in range(4))
    nchunk = t // CHUNK
    h = pl.program_id(1)

    row = lax.broadcasted_iota(jnp.int32, (CHUNK, CHUNK), 0)
    col = lax.broadcasted_iota(jnp.int32, (CHUNK, CHUNK), 1)
    dist = (row - col).astype(F32)
    pos = lax.broadcasted_iota(jnp.int32, (CHUNK, DK_C), 0).astype(F32)
    consts = []
    for d in range(2):
        lg = lg_ref[d, h]
        if d == 0:
            dm = jnp.where(dist >= 0, jnp.exp(jnp.maximum(dist, 0.0) * lg), 0.0)
            qin = jnp.exp((pos + 1.0) * lg)
            kout = jnp.exp((CHUNK - 1.0 - pos) * lg)
        else:
            dm = jnp.where(dist <= 0, jnp.exp(jnp.maximum(-dist, 0.0) * lg), 0.0)
            qin = jnp.exp((CHUNK - pos) * lg)
            kout = jnp.exp(pos * lg)
        gch = jnp.exp(jnp.full((1, DV_C), float(CHUNK), F32) * lg)
        consts.append((dm, qin, kout, gch))

    acc_scr[...] = jnp.zeros_like(acc_scr)
    if has_init:
        st_scr[...] = s0_ref[...]
    else:
        st_scr[...] = jnp.zeros_like(st_scr)

    def chunk(cidx, d):
        dm, qin, kout, gch = consts[d]
        r = _rows(cidx, CHUNK)
        q = q_ref[r, :]
        k = k_ref[r, :] * (DK_C ** -0.5)
        if use_rope:
            cs, sn = cos_ref[r, :], sin_ref[r, :]
            q = q * cs + pltpu.roll(q, DK_C // 2, 1) * sn
            k = k * cs + pltpu.roll(k, DK_C // 2, 1) * sn
        v = v_ref[r, :].astype(BF16)
        st = st_scr[d]
        s = lax.dot_general(q.astype(BF16), k.astype(BF16), NT, preferred_element_type=F32) * dm
        o = jnp.dot(s.astype(BF16), v, preferred_element_type=F32)
        o = o + jnp.dot((q * qin).astype(BF16), st.astype(BF16), preferred_element_type=F32)
        st_scr[d] = gch * st + lax.dot_general((k * kout).astype(BF16), v, TN, preferred_element_type=F32)
        acc_scr[r, :] += o

    def body(c, carry):
        chunk(c, 0)
        chunk(nchunk - 1 - c, 1)
        return carry

    lax.fori_loop(0, nchunk, body, 0, unroll=CHUNK_UNROLL)
    _finalize_heads(acc_scr, g_ref, mix_ref, t)
    sout_ref[...] = st_scr[...]


def _retention(z, base, nseq, t, s0, lg, rope):
    has_init = s0 is not None
    use_rope = rope is not None
    nq = DK_C // LANE
    qoff, koff = 0, H_C * nq
    voff = 2 * H_C * DK_C // DV_C
    goff = voff + H_C
    in_specs = [
        pl.BlockSpec(memory_space=pltpu.SMEM),
        pl.BlockSpec((t, DK_C), lambda s, h: (base + s, qoff + h)),
        pl.BlockSpec((t, DK_C), lambda s, h: (base + s, koff + h)),
        pl.BlockSpec((t, DV_C), lambda s, h: (base + s, voff + h)),
        pl.BlockSpec((t, DV_C), lambda s, h: (base + s, goff + h)),
    ]
    args = [lg, z, z, z, z]
    if use_rope:
        in_specs += [pl.BlockSpec((t, DK_C), lambda s, h: (0, 0)), pl.BlockSpec((t, DK_C), lambda s, h: (0, 0))]
        args += list(rope)
    if has_init:
        in_specs.append(pl.BlockSpec((None, 2, None, DK_C, DV_C), lambda s, h: (s, 0, h, 0, 0)))
        args.append(s0)
    return pl.pallas_call(
        partial(_ret_kernel, t=t, use_rope=use_rope, has_init=has_init),
        grid=(nseq, H_C),
        in_specs=in_specs,
        out_specs=[
            pl.BlockSpec((t, DV_C), lambda s, h: (s, h)),
            pl.BlockSpec((None, 2, None, DK_C, DV_C), lambda s, h: (s, 0, h, 0, 0)),
        ],
        out_shape=[
            jax.ShapeDtypeStruct((nseq * t, H_C * DV_C), BF16),
            jax.ShapeDtypeStruct((nseq, 2, H_C, DK_C, DV_C), F32),
        ],
        scratch_shapes=[pltpu.VMEM((t, DV_C), F32), pltpu.VMEM((2, DK_C, DV_C), F32)],
        compiler_params=_params(("arbitrary", "arbitrary")),
        name="retention",
    )(*args)


def _gla_intra(q, k, g, ones, reverse):
    nsub = CHUNK // SUB
    lane = lax.broadcasted_iota(jnp.int32, (SUB, CHUNK), 1)
    rowi = lax.broadcasted_iota(jnp.int32, (SUB, CHUNK), 0)
    out = []
    for jb in range(nsub):
        lo, hi = jb * SUB, (jb + 1) * SUB
        qj, gj = q[lo:hi], g[lo:hi]
        if reverse and jb < nsub - 1:
            gref = g[hi:hi + 1]
            kt = (k[hi:] * jnp.exp(gref - g[hi:])).astype(BF16)
            kt = jnp.concatenate([jnp.zeros((hi, DK_D), BF16), kt], axis=0)
        elif not reverse and jb > 0:
            gref = g[lo - 1:lo]
            kt = (k[:lo] * jnp.exp(gref - g[:lo])).astype(BF16)
            kt = jnp.concatenate([kt, jnp.zeros((CHUNK - lo, DK_D), BF16)], axis=0)
        else:
            kt = None
        if kt is None:
            off = jnp.zeros((SUB, CHUNK), F32)
        else:
            qt = (qj * jnp.exp(gj - gref)).astype(BF16)
            off = lax.dot_general(qt, kt, NT, preferred_element_type=F32)
        prods = [qj * jnp.exp(jnp.minimum(gj - g[i:i + 1], 0.0)) * k[i:i + 1] for i in range(lo, hi)]
        sums = jnp.dot(jnp.concatenate(prods, axis=0).astype(BF16), ones, preferred_element_type=F32)
        diag = jnp.zeros((SUB, CHUNK), F32)
        for ii in range(SUB):
            diag = jnp.where(lane == lo + ii, sums[ii * SUB:(ii + 1) * SUB, :CHUNK], diag)
        mask = (rowi + lo <= lane) if reverse else (rowi + lo >= lane)
        out.append(off + jnp.where(mask, diag, 0.0))
    return jnp.concatenate(out, axis=0)


def _gla_kernel(*refs, t, has_init):
    it = iter(refs)
    q_ref, k_ref, v_ref, r_ref, a_ref, wup_ref, gb_ref = (next(it) for _ in range(7))
    s0_ref = next(it) if has_init else None
    mix_ref, sout_ref, acc_scr, st_scr, g_scr = (next(it) for _ in range(5))
    nchunk = t // CHUNK
    tile = min(t, 256)

    def gates(i, carry):
        r = _rows(i, tile)
        a = a_ref[r, :].astype(BF16)
        for d in range(2):
            x = jnp.dot(a, wup_ref[d], preferred_element_type=F32) + gb_ref[d]
            g_scr[d, r, :] = (jnp.minimum(x, 0.0) - jnp.log1p(jnp.exp(-jnp.abs(x)))) * (1.0 / GLA_TAU)
        acc_scr[r, :] = jnp.zeros((tile, DV_D), F32)
        return carry

    lax.fori_loop(0, t // tile, gates, 0)

    for d in range(2):
        st_scr[d] = s0_ref[d].T if has_init else jnp.zeros((DV_D, DK_D), F32)

    ones = jnp.ones((DK_D, LANE), BF16)

    def chunk(cidx, d):
        reverse = d == 1
        last = 0 if reverse else CHUNK - 1
        r = _rows(cidx, CHUNK)
        g = jnp.dot(_tri(reverse), g_scr[d, r, :], precision=HIGHEST, preferred_element_type=F32)
        g_last = g[last:last + 1, :]
        q = q_ref[r, :] * (DK_D ** -0.5)
        k = k_ref[r, :]
        v = v_ref[r, :].astype(BF16)
        st = st_scr[d]
        s = _gla_intra(q, k, g, ones, reverse)
        o = jnp.dot(s.astype(BF16), v, preferred_element_type=F32)
        o = o + lax.dot_general((q * jnp.exp(g)).astype(BF16), st.astype(BF16), NT, preferred_element_type=F32)
        kd = (k * jnp.exp(g_last - g)).astype(BF16)
        st_scr[d] = jnp.exp(g_last) * st + lax.dot_general(v, kd, TN, preferred_element_type=F32)
        acc_scr[r, :] += o

    def body(c, carry):
        chunk(c, 0)
        chunk(nchunk - 1 - c, 1)
        return carry

    lax.fori_loop(0, nchunk, body, 0)
    _finalize_heads(acc_scr, r_ref, mix_ref, t)
    for d in range(2):
        sout_ref[d] = st_scr[d].T


def _gla(z, zt, base, nseq, t, s0, w_up, g_bias):
    has_init = s0 is not None
    nq = DK_D // LANE
    cbase = (2 * H_C * DK_C + 2 * H_C * DV_C)
    qoff = cbase // DK_D
    koff = qoff + H_D * nq
    voff = (cbase + 2 * H_D * DK_D) // DV_D
    roff = voff + H_D
    in_specs = [
        pl.BlockSpec((t, DK_D), lambda s, h: (base + s, qoff + h)),
        pl.BlockSpec((t, DK_D), lambda s, h: (base + s, koff + h)),
        pl.BlockSpec((t, DV_D), lambda s, h: (base + s, voff + h)),
        pl.BlockSpec((t, DV_D), lambda s, h: (base + s, roff + h)),
        pl.BlockSpec((t, LANE), lambda s, h: (base + s, 0)),
        pl.BlockSpec((2, None, LANE, DK_D), lambda s, h: (0, h, 0, 0)),
        pl.BlockSpec((2, None, 1, DK_D), lambda s, h: (0, h, 0, 0)),
    ]
    args = [z, z, z, z, zt, w_up, g_bias]
    if has_init:
        in_specs.append(pl.BlockSpec((None, 2, None, DK_D, DV_D), lambda s, h: (s, 0, h, 0, 0)))
        args.append(s0)
    return pl.pallas_call(
        partial(_gla_kernel, t=t, has_init=has_init),
        grid=(nseq, H_D),
        in_specs=in_specs,
        out_specs=[
            pl.BlockSpec((t, DV_D), lambda s, h: (s, h)),
            pl.BlockSpec((None, 2, None, DK_D, DV_D), lambda s, h: (s, 0, h, 0, 0)),
        ],
        out_shape=[
            jax.ShapeDtypeStruct((nseq * t, H_D * DV_D), BF16),
            jax.ShapeDtypeStruct((nseq, 2, H_D, DK_D, DV_D), F32),
        ],
        scratch_shapes=[
            pltpu.VMEM((t, DV_D), F32),
            pltpu.VMEM((2, DV_D, DK_D), F32),
            pltpu.VMEM((2, t, DK_D), F32),
        ],
        compiler_params=_params(("arbitrary", "arbitrary")),
        name="gla",
    )(*args)


def _rope_tables(t):
    pos = jnp.arange(t)
    row = (pos // GRID_W).astype(F32)
    col = (pos % GRID_W).astype(F32)
    nf = DK_C // 4
    freqs = ROPE_BASE ** (-jnp.arange(nf, dtype=F32) / nf)
    ang = jnp.concatenate([row[:, None] * freqs, col[:, None] * freqs], axis=-1)
    cos, sin = jnp.cos(ang), jnp.sin(ang)
    return jnp.concatenate([cos, cos], axis=-1), jnp.concatenate([-sin, sin], axis=-1)


def _ab_tail(w_in, i_bias, f_bias):
    n_main = w_in.shape[1] - 4 * H_B
    idx = jnp.array([[h, H_B + h, 2 * H_B + h, 3 * H_B + h] for h in range(H_B)])
    cols = w_in[:, n_main:][:, idx]
    w_tail = jnp.pad(cols, ((0, 0), (0, 0), (0, LANE - 4))).reshape(D_MODEL, H_B * LANE)
    bias = jnp.concatenate([i_bias.reshape(-1), f_bias.reshape(-1)])[idx]
    bias_t = jnp.pad(bias, ((0, 0), (0, LANE - 4))).reshape(1, H_B * LANE)
    return n_main, w_tail, bias_t


def _cd_tail(w_in, w_up, g_bias):
    n_main = w_in.shape[1] - 2 * GLA_RANK
    w_tail = jnp.pad(w_in[:, n_main:], ((0, 0), (0, LANE - 2 * GLA_RANK)))
    up = w_up.reshape(2, GLA_RANK, H_D, DK_D).transpose(0, 2, 1, 3)
    up = jnp.stack([jnp.pad(up[d], ((0, 0), (d * GLA_RANK, LANE - (d + 1) * GLA_RANK), (0, 0))) for d in range(2)])
    return n_main, w_tail, up.astype(BF16), g_bias.reshape(2, H_D, 1, DK_D)


def kernel(x_prompt, x_sample, c, state_lru, state_mlstm_C, state_mlstm_n, state_mlstm_m, state_ret, state_gla, c_ctx, w_mod, b_mod, norm_w, ffn_w_gate, ffn_w_up, ffn_w_down, w_in_ab, w_out_ab, lru_conv_w, lru_conv_b, lru_w_a, lru_b_a, lru_w_i, lru_b_i, lru_lambda, mlstm_conv_w, mlstm_conv_b, mlstm_i_bias, mlstm_f_bias, w_in_cd, w_out_cd, ret_decay_log, gla_w_up, gla_b, final_norm_w):
    Bp, Tp, D = x_prompt.shape
    Bs, Ts, _ = x_sample.shape
    n_ctx = Bp * Tp
    tm_ffn, tf_ffn, tm_in, tm_out = 1024, 256, 1024, 512
    assert all(n_ctx % t == 0 and Ts % t == 0 for t in (tm_ffn, tm_in, tm_out)) and 1 + Bs <= MOD_ROWS and n_ctx % Ts == 0
    row_ffn, row_in, row_out = (_mod_row_map(n_ctx // t, Ts // t) for t in (tm_ffn, tm_in, tm_out))
    groups = ((0, Bp, Tp), (n_ctx // Ts, Bs, Ts))

    cond = jnp.concatenate([c_ctx[None], c, jnp.zeros((MOD_ROWS - 1 - Bs, D), F32)], axis=0)
    mod = _modulation(cond, w_mod, b_mod).reshape(DEPTH, MOD_ROWS, N_MOD, D)

    x = jnp.concatenate([x_prompt.reshape(n_ctx, D), x_sample.reshape(Bs * Ts, D)], axis=0)
    rope = _rope_tables(Ts)

    st_lru, st_C, st_n, st_m, st_ret, st_gla = [], [], [], [], [], []
    for l in range(DEPTH):
        mod_l = mod[l]
        nw = norm_w[l].reshape(3, 1, D)
        wg, wu, wd = ffn_w_gate[l].astype(BF16), ffn_w_up[l].astype(BF16), ffn_w_down[l].astype(BF16)
        x = _ffn(x, mod_l, nw, wg[0], wu[0], wd[0], 0, row_ffn, tm_ffn, tf_ffn)
        (pb, pn, pt), (sb, sn, st) = groups
        if l % 2 == 0:
            e = l // 2
            n_main, w_tail, bias_t = _ab_tail(w_in_ab[e], mlstm_i_bias[e], mlstm_f_bias[e])
            z, zt = _inproj(x, mod_l, nw, w_in_ab[e].astype(BF16), n_main, w_tail.astype(BF16), row_in, tm_in)
            lru_args = (lru_conv_w[e], lru_conv_b[e], lru_w_a[e], lru_b_a[e], lru_w_i[e], lru_b_i[e], lru_lambda[e])
            a_p, s_lru = _lru(z, pb, pn, pt, None, *lru_args)
            a_s, _ = _lru(z, sb, sn, st, state_lru[:, e], *lru_args)
            b_p, s_C, s_n, s_m = _mlstm(z, zt, bias_t, pb, pn, pt, None, mlstm_conv_w[e], mlstm_conv_b[e])
            b_s, _, _, _ = _mlstm(z, zt, bias_t, sb, sn, st,
                                  (state_mlstm_C[:, e], state_mlstm_n[:, e], state_mlstm_m[:, e]),
                                  mlstm_conv_w[e], mlstm_conv_b[e])
            st_lru.append(s_lru)
            st_C.append(s_C)
            st_n.append(s_n.reshape(Bp, 2, H_B, DK_B))
            st_m.append(s_m[:, :, :, 0, 0])
            mix_ctx, mix_lat = (a_p, b_p), (a_s, b_s)
            w_out = w_out_ab[e].astype(BF16)
        else:
            o = l // 2
            n_main, w_tail, w_up, g_bias = _cd_tail(w_in_cd[o], gla_w_up[o], gla_b[o])
            z, zt = _inproj(x, mod_l, nw, w_in_cd[o].astype(BF16), n_main, w_tail.astype(BF16), row_in, tm_in)
            c_p, s_ret = _retention(z, pb, pn, pt, None, ret_decay_log[o], None)
            c_s, _ = _retention(z, sb, sn, st, state_ret[:, o], ret_decay_log[o], rope)
            d_p, s_gla = _gla(z, zt, pb, pn, pt, None, w_up, g_bias)
            d_s, _ = _gla(z, zt, sb, sn, st, state_gla[:, o], w_up, g_bias)
            st_ret.append(s_ret)
            st_gla.append(s_gla)
            mix_ctx, mix_lat = (c_p, d_p), (c_s, d_s)
            w_out = w_out_cd[o].astype(BF16)
        x = _outproj(x, mix_ctx, mix_lat, mod_l, w_out, row_out, tm_out)
        x = _ffn(x, mod_l, nw, wg[1], wu[1], wd[1], 2, row_ffn, tm_ffn, tf_ffn)

    y_prompt = _final_norm(x, final_norm_w, 0, n_ctx).reshape(Bp, Tp, D)
    y_sample = _final_norm(x, final_norm_w, n_ctx, Bs * Ts).reshape(Bs, Ts, D)
    return (y_prompt, y_sample,
            jnp.stack(st_lru, axis=1), jnp.stack(st_C, axis=1), jnp.stack(st_n, axis=1), jnp.stack(st_m, axis=1),
            jnp.stack(st_ret, axis=1), jnp.stack(st_gla, axis=1))
```

```python
from functools import partial

import jax
import jax.numpy as jnp
from jax import lax
from jax.experimental import pallas as pl
from jax.experimental.pallas import tpu as pltpu

F32 = jnp.float32
BF16 = jnp.bfloat16

D_MODEL = 2048
DEPTH = 4
N_MOD = 9
D_FF = 5632
EPS = 1e-6
CHUNK = 64
CONV_W = 4
D_A = D_MODEL // 2
NB_A = 8
BS_A = D_A // NB_A
LRU_C = 8.0
H_B = 4
DK_B = D_MODEL // 8
DV_B = D_MODEL // 8
H_C = 4
DK_C = D_MODEL // 16
DV_C = D_MODEL // 8
H_D = 4
DK_D = D_MODEL // 16
DV_D = D_MODEL // 8
GLA_RANK = 16
GLA_TAU = 16.0
ROPE_BASE = 10000.0
GRID_W = 64

MOD_ROWS = 8
LANE = 128
SUBLANE = 8
SUB = 16
CHUNK_UNROLL = 4
VMEM_LIMIT = 56 * 1024 * 1024
NEG_INF = float("-inf")
LOG2E = 1.4426950408889634
HIGHEST = lax.Precision.HIGHEST
NT = (((1,), (1,)), ((), ()))
TN = (((0,), (0,)), ((), ()))


def _params(sem):
    return pltpu.CompilerParams(dimension_semantics=sem, vmem_limit_bytes=VMEM_LIMIT)


def _silu(x):
    return x * jax.nn.sigmoid(x)


def _rows(i, n):
    return pl.ds(pl.multiple_of(i * n, n), n)


def _mod_kernel(c_ref, w_ref, b_ref, o_ref):
    s = _silu(c_ref[...]).astype(BF16)
    o_ref[...] = jnp.dot(s, w_ref[...].astype(BF16), preferred_element_type=F32) + b_ref[...]


def _modulation(cond, w_mod, b_mod, tn=1024):
    n = w_mod.shape[-1]
    return pl.pallas_call(
        _mod_kernel,
        grid=(DEPTH, n // tn),
        in_specs=[
            pl.BlockSpec((MOD_ROWS, D_MODEL), lambda l, j: (0, 0)),
            pl.BlockSpec((None, D_MODEL, tn), lambda l, j: (l, 0, j)),
            pl.BlockSpec((None, 1, tn), lambda l, j: (l, 0, j)),
        ],
        out_specs=pl.BlockSpec((None, MOD_ROWS, tn), lambda l, j: (l, 0, j)),
        out_shape=jax.ShapeDtypeStruct((DEPTH, MOD_ROWS, n), F32),
        compiler_params=_params(("arbitrary", "arbitrary")),
        name="modulation",
    )(cond, w_mod, b_mod.reshape(DEPTH, 1, n))


def _norm_mod(x, nw, shift, scale):
    ms = jnp.mean(x * x, axis=-1, keepdims=True)
    y = x * lax.rsqrt(ms + EPS) * nw
    return y * (1.0 + scale) + shift


def _mod_row_map(n_ctx_tiles, tiles_per_req):
    def row(i):
        return jnp.where(i < n_ctx_tiles, 0, 1 + (i - n_ctx_tiles) // tiles_per_req)
    return row


def _ffn_kernel(x_ref, mod_ref, nw_ref, wg_ref, wu_ref, wd_ref, o_ref, h_scr, *, k, nf):
    f = pl.program_id(1)

    @pl.when(f == 0)
    def _():
        h = _norm_mod(x_ref[...], nw_ref[...], mod_ref[3 * k:3 * k + 1, :], mod_ref[3 * k + 1:3 * k + 2, :])
        h_scr[...] = h.astype(BF16)
        o_ref[...] = jnp.zeros_like(o_ref)

    h = h_scr[...]
    g = jnp.dot(h, wg_ref[...], preferred_element_type=F32)
    u = jnp.dot(h, wu_ref[...], preferred_element_type=F32)
    a = (_silu(g) * u).astype(BF16)
    o_ref[...] += jnp.dot(a, wd_ref[...], preferred_element_type=F32)

    @pl.when(f == nf - 1)
    def _():
        o_ref[...] = x_ref[...] + (0.5 * mod_ref[3 * k + 2:3 * k + 3, :]) * o_ref[...]


def _ffn(x, mod, nw, wg, wu, wd, l, j, row_of_tile, tm, tf):
    t = x.shape[0]
    nf = D_FF // tf
    k = 2 * j
    return pl.pallas_call(
        partial(_ffn_kernel, k=k, nf=nf),
        grid=(t // tm, nf),
        in_specs=[
            pl.BlockSpec((tm, D_MODEL), lambda i, f: (i, 0)),
            pl.BlockSpec((None, None, N_MOD, D_MODEL), lambda i, f: (l, row_of_tile(i), 0, 0)),
            pl.BlockSpec((None, None, 1, D_MODEL), lambda i, f: (l, k, 0, 0)),
            pl.BlockSpec((None, None, D_MODEL, tf), lambda i, f: (l, j, 0, f)),
            pl.BlockSpec((None, None, D_MODEL, tf), lambda i, f: (l, j, 0, f)),
            pl.BlockSpec((None, None, tf, D_MODEL), lambda i, f: (l, j, f, 0)),
        ],
        out_specs=pl.BlockSpec((tm, D_MODEL), lambda i, f: (i, 0)),
        out_shape=jax.ShapeDtypeStruct((t, D_MODEL), F32),
        scratch_shapes=[pltpu.VMEM((tm, D_MODEL), BF16)],
        compiler_params=_params(("arbitrary", "arbitrary")),
        name="ffn",
    )(x, mod, nw, wg, wu, wd)


def _inproj_kernel(x_ref, mod_ref, nw_ref, w_ref, wt_ref, z_ref, zt_ref, h_scr):
    j = pl.program_id(1)

    @pl.when(j == 0)
    def _():
        h = _norm_mod(x_ref[...], nw_ref[...], mod_ref[3:4, :], mod_ref[4:5, :])
        h_scr[...] = h.astype(BF16)
        zt_ref[...] = jnp.dot(h_scr[...], wt_ref[...], preferred_element_type=F32)

    z_ref[...] = jnp.dot(h_scr[...], w_ref[...], preferred_element_type=F32)


def _inproj(x, mod, nw, w_in, l, e, n, w_tail, row_of_tile, tm, tn=1024):
    t = x.shape[0]
    tw = w_tail.shape[1]
    return pl.pallas_call(
        _inproj_kernel,
        grid=(t // tm, n // tn),
        in_specs=[
            pl.BlockSpec((tm, D_MODEL), lambda i, j: (i, 0)),
            pl.BlockSpec((None, None, N_MOD, D_MODEL), lambda i, j: (l, row_of_tile(i), 0, 0)),
            pl.BlockSpec((None, None, 1, D_MODEL), lambda i, j: (l, 1, 0, 0)),
            pl.BlockSpec((None, D_MODEL, tn), lambda i, j: (e, 0, j)),
            pl.BlockSpec((D_MODEL, tw), lambda i, j: (0, 0)),
        ],
        out_specs=[
            pl.BlockSpec((tm, tn), lambda i, j: (i, j)),
            pl.BlockSpec((tm, tw), lambda i, j: (i, 0)),
        ],
        out_shape=[jax.ShapeDtypeStruct((t, n), F32), jax.ShapeDtypeStruct((t, tw), F32)],
        scratch_shapes=[pltpu.VMEM((tm, D_MODEL), BF16)],
        compiler_params=_params(("arbitrary", "arbitrary")),
        name="inproj",
    )(x, mod, nw, w_in, w_tail)


def _outproj_kernel(x_ref, ap_ref, bp_ref, as_ref, bs_ref, mod_ref, wa_ref, wb_ref, o_ref, *, n_ctx_tiles):
    def emit(ma_ref, mb_ref):
        y = jnp.dot(ma_ref[...], wa_ref[...], preferred_element_type=F32)
        y = y + jnp.dot(mb_ref[...], wb_ref[...], preferred_element_type=F32)
        o_ref[...] = x_ref[...] + mod_ref[5:6, :] * y

    is_ctx = pl.program_id(0) < n_ctx_tiles
    pl.when(is_ctx)(lambda: emit(ap_ref, bp_ref))
    pl.when(jnp.logical_not(is_ctx))(lambda: emit(as_ref, bs_ref))


def _outproj(x, mix_ctx, mix_lat, mod, w_out, l, e, row_of_tile, tm):
    t = x.shape[0]
    half = D_MODEL // 2
    n_ctx_tiles = mix_ctx[0].shape[0] // tm
    n_lat_tiles = mix_lat[0].shape[0] // tm
    ctx_blk = lambda i: (jnp.minimum(i, n_ctx_tiles - 1), 0)
    lat_blk = lambda i: (jnp.clip(i - n_ctx_tiles, 0, n_lat_tiles - 1), 0)
    return pl.pallas_call(
        partial(_outproj_kernel, n_ctx_tiles=n_ctx_tiles),
        grid=(t // tm,),
        in_specs=[
            pl.BlockSpec((tm, D_MODEL), lambda i: (i, 0)),
            pl.BlockSpec((tm, half), ctx_blk),
            pl.BlockSpec((tm, half), ctx_blk),
            pl.BlockSpec((tm, half), lat_blk),
            pl.BlockSpec((tm, half), lat_blk),
            pl.BlockSpec((None, None, N_MOD, D_MODEL), lambda i: (l, row_of_tile(i), 0, 0)),
            pl.BlockSpec((None, half, D_MODEL), lambda i: (e, 0, 0)),
            pl.BlockSpec((None, half, D_MODEL), lambda i: (e, 1, 0)),
        ],
        out_specs=pl.BlockSpec((tm, D_MODEL), lambda i: (i, 0)),
        out_shape=jax.ShapeDtypeStruct((t, D_MODEL), F32),
        compiler_params=_params(("arbitrary",)),
        name="outproj",
    )(x, *mix_ctx, *mix_lat, mod, w_out, w_out)


def _final_norm_kernel(x_ref, w_ref, o_ref):
    x = x_ref[...]
    ms = jnp.mean(x * x, axis=-1, keepdims=True)
    o_ref[...] = x * lax.rsqrt(ms + EPS) * w_ref[...]


def _final_norm(x, w, row0, rows, tm=512):
    base = row0 // tm
    return pl.pallas_call(
        _final_norm_kernel,
        grid=(rows // tm,),
        in_specs=[pl.BlockSpec((tm, D_MODEL), lambda i: (base + i, 0)), pl.BlockSpec((1, D_MODEL), lambda i: (0, 0))],
        out_specs=pl.BlockSpec((tm, D_MODEL), lambda i: (i, 0)),
        out_shape=jax.ShapeDtypeStruct((rows, D_MODEL), F32),
        compiler_params=_params(("arbitrary",)),
        name="final_norm",
    )(x, w.reshape(1, D_MODEL))


def _finalize_heads(acc_scr, gate_ref, mix_ref, t, hg, dv):
    tile = min(t, 256)

    def body(i, carry):
        r = _rows(i, tile)
        for u in range(hg):
            cv = slice(u * dv, (u + 1) * dv)
            o = acc_scr[r, cv]
            d = o - jnp.mean(o, axis=-1, keepdims=True)
            var = jnp.mean(d * d, axis=-1, keepdims=True)
            mix_ref[r, cv] = (d * lax.rsqrt(var + EPS) * _silu(gate_ref[r, cv])).astype(BF16)
        return carry

    lax.fori_loop(0, t // tile, body, 0)


def _dwconv_to(dst_ref, src_ref, pad_scr, w_ref, b_ref, t, post):
    c = src_ref.shape[1]
    pad_scr[pl.ds(0, SUBLANE), :] = jnp.zeros((SUBLANE, c), F32)
    pad_scr[pl.ds(t + SUBLANE, SUBLANE), :] = jnp.zeros((SUBLANE, c), F32)
    pad_scr[pl.ds(SUBLANE, t), :] = src_ref[...]
    tile = min(t, 256)
    for r0 in range(0, t, tile):
        y = b_ref[...]
        for k in range(CONV_W):
            y = y + w_ref[k:k + 1, :] * pad_scr[pl.ds(r0 + SUBLANE - CONV_W // 2 + k, tile), :]
        dst_ref[pl.ds(r0, tile), :] = post(y).astype(dst_ref.dtype)


def _tri(reverse):
    row = lax.broadcasted_iota(jnp.int32, (CHUNK, CHUNK), 0)
    col = lax.broadcasted_iota(jnp.int32, (CHUNK, CHUNK), 1)
    return jnp.where((col >= row) if reverse else (col <= row), 1.0, 0.0).astype(F32)


def _scan_block(a, b, rowid, reverse):
    for s in (1, 2, 4):
        if reverse:
            valid = rowid < SUBLANE - s
            shift = SUBLANE - s
        else:
            valid = rowid >= s
            shift = s
        a_sh = jnp.where(valid, pltpu.roll(a, shift, 0), 1.0)
        b_sh = jnp.where(valid, pltpu.roll(b, shift, 0), 0.0)
        b = b + a * b_sh
        a = a * a_sh
    return a, b


def _lru_kernel(*refs, t, cb, has_init):
    it = iter(refs)
    x_ref, y_ref, cw_ref, cbias_ref, wa_ref, ba_ref, wi_ref, bi_ref, lam_ref = (next(it) for _ in range(9))
    h0_ref = next(it) if has_init else None
    out_ref, sout_ref, pad_scr, xc_scr, a_scr, b_scr = (next(it) for _ in range(6))
    nb = cb // BS_A

    _dwconv_to(xc_scr, x_ref, pad_scr, cw_ref, cbias_ref, t, lambda v: v)

    lam = lam_ref[...]
    sp = jnp.maximum(-lam, 0.0) + jnp.log1p(jnp.exp(-jnp.abs(lam)))
    tile = min(t, 256)

    def gates(i, carry):
        r = _rows(i, tile)
        xc = xc_scr[r, :]
        xcb = xc.astype(BF16)
        for d in range(2):
            ra = jnp.concatenate([jnp.dot(xcb[:, n * BS_A:(n + 1) * BS_A], wa_ref[d, n], preferred_element_type=F32)
                                  for n in range(nb)], axis=1) + ba_ref[d:d + 1, :]
            ia = jnp.concatenate([jnp.dot(xcb[:, n * BS_A:(n + 1) * BS_A], wi_ref[d, n], preferred_element_type=F32)
                                  for n in range(nb)], axis=1) + bi_ref[d:d + 1, :]
            log_a = -LRU_C * jax.nn.sigmoid(ra) * sp[d:d + 1, :]
            th = jnp.tanh(log_a)
            mult = jnp.sqrt(-2.0 * th / (1.0 - th))
            a_scr[d, r, :] = jnp.exp(log_a)
            b_scr[d, r, :] = mult * jax.nn.sigmoid(ia) * xc
        return carry

    lax.fori_loop(0, t // tile, gates, 0)

    rowid = lax.broadcasted_iota(jnp.int32, (SUBLANE, cb), 0)
    nblk = t // SUBLANE

    def scan(i, carry):
        hf, hb = carry
        rf = _rows(i, SUBLANE)
        rb = _rows(nblk - 1 - i, SUBLANE)
        af, bf = _scan_block(a_scr[0, rf, :], b_scr[0, rf, :], rowid, False)
        ab, bb = _scan_block(a_scr[1, rb, :], b_scr[1, rb, :], rowid, True)
        hf_blk = bf + af * hf
        hb_blk = bb + ab * hb
        a_scr[0, rf, :] = hf_blk
        a_scr[1, rb, :] = hb_blk
        return hf_blk[SUBLANE - 1:SUBLANE, :], hb_blk[0:1, :]

    if has_init:
        init = (h0_ref[0:1, :], h0_ref[1:2, :])
    else:
        init = (jnp.zeros((1, cb), F32), jnp.zeros((1, cb), F32))
    hf, hb = lax.fori_loop(0, nblk, scan, init, unroll=2)
    sout_ref[0:1, :] = hf
    sout_ref[1:2, :] = hb

    def fin(i, carry):
        r = _rows(i, tile)
        out_ref[r, :] = ((a_scr[0, r, :] + a_scr[1, r, :]) * jax.nn.gelu(y_ref[r, :])).astype(BF16)
        return carry

    lax.fori_loop(0, t // tile, fin, 0)


def _lru(z, base, nseq, t, h0, conv_w, conv_b, w_a, b_a, w_i, b_i, lam, cb=256):
    nj = D_A // cb
    nb = cb // BS_A
    has_init = h0 is not None
    in_specs = [
        pl.BlockSpec((t, cb), lambda s, j: (base + s, j)),
        pl.BlockSpec((t, cb), lambda s, j: (base + s, nj + j)),
        pl.BlockSpec((CONV_W, cb), lambda s, j: (0, j)),
        pl.BlockSpec((1, cb), lambda s, j: (0, j)),
        pl.BlockSpec((2, nb, BS_A, BS_A), lambda s, j: (0, j, 0, 0)),
        pl.BlockSpec((2, cb), lambda s, j: (0, j)),
        pl.BlockSpec((2, nb, BS_A, BS_A), lambda s, j: (0, j, 0, 0)),
        pl.BlockSpec((2, cb), lambda s, j: (0, j)),
        pl.BlockSpec((2, cb), lambda s, j: (0, j)),
    ]
    args = [z, z, conv_w, conv_b.reshape(1, D_A), w_a.astype(BF16), b_a, w_i.astype(BF16), b_i, lam]
    if has_init:
        in_specs.append(pl.BlockSpec((None, 2, cb), lambda s, j: (s, 0, j)))
        args.append(h0)
    return pl.pallas_call(
        partial(_lru_kernel, t=t, cb=cb, has_init=has_init),
        grid=(nseq, nj),
        in_specs=in_specs,
        out_specs=[
            pl.BlockSpec((t, cb), lambda s, j: (s, j)),
            pl.BlockSpec((None, 2, cb), lambda s, j: (s, 0, j)),
        ],
        out_shape=[jax.ShapeDtypeStruct((nseq * t, D_A), BF16), jax.ShapeDtypeStruct((nseq, 2, D_A), F32)],
        scratch_shapes=[
            pltpu.VMEM((t + 2 * SUBLANE, cb), F32),
            pltpu.VMEM((t, cb), F32),
            pltpu.VMEM((2, t, cb), F32),
            pltpu.VMEM((2, t, cb), F32),
        ],
        compiler_params=_params(("arbitrary", "arbitrary")),
        name="rglru",
    )(*args)


def _mlstm_kernel(*refs, t, hg, has_init):
    it = iter(refs)
    q_ref, k_ref, v_ref, og_ref, zt_ref, bt_ref, cwq_ref, cbq_ref, cwk_ref, cbk_ref = (next(it) for _ in range(10))
    if has_init:
        c0_ref, n0_ref, m0_ref = next(it), next(it), next(it)
    mix_ref, cout_ref, nout_ref, mout_ref = (next(it) for _ in range(4))
    pad_scr, q_scr, k_scr, gate_scr, acc_scr, c_scr, n_scr = (next(it) for _ in range(7))
    nchunk = t // CHUNK

    _dwconv_to(q_scr, q_ref, pad_scr, cwq_ref, cbq_ref, t, lambda v: _silu(v) * (DK_B ** -0.5))
    _dwconv_to(k_scr, k_ref, pad_scr, cwk_ref, cbk_ref, t, _silu)

    tile = min(t, 256)
    lane_t = lax.broadcasted_iota(jnp.int32, (tile, hg * LANE), 1) % LANE

    def gates(i, carry):
        r = _rows(i, tile)
        x = zt_ref[r, :] + bt_ref[...]
        logsig = jnp.minimum(x, 0.0) - jnp.log1p(jnp.exp(-jnp.abs(x)))
        gate_scr[r, :] = jnp.where(lane_t < 2, x, logsig)
        acc_scr[r, :] = jnp.zeros((tile, hg * DV_B), F32)
        return carry

    lax.fori_loop(0, t // tile, gates, 0)

    if has_init:
        c_scr[...] = c0_ref[...]
        n_scr[...] = n0_ref[...]
        m_init = tuple(m0_ref[d, u, :, 0:1] for u in range(hg) for d in range(2))
    else:
        c_scr[...] = jnp.zeros_like(c_scr)
        n_scr[...] = jnp.zeros_like(n_scr)
        m_init = tuple(jnp.zeros((1, 1), F32) for _ in range(2 * hg))

    row = lax.broadcasted_iota(jnp.int32, (CHUNK, CHUNK), 0)
    col = lax.broadcasted_iota(jnp.int32, (CHUNK, CHUNK), 1)
    lane_c = lax.broadcasted_iota(jnp.int32, (CHUNK, LANE), 1)

    def chunk(cidx, u, d, m):
        reverse = d == 1
        li, lf = d, 2 + d
        last = 0 if reverse else CHUNK - 1
        r = _rows(cidx, CHUNK)
        ck = slice(u * DK_B, (u + 1) * DK_B)
        cv = slice(u * DV_B, (u + 1) * DV_B)
        gt = gate_scr[r, u * LANE:(u + 1) * LANE]
        fsum = jnp.dot(_tri(reverse), gt, precision=HIGHEST, preferred_element_type=F32)
        y = jnp.where(lane_c == lf, fsum, gt)
        yt = y.T
        f_col, i_col = y[:, lf:lf + 1], y[:, li:li + 1]
        f_row, i_row = yt[lf:lf + 1, :], yt[li:li + 1, :]
        b = f_col + m
        mask = (col >= row) if reverse else (col <= row)
        dlog = jnp.where(mask, f_col - f_row + i_row, NEG_INF)
        m_t = jnp.maximum(b, jnp.max(dlog, axis=-1, keepdims=True))
        w = jnp.exp(dlog - m_t)
        inter = jnp.exp(b - m_t)
        q = q_scr[r, ck]
        k32 = k_scr[r, ck]
        v = v_ref[r, cv].astype(BF16)
        s = lax.dot_general(q, k32.astype(BF16), NT, preferred_element_type=F32) * w
        c_old = c_scr[d, u]
        n_old = n_scr[d, u]
        num = jnp.dot(s.astype(BF16), v, preferred_element_type=F32)
        num = num + inter * jnp.dot(q, c_old.astype(BF16), preferred_element_type=F32)
        den = jnp.sum(s, axis=-1, keepdims=True) + inter * jnp.sum(q.astype(F32) * n_old, axis=-1, keepdims=True)
        acc_scr[r, cv] += num / jnp.maximum(jnp.abs(den), jnp.exp(-m_t))
        m_new = m_t[last:last + 1, :]
        f_last = f_col[last:last + 1, :]
        kw = jnp.exp(f_last - f_col + i_col - m_new)
        decay = jnp.exp(f_last + m - m_new)
        kk = k32 * kw
        c_scr[d, u] = decay * c_old + lax.dot_general(kk.astype(BF16), v, TN, preferred_element_type=F32)
        n_scr[d, u] = decay * n_old + jnp.sum(kk, axis=0, keepdims=True)
        return m_new

    def body(c, carry):
        out = []
        for u in range(hg):
            out.append(chunk(c, u, 0, carry[2 * u]))
            out.append(chunk(nchunk - 1 - c, u, 1, carry[2 * u + 1]))
        return tuple(out)

    m_fin = lax.fori_loop(0, nchunk, body, m_init, unroll=CHUNK_UNROLL)

    _finalize_heads(acc_scr, og_ref, mix_ref, t, hg, DV_B)
    cout_ref[...] = c_scr[...]
    nout_ref[...] = n_scr[...]
    for u in range(hg):
        for d in range(2):
            mout_ref[d, u] = jnp.broadcast_to(m_fin[2 * u + d], (1, LANE))


def _mlstm(z, zt, bias_t, base, nseq, t, init, conv_w, conv_b, hg):
    has_init = init is not None
    kw, vw = hg * DK_B, hg * DV_B
    qoff = 2 * D_A // kw
    koff = qoff + H_B // hg
    voff = koff + H_B // hg
    goff = voff + H_B // hg
    cb2 = conv_b.reshape(1, 2 * H_B * DK_B)
    in_specs = [
        pl.BlockSpec((t, kw), lambda s, h: (base + s, qoff + h)),
        pl.BlockSpec((t, kw), lambda s, h: (base + s, koff + h)),
        pl.BlockSpec((t, vw), lambda s, h: (base + s, voff + h)),
        pl.BlockSpec((t, vw), lambda s, h: (base + s, goff + h)),
        pl.BlockSpec((t, hg * LANE), lambda s, h: (base + s, h)),
        pl.BlockSpec((1, hg * LANE), lambda s, h: (0, h)),
        pl.BlockSpec((CONV_W, kw), lambda s, h: (0, h)),
        pl.BlockSpec((1, kw), lambda s, h: (0, h)),
        pl.BlockSpec((CONV_W, kw), lambda s, h: (0, H_B // hg + h)),
        pl.BlockSpec((1, kw), lambda s, h: (0, H_B // hg + h)),
    ]
    args = [z, z, z, z, zt, bias_t, conv_w, cb2, conv_w, cb2]
    if has_init:
        c0, n0, m0 = init
        in_specs += [
            pl.BlockSpec((None, 2, hg, DK_B, DV_B), lambda s, h: (s, 0, h, 0, 0)),
            pl.BlockSpec((None, 2, hg, 1, DK_B), lambda s, h: (s, 0, h, 0, 0)),
            pl.BlockSpec((None, 2, hg, 1, LANE), lambda s, h: (s, 0, h, 0, 0)),
        ]
        args += [c0, n0.reshape(nseq, 2, H_B, 1, DK_B),
                 jnp.broadcast_to(m0[..., None, None], (nseq, 2, H_B, 1, LANE))]
    return pl.pallas_call(
        partial(_mlstm_kernel, t=t, hg=hg, has_init=has_init),
        grid=(nseq, H_B // hg),
        in_specs=in_specs,
        out_specs=[
            pl.BlockSpec((t, vw), lambda s, h: (s, h)),
            pl.BlockSpec((None, 2, hg, DK_B, DV_B), lambda s, h: (s, 0, h, 0, 0)),
            pl.BlockSpec((None, 2, hg, 1, DK_B), lambda s, h: (s, 0, h, 0, 0)),
            pl.BlockSpec((None, 2, hg, 1, LANE), lambda s, h: (s, 0, h, 0, 0)),
        ],
        out_shape=[
            jax.ShapeDtypeStruct((nseq * t, H_B * DV_B), BF16),
            jax.ShapeDtypeStruct((nseq, 2, H_B, DK_B, DV_B), F32),
            jax.ShapeDtypeStruct((nseq, 2, H_B, 1, DK_B), F32),
            jax.ShapeDtypeStruct((nseq, 2, H_B, 1, LANE), F32),
        ],
        scratch_shapes=[
            pltpu.VMEM((t + 2 * SUBLANE, kw), F32),
            pltpu.VMEM((t, kw), BF16),
            pltpu.VMEM((t, kw), F32),
            pltpu.VMEM((t, hg * LANE), F32),
            pltpu.VMEM((t, vw), F32),
            pltpu.VMEM((2, hg, DK_B, DV_B), F32),
            pltpu.VMEM((2, hg, 1, DK_B), F32),
        ],
        compiler_params=_params(("arbitrary", "arbitrary")),
        name="mlstm",
    )(*args)


def _ret_kernel(*refs, t, hg, use_rope, has_init):
    it = iter(refs)
    lg_ref, q_ref, k_ref, v_ref, g_ref = (next(it) for _ in range(5))
    if use_rope:
        cos_ref, sin_ref = next(it), next(it)
    s0_ref = next(it) if has_init else None
    mix_ref, sout_ref, acc_scr, st_scr = (next(it) for _ in range(4))
    nchunk = t // CHUNK
    h0 = pl.program_id(1) * hg

    row = lax.broadcasted_iota(jnp.int32, (CHUNK, CHUNK), 0)
    col = lax.broadcasted_iota(jnp.int32, (CHUNK, CHUNK), 1)
    dist = (row - col).astype(F32)
    pos = lax.broadcasted_iota(jnp.int32, (CHUNK, DK_C), 0).astype(F32)
    consts = {}
    for u in range(hg):
        for d in range(2):
            lg = lg_ref[d, h0 + u]
            if d == 0:
                dm = jnp.where(dist >= 0, jnp.exp(jnp.maximum(dist, 0.0) * lg), 0.0)
                qin = jnp.exp((pos + 1.0) * lg)
                kout = jnp.exp((CHUNK - 1.0 - pos) * lg)
            else:
                dm = jnp.where(dist <= 0, jnp.exp(jnp.maximum(-dist, 0.0) * lg), 0.0)
                qin = jnp.exp((CHUNK - pos) * lg)
                kout = jnp.exp(pos * lg)
            gch = jnp.exp(jnp.full((1, DV_C), float(CHUNK), F32) * lg)
            consts[u, d] = (dm, qin, kout, gch)

    acc_scr[...] = jnp.zeros_like(acc_scr)
    if has_init:
        st_scr[...] = s0_ref[...]
    else:
        st_scr[...] = jnp.zeros_like(st_scr)

    def chunk(cidx, u, d):
        dm, qin, kout, gch = consts[u, d]
        r = _rows(cidx, CHUNK)
        ck = slice(u * DK_C, (u + 1) * DK_C)
        cv = slice(u * DV_C, (u + 1) * DV_C)
        q = q_ref[r, ck]
        k = k_ref[r, ck] * (DK_C ** -0.5)
        if use_rope:
            cs, sn = cos_ref[r, :], sin_ref[r, :]
            q = q * cs + pltpu.roll(q, DK_C // 2, 1) * sn
            k = k * cs + pltpu.roll(k, DK_C // 2, 1) * sn
        v = v_ref[r, cv].astype(BF16)
        st = st_scr[d, u]
        s = lax.dot_general(q.astype(BF16), k.astype(BF16), NT, preferred_element_type=F32) * dm
        o = jnp.dot(s.astype(BF16), v, preferred_element_type=F32)
        o = o + jnp.dot((q * qin).astype(BF16), st.astype(BF16), preferred_element_type=F32)
        st_scr[d, u] = gch * st + lax.dot_general((k * kout).astype(BF16), v, TN, preferred_element_type=F32)
        acc_scr[r, cv] += o

    def body(c, carry):
        for u in range(hg):
            chunk(c, u, 0)
            chunk(nchunk - 1 - c, u, 1)
        return carry

    lax.fori_loop(0, nchunk, body, 0, unroll=CHUNK_UNROLL)
    _finalize_heads(acc_scr, g_ref, mix_ref, t, hg, DV_C)
    sout_ref[...] = st_scr[...]


def _retention(z, base, nseq, t, s0, lg, rope, hg):
    has_init = s0 is not None
    use_rope = rope is not None
    kw, vw = hg * DK_C, hg * DV_C
    qoff, koff = 0, H_C * DK_C // kw
    voff = 2 * H_C * DK_C // vw
    goff = voff + H_C // hg
    in_specs = [
        pl.BlockSpec(memory_space=pltpu.SMEM),
        pl.BlockSpec((t, kw), lambda s, h: (base + s, qoff + h)),
        pl.BlockSpec((t, kw), lambda s, h: (base + s, koff + h)),
        pl.BlockSpec((t, vw), lambda s, h: (base + s, voff + h)),
        pl.BlockSpec((t, vw), lambda s, h: (base + s, goff + h)),
    ]
    args = [lg, z, z, z, z]
    if use_rope:
        in_specs += [pl.BlockSpec((t, DK_C), lambda s, h: (0, 0)), pl.BlockSpec((t, DK_C), lambda s, h: (0, 0))]
        args += list(rope)
    if has_init:
        in_specs.append(pl.BlockSpec((None, 2, hg, DK_C, DV_C), lambda s, h: (s, 0, h, 0, 0)))
        args.append(s0)
    return pl.pallas_call(
        partial(_ret_kernel, t=t, hg=hg, use_rope=use_rope, has_init=has_init),
        grid=(nseq, H_C // hg),
        in_specs=in_specs,
        out_specs=[
            pl.BlockSpec((t, vw), lambda s, h: (s, h)),
            pl.BlockSpec((None, 2, hg, DK_C, DV_C), lambda s, h: (s, 0, h, 0, 0)),
        ],
        out_shape=[
            jax.ShapeDtypeStruct((nseq * t, H_C * DV_C), BF16),
            jax.ShapeDtypeStruct((nseq, 2, H_C, DK_C, DV_C), F32),
        ],
        scratch_shapes=[pltpu.VMEM((t, vw), F32), pltpu.VMEM((2, hg, DK_C, DV_C), F32)],
        compiler_params=_params(("arbitrary", "arbitrary")),
        name="retention",
    )(*args)


def _gla_intra(q, k, g2, ones, reverse):
    nsub = CHUNK // SUB
    lane = lax.broadcasted_iota(jnp.int32, (SUB, CHUNK), 1)
    rowi = lax.broadcasted_iota(jnp.int32, (SUB, CHUNK), 0)
    out = []
    for jb in range(nsub):
        lo, hi = jb * SUB, (jb + 1) * SUB
        qj, gj = q[lo:hi], g2[lo:hi]
        if reverse and jb < nsub - 1:
            gref = g2[hi:hi + 1]
            kt = (k[hi:] * jnp.exp2(gref - g2[hi:])).astype(BF16)
            kt = jnp.concatenate([jnp.zeros((hi, DK_D), BF16), kt], axis=0)
        elif not reverse and jb > 0:
            gref = g2[lo - 1:lo]
            kt = (k[:lo] * jnp.exp2(gref - g2[:lo])).astype(BF16)
            kt = jnp.concatenate([kt, jnp.zeros((CHUNK - lo, DK_D), BF16)], axis=0)
        else:
            kt = None
        if kt is None:
            off = jnp.zeros((SUB, CHUNK), F32)
        else:
            qt = (qj * jnp.exp2(gj - gref)).astype(BF16)
            off = lax.dot_general(qt, kt, NT, preferred_element_type=F32)
        prods = [qj * jnp.exp2(gj - g2[i:i + 1]) * k[i:i + 1] for i in range(lo, hi)]
        sums = jnp.dot(jnp.concatenate(prods, axis=0).astype(BF16), ones, preferred_element_type=F32)
        diag = jnp.zeros((SUB, CHUNK), F32)
        for ii in range(SUB):
            diag = jnp.where(lane == lo + ii, sums[ii * SUB:(ii + 1) * SUB, :CHUNK], diag)
        mask = (rowi + lo <= lane) if reverse else (rowi + lo >= lane)
        out.append(off + jnp.where(mask, diag, 0.0))
    return jnp.concatenate(out, axis=0)


def _gla_kernel(*refs, t, hg, has_init):
    it = iter(refs)
    q_ref, k_ref, v_ref, r_ref, a_ref, wup_ref, gb_ref = (next(it) for _ in range(7))
    s0_ref = next(it) if has_init else None
    mix_ref, sout_ref, acc_scr, st_scr, g_scr = (next(it) for _ in range(5))
    nchunk = t // CHUNK
    tile = min(t, 256)

    def gates(i, carry):
        r = _rows(i, tile)
        a = a_ref[r, :].astype(BF16)
        for u in range(hg):
            for d in range(2):
                x = jnp.dot(a, wup_ref[d, u], preferred_element_type=F32) + gb_ref[d, u]
                g_scr[d, u, r, :] = (jnp.minimum(x, 0.0) - jnp.log1p(jnp.exp(-jnp.abs(x)))) * (1.0 / GLA_TAU)
        acc_scr[r, :] = jnp.zeros((tile, hg * DV_D), F32)
        return carry

    lax.fori_loop(0, t // tile, gates, 0)

    for u in range(hg):
        for d in range(2):
            st_scr[d, u] = s0_ref[d, u].T if has_init else jnp.zeros((DV_D, DK_D), F32)

    ones = jnp.ones((DK_D, LANE), BF16)

    def chunk(cidx, u, d):
        reverse = d == 1
        last = 0 if reverse else CHUNK - 1
        r = _rows(cidx, CHUNK)
        ck = slice(u * DK_D, (u + 1) * DK_D)
        cv = slice(u * DV_D, (u + 1) * DV_D)
        g2 = jnp.dot(_tri(reverse), g_scr[d, u, r, :], precision=HIGHEST, preferred_element_type=F32) * LOG2E
        g2_last = g2[last:last + 1, :]
        q = q_ref[r, ck] * (DK_D ** -0.5)
        k = k_ref[r, ck]
        v = v_ref[r, cv].astype(BF16)
        st = st_scr[d, u]
        s = _gla_intra(q, k, g2, ones, reverse)
        o = jnp.dot(s.astype(BF16), v, preferred_element_type=F32)
        o = o + lax.dot_general((q * jnp.exp2(g2)).astype(BF16), st.astype(BF16), NT, preferred_element_type=F32)
        kd = (k * jnp.exp2(g2_last - g2)).astype(BF16)
        st_scr[d, u] = jnp.exp2(g2_last) * st + lax.dot_general(v, kd, TN, preferred_element_type=F32)
        acc_scr[r, cv] += o

    def body(c, carry):
        for u in range(hg):
            chunk(c, u, 0)
            chunk(nchunk - 1 - c, u, 1)
        return carry

    lax.fori_loop(0, nchunk, body, 0)
    _finalize_heads(acc_scr, r_ref, mix_ref, t, hg, DV_D)
    for u in range(hg):
        for d in range(2):
            sout_ref[d, u] = st_scr[d, u].T


def _gla(z, zt, base, nseq, t, s0, w_up, g_bias, hg):
    has_init = s0 is not None
    kw, vw = hg * DK_D, hg * DV_D
    cbase = 2 * H_C * DK_C + 2 * H_C * DV_C
    qoff = cbase // kw
    koff = qoff + H_D // hg
    voff = (cbase + 2 * H_D * DK_D) // vw
    roff = voff + H_D // hg
    in_specs = [
        pl.BlockSpec((t, kw), lambda s, h: (base + s, qoff + h)),
        pl.BlockSpec((t, kw), lambda s, h: (base + s, koff + h)),
        pl.BlockSpec((t, vw), lambda s, h: (base + s, voff + h)),
        pl.BlockSpec((t, vw), lambda s, h: (base + s, roff + h)),
        pl.BlockSpec((t, LANE), lambda s, h: (base + s, 0)),
        pl.BlockSpec((2, hg, LANE, DK_D), lambda s, h: (0, h, 0, 0)),
        pl.BlockSpec((2, hg, 1, DK_D), lambda s, h: (0, h, 0, 0)),
    ]
    args = [z, z, z, z, zt, w_up, g_bias]
    if has_init:
        in_specs.append(pl.BlockSpec((None, 2, hg, DK_D, DV_D), lambda s, h: (s, 0, h, 0, 0)))
        args.append(s0)
    return pl.pallas_call(
        partial(_gla_kernel, t=t, hg=hg, has_init=has_init),
        grid=(nseq, H_D // hg),
        in_specs=in_specs,
        out_specs=[
            pl.BlockSpec((t, vw), lambda s, h: (s, h)),
            pl.BlockSpec((None, 2, hg, DK_D, DV_D), lambda s, h: (s, 0, h, 0, 0)),
        ],
        out_shape=[
            jax.ShapeDtypeStruct((nseq * t, H_D * DV_D), BF16),
            jax.ShapeDtypeStruct((nseq, 2, H_D, DK_D, DV_D), F32),
        ],
        scratch_shapes=[
            pltpu.VMEM((t, vw), F32),
            pltpu.VMEM((2, hg, DV_D, DK_D), F32),
            pltpu.VMEM((2, hg, t, DK_D), F32),
        ],
        compiler_params=_params(("arbitrary", "arbitrary")),
        name="gla",
    )(*args)


def _rope_tables(t):
    pos = jnp.arange(t)
    row = (pos // GRID_W).astype(F32)
    col = (pos % GRID_W).astype(F32)
    nf = DK_C // 4
    freqs = ROPE_BASE ** (-jnp.arange(nf, dtype=F32) / nf)
    ang = jnp.concatenate([row[:, None] * freqs, col[:, None] * freqs], axis=-1)
    cos, sin = jnp.cos(ang), jnp.sin(ang)
    return jnp.concatenate([cos, cos], axis=-1), jnp.concatenate([-sin, sin], axis=-1)


def _ab_tail(w_in, i_bias, f_bias):
    n_main = w_in.shape[1] - 4 * H_B
    idx = jnp.array([[h, H_B + h, 2 * H_B + h, 3 * H_B + h] for h in range(H_B)])
    cols = w_in[:, n_main:][:, idx]
    w_tail = jnp.pad(cols, ((0, 0), (0, 0), (0, LANE - 4))).reshape(D_MODEL, H_B * LANE)
    bias = jnp.concatenate([i_bias.reshape(-1), f_bias.reshape(-1)])[idx]
    bias_t = jnp.pad(bias, ((0, 0), (0, LANE - 4))).reshape(1, H_B * LANE)
    return n_main, w_tail, bias_t


def _cd_tail(w_in, w_up, g_bias):
    n_main = w_in.shape[1] - 2 * GLA_RANK
    w_tail = jnp.pad(w_in[:, n_main:], ((0, 0), (0, LANE - 2 * GLA_RANK)))
    up = w_up.reshape(2, GLA_RANK, H_D, DK_D).transpose(0, 2, 1, 3)
    up = jnp.stack([jnp.pad(up[d], ((0, 0), (d * GLA_RANK, LANE - (d + 1) * GLA_RANK), (0, 0))) for d in range(2)])
    return n_main, w_tail, up.astype(BF16), g_bias.reshape(2, H_D, 1, DK_D)


def kernel(x_prompt, x_sample, c, state_lru, state_mlstm_C, state_mlstm_n, state_mlstm_m, state_ret, state_gla, c_ctx, w_mod, b_mod, norm_w, ffn_w_gate, ffn_w_up, ffn_w_down, w_in_ab, w_out_ab, lru_conv_w, lru_conv_b, lru_w_a, lru_b_a, lru_w_i, lru_b_i, lru_lambda, mlstm_conv_w, mlstm_conv_b, mlstm_i_bias, mlstm_f_bias, w_in_cd, w_out_cd, ret_decay_log, gla_w_up, gla_b, final_norm_w):
    Bp, Tp, D = x_prompt.shape
    Bs, Ts, _ = x_sample.shape
    n_ctx = Bp * Tp
    tm_ffn, tf_ffn, tm_in, tm_out = 1024, 256, 1024, 512
    assert all(n_ctx % t == 0 and Ts % t == 0 for t in (tm_ffn, tm_in, tm_out)) and 1 + Bs <= MOD_ROWS and n_ctx % Ts == 0
    row_ffn, row_in, row_out = (_mod_row_map(n_ctx // t, Ts // t) for t in (tm_ffn, tm_in, tm_out))
    groups = ((0, Bp, Tp), (n_ctx // Ts, Bs, Ts))
    hg_ctx = {"mlstm": 4, "ret": 4, "gla": 4}
    hg_lat = {"mlstm": 1, "ret": 2, "gla": 2}

    cond = jnp.concatenate([c_ctx[None], c, jnp.zeros((MOD_ROWS - 1 - Bs, D), F32)], axis=0)
    mod = _modulation(cond, w_mod, b_mod).reshape(DEPTH, MOD_ROWS, N_MOD, D)
    nw = norm_w.reshape(DEPTH, 3, 1, D)
    wg, wu, wd = ffn_w_gate.astype(BF16), ffn_w_up.astype(BF16), ffn_w_down.astype(BF16)
    w_in_ab_b, w_in_cd_b = w_in_ab.astype(BF16), w_in_cd.astype(BF16)
    w_out_ab_b, w_out_cd_b = w_out_ab.astype(BF16), w_out_cd.astype(BF16)

    x = jnp.concatenate([x_prompt.reshape(n_ctx, D), x_sample.reshape(Bs * Ts, D)], axis=0)
    rope = _rope_tables(Ts)

    st_lru, st_C, st_n, st_m, st_ret, st_gla = [], [], [], [], [], []
    for l in range(DEPTH):
        x = _ffn(x, mod, nw, wg, wu, wd, l, 0, row_ffn, tm_ffn, tf_ffn)
        (pb, pn, pt), (sb, sn, st) = groups
        if l % 2 == 0:
            e = l // 2
            n_main, w_tail, bias_t = _ab_tail(w_in_ab[e], mlstm_i_bias[e], mlstm_f_bias[e])
            z, zt = _inproj(x, mod, nw, w_in_ab_b, l, e, n_main, w_tail.astype(BF16), row_in, tm_in)
            lru_args = (lru_conv_w[e], lru_conv_b[e], lru_w_a[e], lru_b_a[e], lru_w_i[e], lru_b_i[e], lru_lambda[e])
            a_p, s_lru = _lru(z, pb, pn, pt, None, *lru_args)
            a_s, _ = _lru(z, sb, sn, st, state_lru[:, e], *lru_args)
            b_p, s_C, s_n, s_m = _mlstm(z, zt, bias_t, pb, pn, pt, None, mlstm_conv_w[e], mlstm_conv_b[e], hg_ctx["mlstm"])
            b_s, _, _, _ = _mlstm(z, zt, bias_t, sb, sn, st,
                                  (state_mlstm_C[:, e], state_mlstm_n[:, e], state_mlstm_m[:, e]),
                                  mlstm_conv_w[e], mlstm_conv_b[e], hg_lat["mlstm"])
            st_lru.append(s_lru)
            st_C.append(s_C)
            st_n.append(s_n.reshape(Bp, 2, H_B, DK_B))
            st_m.append(s_m[:, :, :, 0, 0])
            mix_ctx, mix_lat = (a_p, b_p), (a_s, b_s)
            w_out = w_out_ab_b
        else:
            e = l // 2
            n_main, w_tail, w_up, g_bias = _cd_tail(w_in_cd[e], gla_w_up[e], gla_b[e])
            z, zt = _inproj(x, mod, nw, w_in_cd_b, l, e, n_main, w_tail.astype(BF16), row_in, tm_in)
            c_p, s_ret = _retention(z, pb, pn, pt, None, ret_decay_log[e], None, hg_ctx["ret"])
            c_s, _ = _retention(z, sb, sn, st, state_ret[:, e], ret_decay_log[e], rope, hg_lat["ret"])
            d_p, s_gla = _gla(z, zt, pb, pn, pt, None, w_up, g_bias, hg_ctx["gla"])
            d_s, _ = _gla(z, zt, sb, sn, st, state_gla[:, e], w_up, g_bias, hg_lat["gla"])
            st_ret.append(s_ret)
            st_gla.append(s_gla)
            mix_ctx, mix_lat = (c_p, d_p), (c_s, d_s)
            w_out = w_out_cd_b
        x = _outproj(x, mix_ctx, mix_lat, mod, w_out, l, e, row_out, tm_out)
        x = _ffn(x, mod, nw, wg, wu, wd, l, 1, row_ffn, tm_ffn, tf_ffn)

    y_prompt = _final_norm(x, final_norm_w, 0, n_ctx).reshape(Bp, Tp, D)
    y_sample = _final_norm(x, final_norm_w, n_ctx, Bs * Ts).reshape(Bs, Ts, D)
    return (y_prompt, y_sample,
            jnp.stack(st_lru, axis=1), jnp.stack(st_C, axis=1), jnp.stack(st_n, axis=1), jnp.stack(st_m, axis=1),
            jnp.stack(st_ret, axis=1), jnp.stack(st_gla, axis=1))
```

```python
from functools import partial

import jax
import jax.numpy as jnp
from jax import lax
from jax.experimental import pallas as pl
from jax.experimental.pallas import tpu as pltpu

F32 = jnp.float32
BF16 = jnp.bfloat16

D_MODEL = 2048
DEPTH = 4
N_MOD = 9
D_FF = 5632
EPS = 1e-6
CHUNK = 64
CONV_W = 4
D_A = D_MODEL // 2
NB_A = 8
BS_A = D_A // NB_A
LRU_C = 8.0
H_B = 4
DK_B = D_MODEL // 8
DV_B = D_MODEL // 8
H_C = 4
DK_C = D_MODEL // 16
DV_C = D_MODEL // 8
H_D = 4
DK_D = D_MODEL // 16
DV_D = D_MODEL // 8
GLA_RANK = 16
GLA_TAU = 16.0
ROPE_BASE = 10000.0
GRID_W = 64

MOD_ROWS = 8
LANE = 128
SUBLANE = 8
SUB = 16
MLSTM_CHUNK = 256
RET_CHUNK = 256
GLA_CHUNK = 128
VMEM_LIMIT = 56 * 1024 * 1024
NEG_INF = float("-inf")
LOG2E = 1.4426950408889634
HIGHEST = lax.Precision.HIGHEST
NT = (((1,), (1,)), ((), ()))
TN = (((0,), (0,)), ((), ()))


def _params(sem):
    return pltpu.CompilerParams(dimension_semantics=sem, vmem_limit_bytes=VMEM_LIMIT)


def _silu(x):
    return x * jax.nn.sigmoid(x)


def _rows(i, n):
    return pl.ds(pl.multiple_of(i * n, n), n)


def _mod_kernel(c_ref, w_ref, b_ref, o_ref):
    s = _silu(c_ref[...]).astype(BF16)
    o_ref[...] = jnp.dot(s, w_ref[...].astype(BF16), preferred_element_type=F32) + b_ref[...]


def _modulation(cond, w_mod, b_mod, tn=1024):
    n = w_mod.shape[-1]
    return pl.pallas_call(
        _mod_kernel,
        grid=(DEPTH, n // tn),
        in_specs=[
            pl.BlockSpec((MOD_ROWS, D_MODEL), lambda l, j: (0, 0)),
            pl.BlockSpec((None, D_MODEL, tn), lambda l, j: (l, 0, j)),
            pl.BlockSpec((None, 1, tn), lambda l, j: (l, 0, j)),
        ],
        out_specs=pl.BlockSpec((None, MOD_ROWS, tn), lambda l, j: (l, 0, j)),
        out_shape=jax.ShapeDtypeStruct((DEPTH, MOD_ROWS, n), F32),
        compiler_params=_params(("arbitrary", "arbitrary")),
        name="modulation",
    )(cond, w_mod, b_mod.reshape(DEPTH, 1, n))


def _norm_mod(x, nw, shift, scale):
    ms = jnp.mean(x * x, axis=-1, keepdims=True)
    y = x * lax.rsqrt(ms + EPS) * nw
    return y * (1.0 + scale) + shift


def _mod_row_map(n_ctx_tiles, tiles_per_req):
    def row(i):
        return jnp.where(i < n_ctx_tiles, 0, 1 + (i - n_ctx_tiles) // tiles_per_req)
    return row


def _ffn_kernel(x_ref, mod_ref, nw_ref, wg_ref, wu_ref, wd_ref, o_ref, h_scr, *, k, nf):
    f = pl.program_id(1)

    @pl.when(f == 0)
    def _():
        h = _norm_mod(x_ref[...], nw_ref[...], mod_ref[3 * k:3 * k + 1, :], mod_ref[3 * k + 1:3 * k + 2, :])
        h_scr[...] = h.astype(BF16)
        o_ref[...] = jnp.zeros_like(o_ref)

    h = h_scr[...]
    g = jnp.dot(h, wg_ref[...], preferred_element_type=F32)
    u = jnp.dot(h, wu_ref[...], preferred_element_type=F32)
    a = (_silu(g) * u).astype(BF16)
    o_ref[...] += jnp.dot(a, wd_ref[...], preferred_element_type=F32)

    @pl.when(f == nf - 1)
    def _():
        o_ref[...] = x_ref[...] + (0.5 * mod_ref[3 * k + 2:3 * k + 3, :]) * o_ref[...]


def _ffn(x, mod, nw, wg, wu, wd, l, j, row_of_tile, tm, tf):
    t = x.shape[0]
    nf = D_FF // tf
    k = 2 * j
    return pl.pallas_call(
        partial(_ffn_kernel, k=k, nf=nf),
        grid=(t // tm, nf),
        in_specs=[
            pl.BlockSpec((tm, D_MODEL), lambda i, f: (i, 0)),
            pl.BlockSpec((None, None, N_MOD, D_MODEL), lambda i, f: (l, row_of_tile(i), 0, 0)),
            pl.BlockSpec((None, None, 1, D_MODEL), lambda i, f: (l, k, 0, 0)),
            pl.BlockSpec((None, None, D_MODEL, tf), lambda i, f: (l, j, 0, f)),
            pl.BlockSpec((None, None, D_MODEL, tf), lambda i, f: (l, j, 0, f)),
            pl.BlockSpec((None, None, tf, D_MODEL), lambda i, f: (l, j, f, 0)),
        ],
        out_specs=pl.BlockSpec((tm, D_MODEL), lambda i, f: (i, 0)),
        out_shape=jax.ShapeDtypeStruct((t, D_MODEL), F32),
        scratch_shapes=[pltpu.VMEM((tm, D_MODEL), BF16)],
        compiler_params=_params(("arbitrary", "arbitrary")),
        name="ffn",
    )(x, mod, nw, wg, wu, wd)


def _inproj_kernel(x_ref, mod_ref, nw_ref, w_ref, wt_ref, z_ref, zt_ref, h_scr):
    j = pl.program_id(1)

    @pl.when(j == 0)
    def _():
        h = _norm_mod(x_ref[...], nw_ref[...], mod_ref[3:4, :], mod_ref[4:5, :])
        h_scr[...] = h.astype(BF16)
        zt_ref[...] = jnp.dot(h_scr[...], wt_ref[...], preferred_element_type=F32)

    z_ref[...] = jnp.dot(h_scr[...], w_ref[...], preferred_element_type=F32)


def _inproj(x, mod, nw, w_in, l, e, n, w_tail, row_of_tile, tm, tn=1024):
    t = x.shape[0]
    tw = w_tail.shape[1]
    return pl.pallas_call(
        _inproj_kernel,
        grid=(t // tm, n // tn),
        in_specs=[
            pl.BlockSpec((tm, D_MODEL), lambda i, j: (i, 0)),
            pl.BlockSpec((None, None, N_MOD, D_MODEL), lambda i, j: (l, row_of_tile(i), 0, 0)),
            pl.BlockSpec((None, None, 1, D_MODEL), lambda i, j: (l, 1, 0, 0)),
            pl.BlockSpec((None, D_MODEL, tn), lambda i, j: (e, 0, j)),
            pl.BlockSpec((D_MODEL, tw), lambda i, j: (0, 0)),
        ],
        out_specs=[
            pl.BlockSpec((tm, tn), lambda i, j: (i, j)),
            pl.BlockSpec((tm, tw), lambda i, j: (i, 0)),
        ],
        out_shape=[jax.ShapeDtypeStruct((t, n), F32), jax.ShapeDtypeStruct((t, tw), F32)],
        scratch_shapes=[pltpu.VMEM((tm, D_MODEL), BF16)],
        compiler_params=_params(("arbitrary", "arbitrary")),
        name="inproj",
    )(x, mod, nw, w_in, w_tail)


def _outproj_kernel(x_ref, ap_ref, bp_ref, as_ref, bs_ref, mod_ref, wa_ref, wb_ref, o_ref, *, n_ctx_tiles):
    def emit(ma_ref, mb_ref):
        y = jnp.dot(ma_ref[...], wa_ref[...], preferred_element_type=F32)
        y = y + jnp.dot(mb_ref[...], wb_ref[...], preferred_element_type=F32)
        o_ref[...] = x_ref[...] + mod_ref[5:6, :] * y

    is_ctx = pl.program_id(0) < n_ctx_tiles
    pl.when(is_ctx)(lambda: emit(ap_ref, bp_ref))
    pl.when(jnp.logical_not(is_ctx))(lambda: emit(as_ref, bs_ref))


def _outproj(x, mix_ctx, mix_lat, mod, w_out, l, e, row_of_tile, tm):
    t = x.shape[0]
    half = D_MODEL // 2
    n_ctx_tiles = mix_ctx[0].shape[0] // tm
    n_lat_tiles = mix_lat[0].shape[0] // tm
    ctx_blk = lambda i: (jnp.minimum(i, n_ctx_tiles - 1), 0)
    lat_blk = lambda i: (jnp.clip(i - n_ctx_tiles, 0, n_lat_tiles - 1), 0)
    return pl.pallas_call(
        partial(_outproj_kernel, n_ctx_tiles=n_ctx_tiles),
        grid=(t // tm,),
        in_specs=[
            pl.BlockSpec((tm, D_MODEL), lambda i: (i, 0)),
            pl.BlockSpec((tm, half), ctx_blk),
            pl.BlockSpec((tm, half), ctx_blk),
            pl.BlockSpec((tm, half), lat_blk),
            pl.BlockSpec((tm, half), lat_blk),
            pl.BlockSpec((None, None, N_MOD, D_MODEL), lambda i: (l, row_of_tile(i), 0, 0)),
            pl.BlockSpec((None, half, D_MODEL), lambda i: (e, 0, 0)),
            pl.BlockSpec((None, half, D_MODEL), lambda i: (e, 1, 0)),
        ],
        out_specs=pl.BlockSpec((tm, D_MODEL), lambda i: (i, 0)),
        out_shape=jax.ShapeDtypeStruct((t, D_MODEL), F32),
        compiler_params=_params(("arbitrary",)),
        name="outproj",
    )(x, *mix_ctx, *mix_lat, mod, w_out, w_out)


def _final_norm_kernel(x_ref, w_ref, o_ref):
    x = x_ref[...]
    ms = jnp.mean(x * x, axis=-1, keepdims=True)
    o_ref[...] = x * lax.rsqrt(ms + EPS) * w_ref[...]


def _final_norm(x, w, row0, rows, tm=512):
    base = row0 // tm
    return pl.pallas_call(
        _final_norm_kernel,
        grid=(rows // tm,),
        in_specs=[pl.BlockSpec((tm, D_MODEL), lambda i: (base + i, 0)), pl.BlockSpec((1, D_MODEL), lambda i: (0, 0))],
        out_specs=pl.BlockSpec((tm, D_MODEL), lambda i: (i, 0)),
        out_shape=jax.ShapeDtypeStruct((rows, D_MODEL), F32),
        compiler_params=_params(("arbitrary",)),
        name="final_norm",
    )(x, w.reshape(1, D_MODEL))


def _finalize_heads(acc_scr, gate_ref, mix_ref, t, hg, dv):
    tile = min(t, 256)

    def body(i, carry):
        r = _rows(i, tile)
        for u in range(hg):
            cv = slice(u * dv, (u + 1) * dv)
            o = acc_scr[r, cv]
            d = o - jnp.mean(o, axis=-1, keepdims=True)
            var = jnp.mean(d * d, axis=-1, keepdims=True)
            mix_ref[r, cv] = (d * lax.rsqrt(var + EPS) * _silu(gate_ref[r, cv])).astype(BF16)
        return carry

    lax.fori_loop(0, t // tile, body, 0)


def _dwconv_to(dst_ref, src_ref, pad_scr, w_ref, b_ref, t, post):
    c = src_ref.shape[1]
    pad_scr[pl.ds(0, SUBLANE), :] = jnp.zeros((SUBLANE, c), F32)
    pad_scr[pl.ds(t + SUBLANE, SUBLANE), :] = jnp.zeros((SUBLANE, c), F32)
    pad_scr[pl.ds(SUBLANE, t), :] = src_ref[...]
    tile = min(t, 256)
    for r0 in range(0, t, tile):
        y = b_ref[...]
        for k in range(CONV_W):
            y = y + w_ref[k:k + 1, :] * pad_scr[pl.ds(r0 + SUBLANE - CONV_W // 2 + k, tile), :]
        dst_ref[pl.ds(r0, tile), :] = post(y).astype(dst_ref.dtype)


def _tri(reverse, n):
    row = lax.broadcasted_iota(jnp.int32, (n, n), 0)
    col = lax.broadcasted_iota(jnp.int32, (n, n), 1)
    return jnp.where((col >= row) if reverse else (col <= row), 1.0, 0.0).astype(F32)


def _scan_block(a, b, rowid, reverse):
    for s in (1, 2, 4):
        if reverse:
            valid = rowid < SUBLANE - s
            shift = SUBLANE - s
        else:
            valid = rowid >= s
            shift = s
        a_sh = jnp.where(valid, pltpu.roll(a, shift, 0), 1.0)
        b_sh = jnp.where(valid, pltpu.roll(b, shift, 0), 0.0)
        b = b + a * b_sh
        a = a * a_sh
    return a, b


def _lru_kernel(*refs, t, cb, has_init):
    it = iter(refs)
    x_ref, y_ref, cw_ref, cbias_ref, wa_ref, ba_ref, wi_ref, bi_ref, lam_ref = (next(it) for _ in range(9))
    h0_ref = next(it) if has_init else None
    out_ref, sout_ref, pad_scr, xc_scr, a_scr, b_scr = (next(it) for _ in range(6))
    nb = cb // BS_A

    _dwconv_to(xc_scr, x_ref, pad_scr, cw_ref, cbias_ref, t, lambda v: v)

    lam = lam_ref[...]
    sp = jnp.maximum(-lam, 0.0) + jnp.log1p(jnp.exp(-jnp.abs(lam)))
    tile = min(t, 256)

    def gates(i, carry):
        r = _rows(i, tile)
        xc = xc_scr[r, :]
        xcb = xc.astype(BF16)
        for d in range(2):
            ra = jnp.concatenate([jnp.dot(xcb[:, n * BS_A:(n + 1) * BS_A], wa_ref[d, n], preferred_element_type=F32)
                                  for n in range(nb)], axis=1) + ba_ref[d:d + 1, :]
            ia = jnp.concatenate([jnp.dot(xcb[:, n * BS_A:(n + 1) * BS_A], wi_ref[d, n], preferred_element_type=F32)
                                  for n in range(nb)], axis=1) + bi_ref[d:d + 1, :]
            log_a = -LRU_C * jax.nn.sigmoid(ra) * sp[d:d + 1, :]
            th = jnp.tanh(log_a)
            mult = jnp.sqrt(-2.0 * th / (1.0 - th))
            a_scr[d, r, :] = jnp.exp(log_a)
            b_scr[d, r, :] = mult * jax.nn.sigmoid(ia) * xc
        return carry

    lax.fori_loop(0, t // tile, gates, 0)

    rowid = lax.broadcasted_iota(jnp.int32, (SUBLANE, cb), 0)
    nblk = t // SUBLANE

    def scan(i, carry):
        hf, hb = carry
        rf = _rows(i, SUBLANE)
        rb = _rows(nblk - 1 - i, SUBLANE)
        af, bf = _scan_block(a_scr[0, rf, :], b_scr[0, rf, :], rowid, False)
        ab, bb = _scan_block(a_scr[1, rb, :], b_scr[1, rb, :], rowid, True)
        hf_blk = bf + af * hf
        hb_blk = bb + ab * hb
        a_scr[0, rf, :] = hf_blk
        a_scr[1, rb, :] = hb_blk
        return hf_blk[SUBLANE - 1:SUBLANE, :], hb_blk[0:1, :]

    if has_init:
        init = (h0_ref[0:1, :], h0_ref[1:2, :])
    else:
        init = (jnp.zeros((1, cb), F32), jnp.zeros((1, cb), F32))
    hf, hb = lax.fori_loop(0, nblk, scan, init, unroll=2)
    sout_ref[0:1, :] = hf
    sout_ref[1:2, :] = hb

    def fin(i, carry):
        r = _rows(i, tile)
        out_ref[r, :] = ((a_scr[0, r, :] + a_scr[1, r, :]) * jax.nn.gelu(y_ref[r, :])).astype(BF16)
        return carry

    lax.fori_loop(0, t // tile, fin, 0)


def _lru(z, base, nseq, t, h0, conv_w, conv_b, w_a, b_a, w_i, b_i, lam, cb=256):
    nj = D_A // cb
    nb = cb // BS_A
    has_init = h0 is not None
    in_specs = [
        pl.BlockSpec((t, cb), lambda s, j: (base + s, j)),
        pl.BlockSpec((t, cb), lambda s, j: (base + s, nj + j)),
        pl.BlockSpec((CONV_W, cb), lambda s, j: (0, j)),
        pl.BlockSpec((1, cb), lambda s, j: (0, j)),
        pl.BlockSpec((2, nb, BS_A, BS_A), lambda s, j: (0, j, 0, 0)),
        pl.BlockSpec((2, cb), lambda s, j: (0, j)),
        pl.BlockSpec((2, nb, BS_A, BS_A), lambda s, j: (0, j, 0, 0)),
        pl.BlockSpec((2, cb), lambda s, j: (0, j)),
        pl.BlockSpec((2, cb), lambda s, j: (0, j)),
    ]
    args = [z, z, conv_w, conv_b.reshape(1, D_A), w_a.astype(BF16), b_a, w_i.astype(BF16), b_i, lam]
    if has_init:
        in_specs.append(pl.BlockSpec((None, 2, cb), lambda s, j: (s, 0, j)))
        args.append(h0)
    return pl.pallas_call(
        partial(_lru_kernel, t=t, cb=cb, has_init=has_init),
        grid=(nseq, nj),
        in_specs=in_specs,
        out_specs=[
            pl.BlockSpec((t, cb), lambda s, j: (s, j)),
            pl.BlockSpec((None, 2, cb), lambda s, j: (s, 0, j)),
        ],
        out_shape=[jax.ShapeDtypeStruct((nseq * t, D_A), BF16), jax.ShapeDtypeStruct((nseq, 2, D_A), F32)],
        scratch_shapes=[
            pltpu.VMEM((t + 2 * SUBLANE, cb), F32),
            pltpu.VMEM((t, cb), F32),
            pltpu.VMEM((2, t, cb), F32),
            pltpu.VMEM((2, t, cb), F32),
        ],
        compiler_params=_params(("arbitrary", "arbitrary")),
        name="rglru",
    )(*args)


def _mlstm_kernel(*refs, t, hg, has_init):
    it = iter(refs)
    q_ref, k_ref, v_ref, og_ref, zt_ref, bt_ref, cwq_ref, cbq_ref, cwk_ref, cbk_ref = (next(it) for _ in range(10))
    if has_init:
        c0_ref, n0_ref, m0_ref = next(it), next(it), next(it)
    mix_ref, cout_ref, nout_ref, mout_ref = (next(it) for _ in range(4))
    pad_scr, q_scr, k_scr, gate_scr, acc_scr, c_scr, n_scr = (next(it) for _ in range(7))
    CHUNK = MLSTM_CHUNK
    nchunk = t // CHUNK

    _dwconv_to(q_scr, q_ref, pad_scr, cwq_ref, cbq_ref, t, lambda v: _silu(v) * (DK_B ** -0.5))
    _dwconv_to(k_scr, k_ref, pad_scr, cwk_ref, cbk_ref, t, _silu)

    tile = min(t, 256)
    lane_t = lax.broadcasted_iota(jnp.int32, (tile, hg * LANE), 1) % LANE

    def gates(i, carry):
        r = _rows(i, tile)
        x = zt_ref[r, :] + bt_ref[...]
        logsig = jnp.minimum(x, 0.0) - jnp.log1p(jnp.exp(-jnp.abs(x)))
        gate_scr[r, :] = jnp.where(lane_t < 2, x, logsig)
        acc_scr[r, :] = jnp.zeros((tile, hg * DV_B), F32)
        return carry

    lax.fori_loop(0, t // tile, gates, 0)

    if has_init:
        c_scr[...] = c0_ref[...]
        n_scr[...] = n0_ref[...]
        m_init = tuple(m0_ref[d, u, :, 0:1] for u in range(hg) for d in range(2))
    else:
        c_scr[...] = jnp.zeros_like(c_scr)
        n_scr[...] = jnp.zeros_like(n_scr)
        m_init = tuple(jnp.zeros((1, 1), F32) for _ in range(2 * hg))

    row = lax.broadcasted_iota(jnp.int32, (CHUNK, CHUNK), 0)
    col = lax.broadcasted_iota(jnp.int32, (CHUNK, CHUNK), 1)
    lane_c = lax.broadcasted_iota(jnp.int32, (CHUNK, LANE), 1)

    def chunk(cidx, u, d, m):
        reverse = d == 1
        li, lf = d, 2 + d
        last = 0 if reverse else CHUNK - 1
        r = _rows(cidx, CHUNK)
        ck = slice(u * DK_B, (u + 1) * DK_B)
        cv = slice(u * DV_B, (u + 1) * DV_B)
        gt = gate_scr[r, u * LANE:(u + 1) * LANE]
        fsum = jnp.dot(_tri(reverse, CHUNK), gt, precision=HIGHEST, preferred_element_type=F32)
        y = jnp.where(lane_c == lf, fsum, gt)
        yt = y.T
        f_col, i_col = y[:, lf:lf + 1], y[:, li:li + 1]
        f_row, i_row = yt[lf:lf + 1, :], yt[li:li + 1, :]
        b = f_col + m
        mask = (col >= row) if reverse else (col <= row)
        dlog = jnp.where(mask, f_col - f_row + i_row, NEG_INF)
        m_t = jnp.maximum(b, jnp.max(dlog, axis=-1, keepdims=True))
        w = jnp.exp(dlog - m_t)
        inter = jnp.exp(b - m_t)
        q = q_scr[r, ck]
        k32 = k_scr[r, ck]
        v = v_ref[r, cv].astype(BF16)
        s = lax.dot_general(q, k32.astype(BF16), NT, preferred_element_type=F32) * w
        c_old = c_scr[d, u]
        n_old = n_scr[d, u]
        num = jnp.dot(s.astype(BF16), v, preferred_element_type=F32)
        num = num + inter * jnp.dot(q, c_old.astype(BF16), preferred_element_type=F32)
        den = jnp.sum(s, axis=-1, keepdims=True) + inter * jnp.sum(q.astype(F32) * n_old, axis=-1, keepdims=True)
        acc_scr[r, cv] += num / jnp.maximum(jnp.abs(den), jnp.exp(-m_t))
        m_new = m_t[last:last + 1, :]
        f_last = f_col[last:last + 1, :]
        kw = jnp.exp(f_last - f_col + i_col - m_new)
        decay = jnp.exp(f_last + m - m_new)
        kk = k32 * kw
        c_scr[d, u] = decay * c_old + lax.dot_general(kk.astype(BF16), v, TN, preferred_element_type=F32)
        n_scr[d, u] = decay * n_old + jnp.sum(kk, axis=0, keepdims=True)
        return m_new

    def body(c, carry):
        out = []
        for u in range(hg):
            out.append(chunk(c, u, 0, carry[2 * u]))
            out.append(chunk(nchunk - 1 - c, u, 1, carry[2 * u + 1]))
        return tuple(out)

    m_fin = lax.fori_loop(0, nchunk, body, m_init)

    _finalize_heads(acc_scr, og_ref, mix_ref, t, hg, DV_B)
    cout_ref[...] = c_scr[...]
    nout_ref[...] = n_scr[...]
    for u in range(hg):
        for d in range(2):
            mout_ref[d, u] = jnp.broadcast_to(m_fin[2 * u + d], (1, LANE))


def _mlstm(z, zt, bias_t, base, nseq, t, init, conv_w, conv_b, hg):
    has_init = init is not None
    kw, vw = hg * DK_B, hg * DV_B
    qoff = 2 * D_A // kw
    koff = qoff + H_B // hg
    voff = koff + H_B // hg
    goff = voff + H_B // hg
    cb2 = conv_b.reshape(1, 2 * H_B * DK_B)
    in_specs = [
        pl.BlockSpec((t, kw), lambda s, h: (base + s, qoff + h)),
        pl.BlockSpec((t, kw), lambda s, h: (base + s, koff + h)),
        pl.BlockSpec((t, vw), lambda s, h: (base + s, voff + h)),
        pl.BlockSpec((t, vw), lambda s, h: (base + s, goff + h)),
        pl.BlockSpec((t, hg * LANE), lambda s, h: (base + s, h)),
        pl.BlockSpec((1, hg * LANE), lambda s, h: (0, h)),
        pl.BlockSpec((CONV_W, kw), lambda s, h: (0, h)),
        pl.BlockSpec((1, kw), lambda s, h: (0, h)),
        pl.BlockSpec((CONV_W, kw), lambda s, h: (0, H_B // hg + h)),
        pl.BlockSpec((1, kw), lambda s, h: (0, H_B // hg + h)),
    ]
    args = [z, z, z, z, zt, bias_t, conv_w, cb2, conv_w, cb2]
    if has_init:
        c0, n0, m0 = init
        in_specs += [
            pl.BlockSpec((None, 2, hg, DK_B, DV_B), lambda s, h: (s, 0, h, 0, 0)),
            pl.BlockSpec((None, 2, hg, 1, DK_B), lambda s, h: (s, 0, h, 0, 0)),
            pl.BlockSpec((None, 2, hg, 1, LANE), lambda s, h: (s, 0, h, 0, 0)),
        ]
        args += [c0, n0.reshape(nseq, 2, H_B, 1, DK_B),
                 jnp.broadcast_to(m0[..., None, None], (nseq, 2, H_B, 1, LANE))]
    return pl.pallas_call(
        partial(_mlstm_kernel, t=t, hg=hg, has_init=has_init),
        grid=(nseq, H_B // hg),
        in_specs=in_specs,
        out_specs=[
            pl.BlockSpec((t, vw), lambda s, h: (s, h)),
            pl.BlockSpec((None, 2, hg, DK_B, DV_B), lambda s, h: (s, 0, h, 0, 0)),
            pl.BlockSpec((None, 2, hg, 1, DK_B), lambda s, h: (s, 0, h, 0, 0)),
            pl.BlockSpec((None, 2, hg, 1, LANE), lambda s, h: (s, 0, h, 0, 0)),
        ],
        out_shape=[
            jax.ShapeDtypeStruct((nseq * t, H_B * DV_B), BF16),
            jax.ShapeDtypeStruct((nseq, 2, H_B, DK_B, DV_B), F32),
            jax.ShapeDtypeStruct((nseq, 2, H_B, 1, DK_B), F32),
            jax.ShapeDtypeStruct((nseq, 2, H_B, 1, LANE), F32),
        ],
        scratch_shapes=[
            pltpu.VMEM((t + 2 * SUBLANE, kw), F32),
            pltpu.VMEM((t, kw), BF16),
            pltpu.VMEM((t, kw), F32),
            pltpu.VMEM((t, hg * LANE), F32),
            pltpu.VMEM((t, vw), F32),
            pltpu.VMEM((2, hg, DK_B, DV_B), F32),
            pltpu.VMEM((2, hg, 1, DK_B), F32),
        ],
        compiler_params=_params(("arbitrary", "arbitrary")),
        name="mlstm",
    )(*args)


def _ret_kernel(*refs, t, hg, use_rope, has_init):
    it = iter(refs)
    lg_ref, q_ref, k_ref, v_ref, g_ref = (next(it) for _ in range(5))
    if use_rope:
        cos_ref, sin_ref = next(it), next(it)
    s0_ref = next(it) if has_init else None
    mix_ref, sout_ref, acc_scr, st_scr = (next(it) for _ in range(4))
    CHUNK = RET_CHUNK
    nchunk = t // CHUNK
    h0 = pl.program_id(1) * hg

    row = lax.broadcasted_iota(jnp.int32, (CHUNK, CHUNK), 0)
    col = lax.broadcasted_iota(jnp.int32, (CHUNK, CHUNK), 1)
    dist = (row - col).astype(F32)
    pos = lax.broadcasted_iota(jnp.int32, (CHUNK, DK_C), 0).astype(F32)
    consts = {}
    for u in range(hg):
        for d in range(2):
            lg = lg_ref[d, h0 + u]
            if d == 0:
                dm = jnp.where(dist >= 0, jnp.exp(jnp.maximum(dist, 0.0) * lg), 0.0)
                qin = jnp.exp((pos + 1.0) * lg)
                kout = jnp.exp((CHUNK - 1.0 - pos) * lg)
            else:
                dm = jnp.where(dist <= 0, jnp.exp(jnp.maximum(-dist, 0.0) * lg), 0.0)
                qin = jnp.exp((CHUNK - pos) * lg)
                kout = jnp.exp(pos * lg)
            gch = jnp.exp(jnp.full((1, DV_C), float(CHUNK), F32) * lg)
            consts[u, d] = (dm, qin, kout, gch)

    acc_scr[...] = jnp.zeros_like(acc_scr)
    if has_init:
        st_scr[...] = s0_ref[...]
    else:
        st_scr[...] = jnp.zeros_like(st_scr)

    def chunk(cidx, u, d):
        dm, qin, kout, gch = consts[u, d]
        r = _rows(cidx, CHUNK)
        ck = slice(u * DK_C, (u + 1) * DK_C)
        cv = slice(u * DV_C, (u + 1) * DV_C)
        q = q_ref[r, ck]
        k = k_ref[r, ck] * (DK_C ** -0.5)
        if use_rope:
            cs, sn = cos_ref[r, :], sin_ref[r, :]
            q = q * cs + pltpu.roll(q, DK_C // 2, 1) * sn
            k = k * cs + pltpu.roll(k, DK_C // 2, 1) * sn
        v = v_ref[r, cv].astype(BF16)
        st = st_scr[d, u]
        s = lax.dot_general(q.astype(BF16), k.astype(BF16), NT, preferred_element_type=F32) * dm
        o = jnp.dot(s.astype(BF16), v, preferred_element_type=F32)
        o = o + jnp.dot((q * qin).astype(BF16), st.astype(BF16), preferred_element_type=F32)
        st_scr[d, u] = gch * st + lax.dot_general((k * kout).astype(BF16), v, TN, preferred_element_type=F32)
        acc_scr[r, cv] += o

    def body(c, carry):
        for u in range(hg):
            chunk(c, u, 0)
            chunk(nchunk - 1 - c, u, 1)
        return carry

    lax.fori_loop(0, nchunk, body, 0)
    _finalize_heads(acc_scr, g_ref, mix_ref, t, hg, DV_C)
    sout_ref[...] = st_scr[...]


def _retention(z, base, nseq, t, s0, lg, rope, hg):
    has_init = s0 is not None
    use_rope = rope is not None
    kw, vw = hg * DK_C, hg * DV_C
    qoff, koff = 0, H_C * DK_C // kw
    voff = 2 * H_C * DK_C // vw
    goff = voff + H_C // hg
    in_specs = [
        pl.BlockSpec(memory_space=pltpu.SMEM),
        pl.BlockSpec((t, kw), lambda s, h: (base + s, qoff + h)),
        pl.BlockSpec((t, kw), lambda s, h: (base + s, koff + h)),
        pl.BlockSpec((t, vw), lambda s, h: (base + s, voff + h)),
        pl.BlockSpec((t, vw), lambda s, h: (base + s, goff + h)),
    ]
    args = [lg, z, z, z, z]
    if use_rope:
        in_specs += [pl.BlockSpec((t, DK_C), lambda s, h: (0, 0)), pl.BlockSpec((t, DK_C), lambda s, h: (0, 0))]
        args += list(rope)
    if has_init:
        in_specs.append(pl.BlockSpec((None, 2, hg, DK_C, DV_C), lambda s, h: (s, 0, h, 0, 0)))
        args.append(s0)
    return pl.pallas_call(
        partial(_ret_kernel, t=t, hg=hg, use_rope=use_rope, has_init=has_init),
        grid=(nseq, H_C // hg),
        in_specs=in_specs,
        out_specs=[
            pl.BlockSpec((t, vw), lambda s, h: (s, h)),
            pl.BlockSpec((None, 2, hg, DK_C, DV_C), lambda s, h: (s, 0, h, 0, 0)),
        ],
        out_shape=[
            jax.ShapeDtypeStruct((nseq * t, H_C * DV_C), BF16),
            jax.ShapeDtypeStruct((nseq, 2, H_C, DK_C, DV_C), F32),
        ],
        scratch_shapes=[pltpu.VMEM((t, vw), F32), pltpu.VMEM((2, hg, DK_C, DV_C), F32)],
        compiler_params=_params(("arbitrary", "arbitrary")),
        name="retention",
    )(*args)


def _gla_intra(q, k, g2, reverse):
    CHUNK = GLA_CHUNK
    nsub = CHUNK // SUB
    lane = lax.broadcasted_iota(jnp.int32, (SUB, CHUNK), 1)
    rowi = lax.broadcasted_iota(jnp.int32, (SUB, CHUNK), 0)
    out = []
    for jb in range(nsub):
        lo, hi = jb * SUB, (jb + 1) * SUB
        qj, gj = q[lo:hi], g2[lo:hi]
        if reverse and jb < nsub - 1:
            gref = g2[hi:hi + 1]
            kt = (k[hi:] * jnp.exp2(gref - g2[hi:])).astype(BF16)
            kt = jnp.concatenate([jnp.zeros((hi, DK_D), BF16), kt], axis=0)
        elif not reverse and jb > 0:
            gref = g2[lo - 1:lo]
            kt = (k[:lo] * jnp.exp2(gref - g2[:lo])).astype(BF16)
            kt = jnp.concatenate([kt, jnp.zeros((CHUNK - lo, DK_D), BF16)], axis=0)
        else:
            kt = None
        if kt is None:
            off = jnp.zeros((SUB, CHUNK), F32)
        else:
            qt = (qj * jnp.exp2(gj - gref)).astype(BF16)
            off = lax.dot_general(qt, kt, NT, preferred_element_type=F32)
        diag = jnp.zeros((SUB, CHUNK), F32)
        for i in range(lo, hi):
            col = jnp.sum(qj * jnp.exp2(gj - g2[i:i + 1]) * k[i:i + 1], axis=-1, keepdims=True)
            diag = jnp.where(lane == i, col, diag)
        mask = (rowi + lo <= lane) if reverse else (rowi + lo >= lane)
        out.append(off + jnp.where(mask, diag, 0.0))
    return jnp.concatenate(out, axis=0)


def _gla_kernel(*refs, t, hg, has_init):
    it = iter(refs)
    q_ref, k_ref, v_ref, r_ref, a_ref, wup_ref, gb_ref = (next(it) for _ in range(7))
    s0_ref = next(it) if has_init else None
    mix_ref, sout_ref, acc_scr, st_scr, g_scr = (next(it) for _ in range(5))
    CHUNK = GLA_CHUNK
    nchunk = t // CHUNK
    tile = min(t, 256)

    def gates(i, carry):
        r = _rows(i, tile)
        a = a_ref[r, :].astype(BF16)
        for u in range(hg):
            for d in range(2):
                x = jnp.dot(a, wup_ref[d, u], preferred_element_type=F32) + gb_ref[d, u]
                g_scr[d, u, r, :] = (jnp.minimum(x, 0.0) - jnp.log1p(jnp.exp(-jnp.abs(x)))) * (1.0 / GLA_TAU)
        acc_scr[r, :] = jnp.zeros((tile, hg * DV_D), F32)
        return carry

    lax.fori_loop(0, t // tile, gates, 0)

    for u in range(hg):
        for d in range(2):
            st_scr[d, u] = s0_ref[d, u].T if has_init else jnp.zeros((DV_D, DK_D), F32)

    def chunk(cidx, u, d):
        reverse = d == 1
        last = 0 if reverse else CHUNK - 1
        r = _rows(cidx, CHUNK)
        ck = slice(u * DK_D, (u + 1) * DK_D)
        cv = slice(u * DV_D, (u + 1) * DV_D)
        g2 = jnp.dot(_tri(reverse, CHUNK), g_scr[d, u, r, :], precision=HIGHEST, preferred_element_type=F32) * LOG2E
        g2_last = g2[last:last + 1, :]
        q = q_ref[r, ck] * (DK_D ** -0.5)
        k = k_ref[r, ck]
        v = v_ref[r, cv].astype(BF16)
        st = st_scr[d, u]
        s = _gla_intra(q, k, g2, reverse)
        o = jnp.dot(s.astype(BF16), v, preferred_element_type=F32)
        o = o + lax.dot_general((q * jnp.exp2(g2)).astype(BF16), st.astype(BF16), NT, preferred_element_type=F32)
        kd = (k * jnp.exp2(g2_last - g2)).astype(BF16)
        st_scr[d, u] = jnp.exp2(g2_last) * st + lax.dot_general(v, kd, TN, preferred_element_type=F32)
        acc_scr[r, cv] += o

    def body(c, carry):
        for u in range(hg):
            chunk(c, u, 0)
            chunk(nchunk - 1 - c, u, 1)
        return carry

    lax.fori_loop(0, nchunk, body, 0)
    _finalize_heads(acc_scr, r_ref, mix_ref, t, hg, DV_D)
    for u in range(hg):
        for d in range(2):
            sout_ref[d, u] = st_scr[d, u].T


def _gla(z, zt, base, nseq, t, s0, w_up, g_bias, hg):
    has_init = s0 is not None
    kw, vw = hg * DK_D, hg * DV_D
    cbase = 2 * H_C * DK_C + 2 * H_C * DV_C
    qoff = cbase // kw
    koff = qoff + H_D // hg
    voff = (cbase + 2 * H_D * DK_D) // vw
    roff = voff + H_D // hg
    in_specs = [
        pl.BlockSpec((t, kw), lambda s, h: (base + s, qoff + h)),
        pl.BlockSpec((t, kw), lambda s, h: (base + s, koff + h)),
        pl.BlockSpec((t, vw), lambda s, h: (base + s, voff + h)),
        pl.BlockSpec((t, vw), lambda s, h: (base + s, roff + h)),
        pl.BlockSpec((t, LANE), lambda s, h: (base + s, 0)),
        pl.BlockSpec((2, hg, LANE, DK_D), lambda s, h: (0, h, 0, 0)),
        pl.BlockSpec((2, hg, 1, DK_D), lambda s, h: (0, h, 0, 0)),
    ]
    args = [z, z, z, z, zt, w_up, g_bias]
    if has_init:
        in_specs.append(pl.BlockSpec((None, 2, hg, DK_D, DV_D), lambda s, h: (s, 0, h, 0, 0)))
        args.append(s0)
    return pl.pallas_call(
        partial(_gla_kernel, t=t, hg=hg, has_init=has_init),
        grid=(nseq, H_D // hg),
        in_specs=in_specs,
        out_specs=[
            pl.BlockSpec((t, vw), lambda s, h: (s, h)),
            pl.BlockSpec((None, 2, hg, DK_D, DV_D), lambda s, h: (s, 0, h, 0, 0)),
        ],
        out_shape=[
            jax.ShapeDtypeStruct((nseq * t, H_D * DV_D), BF16),
            jax.ShapeDtypeStruct((nseq, 2, H_D, DK_D, DV_D), F32),
        ],
        scratch_shapes=[
            pltpu.VMEM((t, vw), F32),
            pltpu.VMEM((2, hg, DV_D, DK_D), F32),
            pltpu.VMEM((2, hg, t, DK_D), F32),
        ],
        compiler_params=_params(("arbitrary", "arbitrary")),
        name="gla",
    )(*args)


def _rope_tables(t):
    pos = jnp.arange(t)
    row = (pos // GRID_W).astype(F32)
    col = (pos % GRID_W).astype(F32)
    nf = DK_C // 4
    freqs = ROPE_BASE ** (-jnp.arange(nf, dtype=F32) / nf)
    ang = jnp.concatenate([row[:, None] * freqs, col[:, None] * freqs], axis=-1)
    cos, sin = jnp.cos(ang), jnp.sin(ang)
    return jnp.concatenate([cos, cos], axis=-1), jnp.concatenate([-sin, sin], axis=-1)


def _ab_tail(w_in, i_bias, f_bias):
    n_main = w_in.shape[1] - 4 * H_B
    idx = jnp.array([[h, H_B + h, 2 * H_B + h, 3 * H_B + h] for h in range(H_B)])
    cols = w_in[:, n_main:][:, idx]
    w_tail = jnp.pad(cols, ((0, 0), (0, 0), (0, LANE - 4))).reshape(D_MODEL, H_B * LANE)
    bias = jnp.concatenate([i_bias.reshape(-1), f_bias.reshape(-1)])[idx]
    bias_t = jnp.pad(bias, ((0, 0), (0, LANE - 4))).reshape(1, H_B * LANE)
    return n_main, w_tail, bias_t


def _cd_tail(w_in, w_up, g_bias):
    n_main = w_in.shape[1] - 2 * GLA_RANK
    w_tail = jnp.pad(w_in[:, n_main:], ((0, 0), (0, LANE - 2 * GLA_RANK)))
    up = w_up.reshape(2, GLA_RANK, H_D, DK_D).transpose(0, 2, 1, 3)
    up = jnp.stack([jnp.pad(up[d], ((0, 0), (d * GLA_RANK, LANE - (d + 1) * GLA_RANK), (0, 0))) for d in range(2)])
    return n_main, w_tail, up.astype(BF16), g_bias.reshape(2, H_D, 1, DK_D)


def kernel(x_prompt, x_sample, c, state_lru, state_mlstm_C, state_mlstm_n, state_mlstm_m, state_ret, state_gla, c_ctx, w_mod, b_mod, norm_w, ffn_w_gate, ffn_w_up, ffn_w_down, w_in_ab, w_out_ab, lru_conv_w, lru_conv_b, lru_w_a, lru_b_a, lru_w_i, lru_b_i, lru_lambda, mlstm_conv_w, mlstm_conv_b, mlstm_i_bias, mlstm_f_bias, w_in_cd, w_out_cd, ret_decay_log, gla_w_up, gla_b, final_norm_w):
    Bp, Tp, D = x_prompt.shape
    Bs, Ts, _ = x_sample.shape
    n_ctx = Bp * Tp
    tm_ffn, tf_ffn, tm_in, tm_out = 1024, 256, 1024, 512
    assert all(n_ctx % t == 0 and Ts % t == 0 for t in (tm_ffn, tm_in, tm_out)) and 1 + Bs <= MOD_ROWS and n_ctx % Ts == 0
    row_ffn, row_in, row_out = (_mod_row_map(n_ctx // t, Ts // t) for t in (tm_ffn, tm_in, tm_out))
    groups = ((0, Bp, Tp), (n_ctx // Ts, Bs, Ts))
    hg_ctx = {"mlstm": 4, "ret": 4, "gla": 4}
    hg_lat = {"mlstm": 1, "ret": 2, "gla": 2}

    cond = jnp.concatenate([c_ctx[None], c, jnp.zeros((MOD_ROWS - 1 - Bs, D), F32)], axis=0)
    mod = _modulation(cond, w_mod, b_mod).reshape(DEPTH, MOD_ROWS, N_MOD, D)
    nw = norm_w.reshape(DEPTH, 3, 1, D)
    wg, wu, wd = ffn_w_gate.astype(BF16), ffn_w_up.astype(BF16), ffn_w_down.astype(BF16)
    w_in_ab_b, w_in_cd_b = w_in_ab.astype(BF16), w_in_cd.astype(BF16)
    w_out_ab_b, w_out_cd_b = w_out_ab.astype(BF16), w_out_cd.astype(BF16)

    x = jnp.concatenate([x_prompt.reshape(n_ctx, D), x_sample.reshape(Bs * Ts, D)], axis=0)
    rope = _rope_tables(Ts)

    st_lru, st_C, st_n, st_m, st_ret, st_gla = [], [], [], [], [], []
    for l in range(DEPTH):
        x = _ffn(x, mod, nw, wg, wu, wd, l, 0, row_ffn, tm_ffn, tf_ffn)
        (pb, pn, pt), (sb, sn, st) = groups
        if l % 2 == 0:
            e = l // 2
            n_main, w_tail, bias_t = _ab_tail(w_in_ab[e], mlstm_i_bias[e], mlstm_f_bias[e])
            z, zt = _inproj(x, mod, nw, w_in_ab_b, l, e, n_main, w_tail.astype(BF16), row_in, tm_in)
            lru_args = (lru_conv_w[e], lru_conv_b[e], lru_w_a[e], lru_b_a[e], lru_w_i[e], lru_b_i[e], lru_lambda[e])
            a_p, s_lru = _lru(z, pb, pn, pt, None, *lru_args)
            a_s, _ = _lru(z, sb, sn, st, state_lru[:, e], *lru_args)
            b_p, s_C, s_n, s_m = _mlstm(z, zt, bias_t, pb, pn, pt, None, mlstm_conv_w[e], mlstm_conv_b[e], hg_ctx["mlstm"])
            b_s, _, _, _ = _mlstm(z, zt, bias_t, sb, sn, st,
                                  (state_mlstm_C[:, e], state_mlstm_n[:, e], state_mlstm_m[:, e]),
                                  mlstm_conv_w[e], mlstm_conv_b[e], hg_lat["mlstm"])
            st_lru.append(s_lru)
            st_C.append(s_C)
            st_n.append(s_n.reshape(Bp, 2, H_B, DK_B))
            st_m.append(s_m[:, :, :, 0, 0])
            mix_ctx, mix_lat = (a_p, b_p), (a_s, b_s)
            w_out = w_out_ab_b
        else:
            e = l // 2
            n_main, w_tail, w_up, g_bias = _cd_tail(w_in_cd[e], gla_w_up[e], gla_b[e])
            z, zt = _inproj(x, mod, nw, w_in_cd_b, l, e, n_main, w_tail.astype(BF16), row_in, tm_in)
            c_p, s_ret = _retention(z, pb, pn, pt, None, ret_decay_log[e], None, hg_ctx["ret"])
            c_s, _ = _retention(z, sb, sn, st, state_ret[:, e], ret_decay_log[e], rope, hg_lat["ret"])
            d_p, s_gla = _gla(z, zt, pb, pn, pt, None, w_up, g_bias, hg_ctx["gla"])
            d_s, _ = _gla(z, zt, sb, sn, st, state_gla[:, e], w_up, g_bias, hg_lat["gla"])
            st_ret.append(s_ret)
            st_gla.append(s_gla)
            mix_ctx, mix_lat = (c_p, d_p), (c_s, d_s)
            w_out = w_out_cd_b
        x = _outproj(x, mix_ctx, mix_lat, mod, w_out, l, e, row_out, tm_out)
        x = _ffn(x, mod, nw, wg, wu, wd, l, 1, row_ffn, tm_ffn, tf_ffn)

    y_prompt = _final_norm(x, final_norm_w, 0, n_ctx).reshape(Bp, Tp, D)
    y_sample = _final_norm(x, final_norm_w, n_ctx, Bs * Ts).reshape(Bs, Ts, D)
    return (y_prompt, y_sample,
            jnp.stack(st_lru, axis=1), jnp.stack(st_C, axis=1), jnp.stack(st_n, axis=1), jnp.stack(st_m, axis=1),
            jnp.stack(st_ret, axis=1), jnp.stack(st_gla, axis=1))
```

```python
from functools import partial

import jax
import jax.numpy as jnp
from jax import lax
from jax.experimental import pallas as pl
from jax.experimental.pallas import tpu as pltpu

F32 = jnp.float32
BF16 = jnp.bfloat16

D_MODEL = 2048
DEPTH = 4
N_MOD = 9
D_FF = 5632
EPS = 1e-6
CHUNK = 64
CONV_W = 4
D_A = D_MODEL // 2
NB_A = 8
BS_A = D_A // NB_A
LRU_C = 8.0
H_B = 4
DK_B = D_MODEL // 8
DV_B = D_MODEL // 8
H_C = 4
DK_C = D_MODEL // 16
DV_C = D_MODEL // 8
H_D = 4
DK_D = D_MODEL // 16
DV_D = D_MODEL // 8
GLA_RANK = 16
GLA_TAU = 16.0
ROPE_BASE = 10000.0
GRID_W = 64

MOD_ROWS = 8
LANE = 128
SUBLANE = 8
SUB = 16
MLSTM_CHUNK = 256
RET_CHUNK = 256
GLA_CHUNK = 128
VMEM_LIMIT = 56 * 1024 * 1024
FFN_VMEM_LIMIT = 60 * 1024 * 1024
NEG_INF = float("-inf")
LOG2E = 1.4426950408889634
HIGHEST = lax.Precision.HIGHEST
NT = (((1,), (1,)), ((), ()))
TN = (((0,), (0,)), ((), ()))


def _params(sem, vmem_limit=VMEM_LIMIT):
    return pltpu.CompilerParams(dimension_semantics=sem, vmem_limit_bytes=vmem_limit)


def _silu(x):
    return x * jax.nn.sigmoid(x)


def _sigmoid(x):
    return 0.5 * jnp.tanh(0.5 * x) + 0.5


def _rows(i, n):
    return pl.ds(pl.multiple_of(i * n, n), n)


def _mod_kernel(c_ref, w_ref, b_ref, o_ref):
    s = _silu(c_ref[...]).astype(BF16)
    o_ref[...] = jnp.dot(s, w_ref[...].astype(BF16), preferred_element_type=F32) + b_ref[...]


def _modulation(cond, w_mod, b_mod, tn=1024):
    n = w_mod.shape[-1]
    return pl.pallas_call(
        _mod_kernel,
        grid=(DEPTH, n // tn),
        in_specs=[
            pl.BlockSpec((MOD_ROWS, D_MODEL), lambda l, j: (0, 0)),
            pl.BlockSpec((None, D_MODEL, tn), lambda l, j: (l, 0, j)),
            pl.BlockSpec((None, 1, tn), lambda l, j: (l, 0, j)),
        ],
        out_specs=pl.BlockSpec((None, MOD_ROWS, tn), lambda l, j: (l, 0, j)),
        out_shape=jax.ShapeDtypeStruct((DEPTH, MOD_ROWS, n), F32),
        compiler_params=_params(("arbitrary", "arbitrary")),
        name="modulation",
    )(cond, w_mod, b_mod.reshape(DEPTH, 1, n))


def _norm_mod(x, nw, shift, scale):
    ms = jnp.mean(x * x, axis=-1, keepdims=True)
    y = x * lax.rsqrt(ms + EPS) * nw
    return y * (1.0 + scale) + shift


def _mod_row_map(n_ctx_tiles, tiles_per_req):
    def row(i):
        return jnp.where(i < n_ctx_tiles, 0, 1 + (i - n_ctx_tiles) // tiles_per_req)
    return row


def _ffn_kernel(x_ref, mod_ref, nw_ref, wg_ref, wu_ref, wd_ref, o_ref, h_scr, *, k, nf):
    f = pl.program_id(1)

    @pl.when(f == 0)
    def _():
        h = _norm_mod(x_ref[...], nw_ref[...], mod_ref[3 * k:3 * k + 1, :], mod_ref[3 * k + 1:3 * k + 2, :])
        h_scr[...] = h.astype(BF16)
        o_ref[...] = jnp.zeros_like(o_ref)

    h = h_scr[...]
    g = jnp.dot(h, wg_ref[...].astype(BF16), preferred_element_type=F32)
    u = jnp.dot(h, wu_ref[...].astype(BF16), preferred_element_type=F32)
    a = (_silu(g) * u).astype(BF16)
    o_ref[...] += jnp.dot(a, wd_ref[...].astype(BF16), preferred_element_type=F32)

    @pl.when(f == nf - 1)
    def _():
        o_ref[...] = x_ref[...] + (0.5 * mod_ref[3 * k + 2:3 * k + 3, :]) * o_ref[...]


def _ffn(x, mod, nw, wg, wu, wd, l, j, row_of_tile, tm, tf):
    t = x.shape[0]
    nf = D_FF // tf
    k = 2 * j
    return pl.pallas_call(
        partial(_ffn_kernel, k=k, nf=nf),
        grid=(t // tm, nf),
        in_specs=[
            pl.BlockSpec((tm, D_MODEL), lambda i, f: (i, 0)),
            pl.BlockSpec((None, None, N_MOD, D_MODEL), lambda i, f: (l, row_of_tile(i), 0, 0)),
            pl.BlockSpec((None, None, 1, D_MODEL), lambda i, f: (l, k, 0, 0)),
            pl.BlockSpec((None, None, D_MODEL, tf), lambda i, f: (l, j, 0, f)),
            pl.BlockSpec((None, None, D_MODEL, tf), lambda i, f: (l, j, 0, f)),
            pl.BlockSpec((None, None, tf, D_MODEL), lambda i, f: (l, j, f, 0)),
        ],
        out_specs=pl.BlockSpec((tm, D_MODEL), lambda i, f: (i, 0)),
        out_shape=jax.ShapeDtypeStruct((t, D_MODEL), F32),
        scratch_shapes=[pltpu.VMEM((tm, D_MODEL), BF16)],
        compiler_params=_params(("arbitrary", "arbitrary"), FFN_VMEM_LIMIT),
        name="ffn",
    )(x, mod, nw, wg, wu, wd)


def _inproj_kernel(x_ref, mod_ref, nw_ref, w_ref, wt_ref, z_ref, zt_ref, h_scr):
    j = pl.program_id(1)

    @pl.when(j == 0)
    def _():
        h = _norm_mod(x_ref[...], nw_ref[...], mod_ref[3:4, :], mod_ref[4:5, :])
        h_scr[...] = h.astype(BF16)
        zt_ref[...] = jnp.dot(h_scr[...], wt_ref[...], preferred_element_type=F32)

    z_ref[...] = jnp.dot(h_scr[...], w_ref[...], preferred_element_type=F32)


def _inproj(x, mod, nw, w_in, l, e, n, w_tail, row_of_tile, tm, tn=1024):
    t = x.shape[0]
    tw = w_tail.shape[1]
    return pl.pallas_call(
        _inproj_kernel,
        grid=(t // tm, n // tn),
        in_specs=[
            pl.BlockSpec((tm, D_MODEL), lambda i, j: (i, 0)),
            pl.BlockSpec((None, None, N_MOD, D_MODEL), lambda i, j: (l, row_of_tile(i), 0, 0)),
            pl.BlockSpec((None, None, 1, D_MODEL), lambda i, j: (l, 1, 0, 0)),
            pl.BlockSpec((None, D_MODEL, tn), lambda i, j: (e, 0, j)),
            pl.BlockSpec((D_MODEL, tw), lambda i, j: (0, 0)),
        ],
        out_specs=[
            pl.BlockSpec((tm, tn), lambda i, j: (i, j)),
            pl.BlockSpec((tm, tw), lambda i, j: (i, 0)),
        ],
        out_shape=[jax.ShapeDtypeStruct((t, n), F32), jax.ShapeDtypeStruct((t, tw), F32)],
        scratch_shapes=[pltpu.VMEM((tm, D_MODEL), BF16)],
        compiler_params=_params(("arbitrary", "arbitrary")),
        name="inproj",
    )(x, mod, nw, w_in, w_tail)


def _outproj_kernel(x_ref, ap_ref, bp_ref, as_ref, bs_ref, mod_ref, wa_ref, wb_ref, o_ref, *, n_ctx_tiles):
    def emit(ma_ref, mb_ref):
        y = jnp.dot(ma_ref[...], wa_ref[...], preferred_element_type=F32)
        y = y + jnp.dot(mb_ref[...], wb_ref[...], preferred_element_type=F32)
        o_ref[...] = x_ref[...] + mod_ref[5:6, :] * y

    is_ctx = pl.program_id(0) < n_ctx_tiles
    pl.when(is_ctx)(lambda: emit(ap_ref, bp_ref))
    pl.when(jnp.logical_not(is_ctx))(lambda: emit(as_ref, bs_ref))


def _outproj(x, mix_ctx, mix_lat, mod, w_out, l, e, row_of_tile, tm):
    t = x.shape[0]
    half = D_MODEL // 2
    n_ctx_tiles = mix_ctx[0].shape[0] // tm
    n_lat_tiles = mix_lat[0].shape[0] // tm
    ctx_blk = lambda i: (jnp.minimum(i, n_ctx_tiles - 1), 0)
    lat_blk = lambda i: (jnp.clip(i - n_ctx_tiles, 0, n_lat_tiles - 1), 0)
    return pl.pallas_call(
        partial(_outproj_kernel, n_ctx_tiles=n_ctx_tiles),
        grid=(t // tm,),
        in_specs=[
            pl.BlockSpec((tm, D_MODEL), lambda i: (i, 0)),
            pl.BlockSpec((tm, half), ctx_blk),
            pl.BlockSpec((tm, half), ctx_blk),
            pl.BlockSpec((tm, half), lat_blk),
            pl.BlockSpec((tm, half), lat_blk),
            pl.BlockSpec((None, None, N_MOD, D_MODEL), lambda i: (l, row_of_tile(i), 0, 0)),
            pl.BlockSpec((None, half, D_MODEL), lambda i: (e, 0, 0)),
            pl.BlockSpec((None, half, D_MODEL), lambda i: (e, 1, 0)),
        ],
        out_specs=pl.BlockSpec((tm, D_MODEL), lambda i: (i, 0)),
        out_shape=jax.ShapeDtypeStruct((t, D_MODEL), F32),
        compiler_params=_params(("arbitrary",)),
        name="outproj",
    )(x, *mix_ctx, *mix_lat, mod, w_out, w_out)


def _final_norm_kernel(x_ref, w_ref, o_ref):
    x = x_ref[...]
    ms = jnp.mean(x * x, axis=-1, keepdims=True)
    o_ref[...] = x * lax.rsqrt(ms + EPS) * w_ref[...]


def _final_norm(x, w, row0, rows, tm=512):
    base = row0 // tm
    return pl.pallas_call(
        _final_norm_kernel,
        grid=(rows // tm,),
        in_specs=[pl.BlockSpec((tm, D_MODEL), lambda i: (base + i, 0)), pl.BlockSpec((1, D_MODEL), lambda i: (0, 0))],
        out_specs=pl.BlockSpec((tm, D_MODEL), lambda i: (i, 0)),
        out_shape=jax.ShapeDtypeStruct((rows, D_MODEL), F32),
        compiler_params=_params(("arbitrary",)),
        name="final_norm",
    )(x, w.reshape(1, D_MODEL))


def _finalize_heads(acc_scr, gate_ref, mix_ref, t, hg, dv):
    tile = min(t, 256)

    def body(i, carry):
        r = _rows(i, tile)
        for u in range(hg):
            cv = slice(u * dv, (u + 1) * dv)
            o = acc_scr[r, cv]
            d = o - jnp.mean(o, axis=-1, keepdims=True)
            var = jnp.mean(d * d, axis=-1, keepdims=True)
            mix_ref[r, cv] = (d * lax.rsqrt(var + EPS) * _silu(gate_ref[r, cv])).astype(BF16)
        return carry

    lax.fori_loop(0, t // tile, body, 0)


def _dwconv_to(dst_ref, src_ref, pad_scr, w_ref, b_ref, t, post):
    c = src_ref.shape[1]
    pad_scr[pl.ds(0, SUBLANE), :] = jnp.zeros((SUBLANE, c), F32)
    pad_scr[pl.ds(t + SUBLANE, SUBLANE), :] = jnp.zeros((SUBLANE, c), F32)
    pad_scr[pl.ds(SUBLANE, t), :] = src_ref[...]
    tile = min(t, 256)
    for r0 in range(0, t, tile):
        y = b_ref[...]
        for k in range(CONV_W):
            y = y + w_ref[k:k + 1, :] * pad_scr[pl.ds(r0 + SUBLANE - CONV_W // 2 + k, tile), :]
        dst_ref[pl.ds(r0, tile), :] = post(y).astype(dst_ref.dtype)


def _tri(reverse, n):
    row = lax.broadcasted_iota(jnp.int32, (n, n), 0)
    col = lax.broadcasted_iota(jnp.int32, (n, n), 1)
    return jnp.where((col >= row) if reverse else (col <= row), 1.0, 0.0).astype(F32)


def _scan_block(a, b, rowid, reverse):
    for s in (1, 2, 4):
        if reverse:
            valid = rowid < SUBLANE - s
            shift = SUBLANE - s
        else:
            valid = rowid >= s
            shift = s
        a_sh = jnp.where(valid, pltpu.roll(a, shift, 0), 1.0)
        b_sh = jnp.where(valid, pltpu.roll(b, shift, 0), 0.0)
        b = b + a * b_sh
        a = a * a_sh
    return a, b


def _lru_kernel(*refs, t, cb, has_init):
    it = iter(refs)
    x_ref, y_ref, cw_ref, cbias_ref, wa_ref, ba_ref, wi_ref, bi_ref, lam_ref = (next(it) for _ in range(9))
    h0_ref = next(it) if has_init else None
    out_ref, sout_ref, pad_scr, xc_scr, a_scr, b_scr = (next(it) for _ in range(6))
    nb = cb // BS_A

    _dwconv_to(xc_scr, x_ref, pad_scr, cw_ref, cbias_ref, t, lambda v: v)

    lam = lam_ref[...]
    sp = jnp.maximum(-lam, 0.0) + jnp.log1p(jnp.exp(-jnp.abs(lam)))
    tile = min(t, 256)

    def gates(i, carry):
        r = _rows(i, tile)
        xc = xc_scr[r, :]
        xcb = xc.astype(BF16)
        for d in range(2):
            ra = jnp.concatenate([jnp.dot(xcb[:, n * BS_A:(n + 1) * BS_A], wa_ref[d, n], preferred_element_type=F32)
                                  for n in range(nb)], axis=1) + ba_ref[d:d + 1, :]
            ia = jnp.concatenate([jnp.dot(xcb[:, n * BS_A:(n + 1) * BS_A], wi_ref[d, n], preferred_element_type=F32)
                                  for n in range(nb)], axis=1) + bi_ref[d:d + 1, :]
            log_a = -LRU_C * _sigmoid(ra) * sp[d:d + 1, :]
            th = jnp.tanh(log_a)
            mult = jnp.sqrt(-2.0 * th / (1.0 - th))
            a_scr[d, r, :] = jnp.exp(log_a)
            b_scr[d, r, :] = mult * _sigmoid(ia) * xc
        return carry

    lax.fori_loop(0, t // tile, gates, 0)

    rowid = lax.broadcasted_iota(jnp.int32, (SUBLANE, cb), 0)
    nblk = t // SUBLANE

    def scan(i, carry):
        hf, hb = carry
        rf = _rows(i, SUBLANE)
        rb = _rows(nblk - 1 - i, SUBLANE)
        af, bf = _scan_block(a_scr[0, rf, :], b_scr[0, rf, :], rowid, False)
        ab, bb = _scan_block(a_scr[1, rb, :], b_scr[1, rb, :], rowid, True)
        hf_blk = bf + af * hf
        hb_blk = bb + ab * hb
        a_scr[0, rf, :] = hf_blk
        a_scr[1, rb, :] = hb_blk
        return hf_blk[SUBLANE - 1:SUBLANE, :], hb_blk[0:1, :]

    if has_init:
        init = (h0_ref[0:1, :], h0_ref[1:2, :])
    else:
        init = (jnp.zeros((1, cb), F32), jnp.zeros((1, cb), F32))
    hf, hb = lax.fori_loop(0, nblk, scan, init, unroll=2)
    sout_ref[0:1, :] = hf
    sout_ref[1:2, :] = hb

    def fin(i, carry):
        r = _rows(i, tile)
        out_ref[r, :] = ((a_scr[0, r, :] + a_scr[1, r, :]) * jax.nn.gelu(y_ref[r, :])).astype(BF16)
        return carry

    lax.fori_loop(0, t // tile, fin, 0)


def _lru(z, base, nseq, t, h0, conv_w, conv_b, w_a, b_a, w_i, b_i, lam, cb=512):
    nj = D_A // cb
    nb = cb // BS_A
    has_init = h0 is not None
    in_specs = [
        pl.BlockSpec((t, cb), lambda s, j: (base + s, j)),
        pl.BlockSpec((t, cb), lambda s, j: (base + s, nj + j)),
        pl.BlockSpec((CONV_W, cb), lambda s, j: (0, j)),
        pl.BlockSpec((1, cb), lambda s, j: (0, j)),
        pl.BlockSpec((2, nb, BS_A, BS_A), lambda s, j: (0, j, 0, 0)),
        pl.BlockSpec((2, cb), lambda s, j: (0, j)),
        pl.BlockSpec((2, nb, BS_A, BS_A), lambda s, j: (0, j, 0, 0)),
        pl.BlockSpec((2, cb), lambda s, j: (0, j)),
        pl.BlockSpec((2, cb), lambda s, j: (0, j)),
    ]
    args = [z, z, conv_w, conv_b.reshape(1, D_A), w_a.astype(BF16), b_a, w_i.astype(BF16), b_i, lam]
    if has_init:
        in_specs.append(pl.BlockSpec((None, 2, cb), lambda s, j: (s, 0, j)))
        args.append(h0)
    return pl.pallas_call(
        partial(_lru_kernel, t=t, cb=cb, has_init=has_init),
        grid=(nseq, nj),
        in_specs=in_specs,
        out_specs=[
            pl.BlockSpec((t, cb), lambda s, j: (s, j)),
            pl.BlockSpec((None, 2, cb), lambda s, j: (s, 0, j)),
        ],
        out_shape=[jax.ShapeDtypeStruct((nseq * t, D_A), BF16), jax.ShapeDtypeStruct((nseq, 2, D_A), F32)],
        scratch_shapes=[
            pltpu.VMEM((t + 2 * SUBLANE, cb), F32),
            pltpu.VMEM((t, cb), F32),
            pltpu.VMEM((2, t, cb), F32),
            pltpu.VMEM((2, t, cb), F32),
        ],
        compiler_params=_params(("arbitrary", "arbitrary")),
        name="rglru",
    )(*args)


def _mlstm_kernel(*refs, t, hg, has_init):
    it = iter(refs)
    q_ref, k_ref, v_ref, og_ref, zt_ref, bt_ref, cwq_ref, cbq_ref, cwk_ref, cbk_ref = (next(it) for _ in range(10))
    if has_init:
        c0_ref, n0_ref, m0_ref = next(it), next(it), next(it)
    mix_ref, cout_ref, nout_ref, mout_ref = (next(it) for _ in range(4))
    pad_scr, q_scr, k_scr, gate_scr, acc_scr, c_scr, n_scr = (next(it) for _ in range(7))
    CHUNK = MLSTM_CHUNK
    nchunk = t // CHUNK

    _dwconv_to(q_scr, q_ref, pad_scr, cwq_ref, cbq_ref, t, lambda v: _silu(v) * (DK_B ** -0.5))
    _dwconv_to(k_scr, k_ref, pad_scr, cwk_ref, cbk_ref, t, _silu)

    tile = min(t, 256)
    lane_t = lax.broadcasted_iota(jnp.int32, (tile, hg * LANE), 1) % LANE

    def gates(i, carry):
        r = _rows(i, tile)
        x = zt_ref[r, :] + bt_ref[...]
        logsig = jnp.minimum(x, 0.0) - jnp.log1p(jnp.exp(-jnp.abs(x)))
        gate_scr[r, :] = jnp.where(lane_t < 2, x, logsig)
        acc_scr[r, :] = jnp.zeros((tile, hg * DV_B), F32)
        return carry

    lax.fori_loop(0, t // tile, gates, 0)

    if has_init:
        c_scr[...] = c0_ref[...]
        n_scr[...] = n0_ref[...]
        m_init = tuple(m0_ref[d, u, :, 0:1] for u in range(hg) for d in range(2))
    else:
        c_scr[...] = jnp.zeros_like(c_scr)
        n_scr[...] = jnp.zeros_like(n_scr)
        m_init = tuple(jnp.zeros((1, 1), F32) for _ in range(2 * hg))

    row = lax.broadcasted_iota(jnp.int32, (CHUNK, CHUNK), 0)
    col = lax.broadcasted_iota(jnp.int32, (CHUNK, CHUNK), 1)
    lane_c = lax.broadcasted_iota(jnp.int32, (CHUNK, LANE), 1)

    def chunk(cidx, u, d, m):
        reverse = d == 1
        li, lf = d, 2 + d
        last = 0 if reverse else CHUNK - 1
        r = _rows(cidx, CHUNK)
        ck = slice(u * DK_B, (u + 1) * DK_B)
        cv = slice(u * DV_B, (u + 1) * DV_B)
        gt = gate_scr[r, u * LANE:(u + 1) * LANE]
        fsum = jnp.dot(_tri(reverse, CHUNK), gt, precision=HIGHEST, preferred_element_type=F32)
        y = jnp.where(lane_c == lf, fsum, gt)
        yt = y.T
        f_col, i_col = y[:, lf:lf + 1], y[:, li:li + 1]
        f_row, i_row = yt[lf:lf + 1, :], yt[li:li + 1, :]
        b = f_col + m
        mask = (col >= row) if reverse else (col <= row)
        dlog = jnp.where(mask, f_col - f_row + i_row, NEG_INF)
        m_t = jnp.maximum(b, jnp.max(dlog, axis=-1, keepdims=True))
        w = jnp.exp(dlog - m_t)
        inter = jnp.exp(b - m_t)
        q = q_scr[r, ck]
        k32 = k_scr[r, ck]
        v = v_ref[r, cv].astype(BF16)
        s = lax.dot_general(q, k32.astype(BF16), NT, preferred_element_type=F32) * w
        c_old = c_scr[d, u]
        n_old = n_scr[d, u]
        num = jnp.dot(s.astype(BF16), v, preferred_element_type=F32)
        num = num + inter * jnp.dot(q, c_old.astype(BF16), preferred_element_type=F32)
        den = jnp.sum(s, axis=-1, keepdims=True) + inter * jnp.sum(q.astype(F32) * n_old, axis=-1, keepdims=True)
        acc_scr[r, cv] += num / jnp.maximum(jnp.abs(den), jnp.exp(-m_t))
        m_new = m_t[last:last + 1, :]
        f_last = f_col[last:last + 1, :]
        kw = jnp.exp(f_last - f_col + i_col - m_new)
        decay = jnp.exp(f_last + m - m_new)
        kk = k32 * kw
        c_scr[d, u] = decay * c_old + lax.dot_general(kk.astype(BF16), v, TN, preferred_element_type=F32)
        n_scr[d, u] = decay * n_old + jnp.sum(kk, axis=0, keepdims=True)
        return m_new

    def body(c, carry):
        out = []
        for u in range(hg):
            out.append(chunk(c, u, 0, carry[2 * u]))
            out.append(chunk(nchunk - 1 - c, u, 1, carry[2 * u + 1]))
        return tuple(out)

    m_fin = lax.fori_loop(0, nchunk, body, m_init)

    _finalize_heads(acc_scr, og_ref, mix_ref, t, hg, DV_B)
    cout_ref[...] = c_scr[...]
    nout_ref[...] = n_scr[...]
    for u in range(hg):
        for d in range(2):
            mout_ref[d, u] = jnp.broadcast_to(m_fin[2 * u + d], (1, LANE))


def _mlstm(z, zt, bias_t, base, nseq, t, init, conv_w, conv_b, hg):
    has_init = init is not None
    kw, vw = hg * DK_B, hg * DV_B
    qoff = 2 * D_A // kw
    koff = qoff + H_B // hg
    voff = koff + H_B // hg
    goff = voff + H_B // hg
    cb2 = conv_b.reshape(1, 2 * H_B * DK_B)
    in_specs = [
        pl.BlockSpec((t, kw), lambda s, h: (base + s, qoff + h)),
        pl.BlockSpec((t, kw), lambda s, h: (base + s, koff + h)),
        pl.BlockSpec((t, vw), lambda s, h: (base + s, voff + h)),
        pl.BlockSpec((t, vw), lambda s, h: (base + s, goff + h)),
        pl.BlockSpec((t, hg * LANE), lambda s, h: (base + s, h)),
        pl.BlockSpec((1, hg * LANE), lambda s, h: (0, h)),
        pl.BlockSpec((CONV_W, kw), lambda s, h: (0, h)),
        pl.BlockSpec((1, kw), lambda s, h: (0, h)),
        pl.BlockSpec((CONV_W, kw), lambda s, h: (0, H_B // hg + h)),
        pl.BlockSpec((1, kw), lambda s, h: (0, H_B // hg + h)),
    ]
    args = [z, z, z, z, zt, bias_t, conv_w, cb2, conv_w, cb2]
    if has_init:
        c0, n0, m0 = init
        in_specs += [
            pl.BlockSpec((None, 2, hg, DK_B, DV_B), lambda s, h: (s, 0, h, 0, 0)),
            pl.BlockSpec((None, 2, hg, 1, DK_B), lambda s, h: (s, 0, h, 0, 0)),
            pl.BlockSpec((None, 2, hg, 1, LANE), lambda s, h: (s, 0, h, 0, 0)),
        ]
        args += [c0, n0.reshape(nseq, 2, H_B, 1, DK_B),
                 jnp.broadcast_to(m0[..., None, None], (nseq, 2, H_B, 1, LANE))]
    return pl.pallas_call(
        partial(_mlstm_kernel, t=t, hg=hg, has_init=has_init),
        grid=(nseq, H_B // hg),
        in_specs=in_specs,
        out_specs=[
            pl.BlockSpec((t, vw), lambda s, h: (s, h)),
            pl.BlockSpec((None, 2, hg, DK_B, DV_B), lambda s, h: (s, 0, h, 0, 0)),
            pl.BlockSpec((None, 2, hg, 1, DK_B), lambda s, h: (s, 0, h, 0, 0)),
            pl.BlockSpec((None, 2, hg, 1, LANE), lambda s, h: (s, 0, h, 0, 0)),
        ],
        out_shape=[
            jax.ShapeDtypeStruct((nseq * t, H_B * DV_B), BF16),
            jax.ShapeDtypeStruct((nseq, 2, H_B, DK_B, DV_B), F32),
            jax.ShapeDtypeStruct((nseq, 2, H_B, 1, DK_B), F32),
            jax.ShapeDtypeStruct((nseq, 2, H_B, 1, LANE), F32),
        ],
        scratch_shapes=[
            pltpu.VMEM((t + 2 * SUBLANE, kw), F32),
            pltpu.VMEM((t, kw), BF16),
            pltpu.VMEM((t, kw), F32),
            pltpu.VMEM((t, hg * LANE), F32),
            pltpu.VMEM((t, vw), F32),
            pltpu.VMEM((2, hg, DK_B, DV_B), F32),
            pltpu.VMEM((2, hg, 1, DK_B), F32),
        ],
        compiler_params=_params(("arbitrary", "arbitrary")),
        name="mlstm",
    )(*args)


def _ret_kernel(*refs, t, hg, use_rope, has_init):
    it = iter(refs)
    lg_ref, q_ref, k_ref, v_ref, g_ref = (next(it) for _ in range(5))
    if use_rope:
        cos_ref, sin_ref = next(it), next(it)
    s0_ref = next(it) if has_init else None
    mix_ref, sout_ref, acc_scr, st_scr = (next(it) for _ in range(4))
    CHUNK = RET_CHUNK
    nchunk = t // CHUNK
    h0 = pl.program_id(1) * hg

    row = lax.broadcasted_iota(jnp.int32, (CHUNK, CHUNK), 0)
    col = lax.broadcasted_iota(jnp.int32, (CHUNK, CHUNK), 1)
    dist = (row - col).astype(F32)
    pos = lax.broadcasted_iota(jnp.int32, (CHUNK, DK_C), 0).astype(F32)
    consts = {}
    for u in range(hg):
        for d in range(2):
            lg = lg_ref[d, h0 + u]
            if d == 0:
                dm = jnp.where(dist >= 0, jnp.exp(jnp.maximum(dist, 0.0) * lg), 0.0)
                qin = jnp.exp((pos + 1.0) * lg)
                kout = jnp.exp((CHUNK - 1.0 - pos) * lg)
            else:
                dm = jnp.where(dist <= 0, jnp.exp(jnp.maximum(-dist, 0.0) * lg), 0.0)
                qin = jnp.exp((CHUNK - pos) * lg)
                kout = jnp.exp(pos * lg)
            gch = jnp.exp(jnp.full((1, DV_C), float(CHUNK), F32) * lg)
            consts[u, d] = (dm, qin, kout, gch)

    acc_scr[...] = jnp.zeros_like(acc_scr)
    if has_init:
        st_scr[...] = s0_ref[...]
    else:
        st_scr[...] = jnp.zeros_like(st_scr)

    def chunk(cidx, u, d):
        dm, qin, kout, gch = consts[u, d]
        r = _rows(cidx, CHUNK)
        ck = slice(u * DK_C, (u + 1) * DK_C)
        cv = slice(u * DV_C, (u + 1) * DV_C)
        q = q_ref[r, ck]
        k = k_ref[r, ck] * (DK_C ** -0.5)
        if use_rope:
            cs, sn = cos_ref[r, :], sin_ref[r, :]
            q = q * cs + pltpu.roll(q, DK_C // 2, 1) * sn
            k = k * cs + pltpu.roll(k, DK_C // 2, 1) * sn
        v = v_ref[r, cv].astype(BF16)
        st = st_scr[d, u]
        s = lax.dot_general(q.astype(BF16), k.astype(BF16), NT, preferred_element_type=F32) * dm
        o = jnp.dot(s.astype(BF16), v, preferred_element_type=F32)
        o = o + jnp.dot((q * qin).astype(BF16), st.astype(BF16), preferred_element_type=F32)
        st_scr[d, u] = gch * st + lax.dot_general((k * kout).astype(BF16), v, TN, preferred_element_type=F32)
        acc_scr[r, cv] += o

    def body(c, carry):
        for u in range(hg):
            chunk(c, u, 0)
            chunk(nchunk - 1 - c, u, 1)
        return carry

    lax.fori_loop(0, nchunk, body, 0)
    _finalize_heads(acc_scr, g_ref, mix_ref, t, hg, DV_C)
    sout_ref[...] = st_scr[...]


def _retention(z, base, nseq, t, s0, lg, rope, hg):
    has_init = s0 is not None
    use_rope = rope is not None
    kw, vw = hg * DK_C, hg * DV_C
    qoff, koff = 0, H_C * DK_C // kw
    voff = 2 * H_C * DK_C // vw
    goff = voff + H_C // hg
    in_specs = [
        pl.BlockSpec(memory_space=pltpu.SMEM),
        pl.BlockSpec((t, kw), lambda s, h: (base + s, qoff + h)),
        pl.BlockSpec((t, kw), lambda s, h: (base + s, koff + h)),
        pl.BlockSpec((t, vw), lambda s, h: (base + s, voff + h)),
        pl.BlockSpec((t, vw), lambda s, h: (base + s, goff + h)),
    ]
    args = [lg, z, z, z, z]
    if use_rope:
        in_specs += [pl.BlockSpec((t, DK_C), lambda s, h: (0, 0)), pl.BlockSpec((t, DK_C), lambda s, h: (0, 0))]
        args += list(rope)
    if has_init:
        in_specs.append(pl.BlockSpec((None, 2, hg, DK_C, DV_C), lambda s, h: (s, 0, h, 0, 0)))
        args.append(s0)
    return pl.pallas_call(
        partial(_ret_kernel, t=t, hg=hg, use_rope=use_rope, has_init=has_init),
        grid=(nseq, H_C // hg),
        in_specs=in_specs,
        out_specs=[
            pl.BlockSpec((t, vw), lambda s, h: (s, h)),
            pl.BlockSpec((None, 2, hg, DK_C, DV_C), lambda s, h: (s, 0, h, 0, 0)),
        ],
        out_shape=[
            jax.ShapeDtypeStruct((nseq * t, H_C * DV_C), BF16),
            jax.ShapeDtypeStruct((nseq, 2, H_C, DK_C, DV_C), F32),
        ],
        scratch_shapes=[pltpu.VMEM((t, vw), F32), pltpu.VMEM((2, hg, DK_C, DV_C), F32)],
        compiler_params=_params(("arbitrary", "arbitrary")),
        name="retention",
    )(*args)


def _gla_intra(q, k, g2, reverse):
    CHUNK = GLA_CHUNK
    nsub = CHUNK // SUB
    lane = lax.broadcasted_iota(jnp.int32, (SUB, CHUNK), 1)
    rowi = lax.broadcasted_iota(jnp.int32, (SUB, CHUNK), 0)
    out = []
    for jb in range(nsub):
        lo, hi = jb * SUB, (jb + 1) * SUB
        qj, gj = q[lo:hi], g2[lo:hi]
        if reverse and jb < nsub - 1:
            gref = g2[hi:hi + 1]
            kt = (k[hi:] * jnp.exp2(gref - g2[hi:])).astype(BF16)
            kt = jnp.concatenate([jnp.zeros((hi, DK_D), BF16), kt], axis=0)
        elif not reverse and jb > 0:
            gref = g2[lo - 1:lo]
            kt = (k[:lo] * jnp.exp2(gref - g2[:lo])).astype(BF16)
            kt = jnp.concatenate([kt, jnp.zeros((CHUNK - lo, DK_D), BF16)], axis=0)
        else:
            kt = None
        if kt is None:
            off = jnp.zeros((SUB, CHUNK), F32)
        else:
            qt = (qj * jnp.exp2(gj - gref)).astype(BF16)
            off = lax.dot_general(qt, kt, NT, preferred_element_type=F32)
        diag = jnp.zeros((SUB, CHUNK), F32)
        for i in range(lo, hi):
            col = jnp.sum(qj * jnp.exp2(gj - g2[i:i + 1]) * k[i:i + 1], axis=-1, keepdims=True)
            diag = jnp.where(lane == i, col, diag)
        mask = (rowi + lo <= lane) if reverse else (rowi + lo >= lane)
        out.append(off + jnp.where(mask, diag, 0.0))
    return jnp.concatenate(out, axis=0)


def _gla_kernel(*refs, t, hg, has_init):
    it = iter(refs)
    q_ref, k_ref, v_ref, r_ref, a_ref, wup_ref, gb_ref = (next(it) for _ in range(7))
    s0_ref = next(it) if has_init else None
    mix_ref, sout_ref, acc_scr, st_scr, g_scr = (next(it) for _ in range(5))
    CHUNK = GLA_CHUNK
    nchunk = t // CHUNK
    tile = min(t, 256)

    def gates(i, carry):
        r = _rows(i, tile)
        a = a_ref[r, :].astype(BF16)
        for u in range(hg):
            for d in range(2):
                x = jnp.dot(a, wup_ref[d, u], preferred_element_type=F32) + gb_ref[d, u]
                g_scr[d, u, r, :] = (jnp.minimum(x, 0.0) - jnp.log1p(jnp.exp(-jnp.abs(x)))) * (1.0 / GLA_TAU)
        acc_scr[r, :] = jnp.zeros((tile, hg * DV_D), F32)
        return carry

    lax.fori_loop(0, t // tile, gates, 0)

    for u in range(hg):
        for d in range(2):
            st_scr[d, u] = s0_ref[d, u].T if has_init else jnp.zeros((DV_D, DK_D), F32)

    def chunk(cidx, u, d):
        reverse = d == 1
        last = 0 if reverse else CHUNK - 1
        r = _rows(cidx, CHUNK)
        ck = slice(u * DK_D, (u + 1) * DK_D)
        cv = slice(u * DV_D, (u + 1) * DV_D)
        g2 = jnp.dot(_tri(reverse, CHUNK), g_scr[d, u, r, :], precision=HIGHEST, preferred_element_type=F32) * LOG2E
        g2_last = g2[last:last + 1, :]
        q = q_ref[r, ck] * (DK_D ** -0.5)
        k = k_ref[r, ck]
        v = v_ref[r, cv].astype(BF16)
        st = st_scr[d, u]
        s = _gla_intra(q, k, g2, reverse)
        o = jnp.dot(s.astype(BF16), v, preferred_element_type=F32)
        o = o + lax.dot_general((q * jnp.exp2(g2)).astype(BF16), st.astype(BF16), NT, preferred_element_type=F32)
        kd = (k * jnp.exp2(g2_last - g2)).astype(BF16)
        st_scr[d, u] = jnp.exp2(g2_last) * st + lax.dot_general(v, kd, TN, preferred_element_type=F32)
        acc_scr[r, cv] += o

    def body(c, carry):
        for u in range(hg):
            chunk(c, u, 0)
            chunk(nchunk - 1 - c, u, 1)
        return carry

    lax.fori_loop(0, nchunk, body, 0)
    _finalize_heads(acc_scr, r_ref, mix_ref, t, hg, DV_D)
    for u in range(hg):
        for d in range(2):
            sout_ref[d, u] = st_scr[d, u].T


def _gla(z, zt, base, nseq, t, s0, w_up, g_bias, hg):
    has_init = s0 is not None
    kw, vw = hg * DK_D, hg * DV_D
    cbase = 2 * H_C * DK_C + 2 * H_C * DV_C
    qoff = cbase // kw
    koff = qoff + H_D // hg
    voff = (cbase + 2 * H_D * DK_D) // vw
    roff = voff + H_D // hg
    in_specs = [
        pl.BlockSpec((t, kw), lambda s, h: (base + s, qoff + h)),
        pl.BlockSpec((t, kw), lambda s, h: (base + s, koff + h)),
        pl.BlockSpec((t, vw), lambda s, h: (base + s, voff + h)),
        pl.BlockSpec((t, vw), lambda s, h: (base + s, roff + h)),
        pl.BlockSpec((t, LANE), lambda s, h: (base + s, 0)),
        pl.BlockSpec((2, hg, LANE, DK_D), lambda s, h: (0, h, 0, 0)),
        pl.BlockSpec((2, hg, 1, DK_D), lambda s, h: (0, h, 0, 0)),
    ]
    args = [z, z, z, z, zt, w_up, g_bias]
    if has_init:
        in_specs.append(pl.BlockSpec((None, 2, hg, DK_D, DV_D), lambda s, h: (s, 0, h, 0, 0)))
        args.append(s0)
    return pl.pallas_call(
        partial(_gla_kernel, t=t, hg=hg, has_init=has_init),
        grid=(nseq, H_D // hg),
        in_specs=in_specs,
        out_specs=[
            pl.BlockSpec((t, vw), lambda s, h: (s, h)),
            pl.BlockSpec((None, 2, hg, DK_D, DV_D), lambda s, h: (s, 0, h, 0, 0)),
        ],
        out_shape=[
            jax.ShapeDtypeStruct((nseq * t, H_D * DV_D), BF16),
            jax.ShapeDtypeStruct((nseq, 2, H_D, DK_D, DV_D), F32),
        ],
        scratch_shapes=[
            pltpu.VMEM((t, vw), F32),
            pltpu.VMEM((2, hg, DV_D, DK_D), F32),
            pltpu.VMEM((2, hg, t, DK_D), F32),
        ],
        compiler_params=_params(("arbitrary", "arbitrary")),
        name="gla",
    )(*args)


def _rope_tables(t):
    pos = jnp.arange(t)
    row = (pos // GRID_W).astype(F32)
    col = (pos % GRID_W).astype(F32)
    nf = DK_C // 4
    freqs = ROPE_BASE ** (-jnp.arange(nf, dtype=F32) / nf)
    ang = jnp.concatenate([row[:, None] * freqs, col[:, None] * freqs], axis=-1)
    cos, sin = jnp.cos(ang), jnp.sin(ang)
    return jnp.concatenate([cos, cos], axis=-1), jnp.concatenate([-sin, sin], axis=-1)


def _ab_tail(w_in, i_bias, f_bias):
    n_main = w_in.shape[1] - 4 * H_B
    gates = w_in[:, n_main:].reshape(D_MODEL, 4, H_B)
    cols = jnp.swapaxes(gates, 1, 2)
    w_tail = jnp.pad(cols, ((0, 0), (0, 0), (0, LANE - 4))).reshape(D_MODEL, H_B * LANE)
    bias = jnp.concatenate([i_bias, f_bias], axis=0).T
    bias_t = jnp.pad(bias, ((0, 0), (0, LANE - 4))).reshape(1, H_B * LANE)
    return n_main, w_tail, bias_t


def _cd_tail(w_in, w_up, g_bias):
    n_main = w_in.shape[1] - 2 * GLA_RANK
    w_tail = jnp.pad(w_in[:, n_main:], ((0, 0), (0, LANE - 2 * GLA_RANK)))
    up = w_up.reshape(2, GLA_RANK, H_D, DK_D).transpose(0, 2, 1, 3)
    up = jnp.stack([jnp.pad(up[d], ((0, 0), (d * GLA_RANK, LANE - (d + 1) * GLA_RANK), (0, 0))) for d in range(2)])
    return n_main, w_tail, up.astype(BF16), g_bias.reshape(2, H_D, 1, DK_D)


def kernel(x_prompt, x_sample, c, state_lru, state_mlstm_C, state_mlstm_n, state_mlstm_m, state_ret, state_gla, c_ctx, w_mod, b_mod, norm_w, ffn_w_gate, ffn_w_up, ffn_w_down, w_in_ab, w_out_ab, lru_conv_w, lru_conv_b, lru_w_a, lru_b_a, lru_w_i, lru_b_i, lru_lambda, mlstm_conv_w, mlstm_conv_b, mlstm_i_bias, mlstm_f_bias, w_in_cd, w_out_cd, ret_decay_log, gla_w_up, gla_b, final_norm_w):
    Bp, Tp, D = x_prompt.shape
    Bs, Ts, _ = x_sample.shape
    n_ctx = Bp * Tp
    tm_ffn, tf_ffn, tm_in, tm_out = 1024, 256, 1024, 512
    assert all(n_ctx % t == 0 and Ts % t == 0 for t in (tm_ffn, tm_in, tm_out)) and 1 + Bs <= MOD_ROWS and n_ctx % Ts == 0
    row_ffn, row_in, row_out = (_mod_row_map(n_ctx // t, Ts // t) for t in (tm_ffn, tm_in, tm_out))
    groups = ((0, Bp, Tp), (n_ctx // Ts, Bs, Ts))
    hg_ctx = {"mlstm": 4, "ret": 4, "gla": 4}
    hg_lat = {"mlstm": 1, "ret": 2, "gla": 2}

    cond = jnp.concatenate([c_ctx[None], c, jnp.zeros((MOD_ROWS - 1 - Bs, D), F32)], axis=0)
    mod = _modulation(cond, w_mod, b_mod).reshape(DEPTH, MOD_ROWS, N_MOD, D)
    nw = norm_w.reshape(DEPTH, 3, 1, D)
    wg, wu, wd = ffn_w_gate, ffn_w_up, ffn_w_down
    w_in_ab_b, w_in_cd_b = w_in_ab.astype(BF16), w_in_cd.astype(BF16)
    w_out_ab_b, w_out_cd_b = w_out_ab.astype(BF16), w_out_cd.astype(BF16)

    x = jnp.concatenate([x_prompt.reshape(n_ctx, D), x_sample.reshape(Bs * Ts, D)], axis=0)
    rope = _rope_tables(Ts)

    st_lru, st_C, st_n, st_m, st_ret, st_gla = [], [], [], [], [], []
    for l in range(DEPTH):
        x = _ffn(x, mod, nw, wg, wu, wd, l, 0, row_ffn, tm_ffn, tf_ffn)
        (pb, pn, pt), (sb, sn, st) = groups
        if l % 2 == 0:
            e = l // 2
            n_main, w_tail, bias_t = _ab_tail(w_in_ab[e], mlstm_i_bias[e], mlstm_f_bias[e])
            z, zt = _inproj(x, mod, nw, w_in_ab_b, l, e, n_main, w_tail.astype(BF16), row_in, tm_in)
            lru_args = (lru_conv_w[e], lru_conv_b[e], lru_w_a[e], lru_b_a[e], lru_w_i[e], lru_b_i[e], lru_lambda[e])
            a_p, s_lru = _lru(z, pb, pn, pt, None, *lru_args)
            a_s, _ = _lru(z, sb, sn, st, state_lru[:, e], *lru_args)
            b_p, s_C, s_n, s_m = _mlstm(z, zt, bias_t, pb, pn, pt, None, mlstm_conv_w[e], mlstm_conv_b[e], hg_ctx["mlstm"])
            b_s, _, _, _ = _mlstm(z, zt, bias_t, sb, sn, st,
                                  (state_mlstm_C[:, e], state_mlstm_n[:, e], state_mlstm_m[:, e]),
                                  mlstm_conv_w[e], mlstm_conv_b[e], hg_lat["mlstm"])
            st_lru.append(s_lru)
            st_C.append(s_C)
            st_n.append(s_n.reshape(Bp, 2, H_B, DK_B))
            st_m.append(s_m[:, :, :, 0, 0])
            mix_ctx, mix_lat = (a_p, b_p), (a_s, b_s)
            w_out = w_out_ab_b
        else:
            e = l // 2
            n_main, w_tail, w_up, g_bias = _cd_tail(w_in_cd[e], gla_w_up[e], gla_b[e])
            z, zt = _inproj(x, mod, nw, w_in_cd_b, l, e, n_main, w_tail.astype(BF16), row_in, tm_in)
            c_p, s_ret = _retention(z, pb, pn, pt, None, ret_decay_log[e], None, hg_ctx["ret"])
            c_s, _ = _retention(z, sb, sn, st, state_ret[:, e], ret_decay_log[e], rope, hg_lat["ret"])
            d_p, s_gla = _gla(z, zt, pb, pn, pt, None, w_up, g_bias, hg_ctx["gla"])
            d_s, _ = _gla(z, zt, sb, sn, st, state_gla[:, e], w_up, g_bias, hg_lat["gla"])
            st_ret.append(s_ret)
            st_gla.append(s_gla)
            mix_ctx, mix_lat = (c_p, d_p), (c_s, d_s)
            w_out = w_out_cd_b
        x = _outproj(x, mix_ctx, mix_lat, mod, w_out, l, e, row_out, tm_out)
        x = _ffn(x, mod, nw, wg, wu, wd, l, 1, row_ffn, tm_ffn, tf_ffn)

    y_prompt = _final_norm(x, final_norm_w, 0, n_ctx).reshape(Bp, Tp, D)
    y_sample = _final_norm(x, final_norm_w, n_ctx, Bs * Ts).reshape(Bs, Ts, D)
    return (y_prompt, y_sample,
            jnp.stack(st_lru, axis=1), jnp.stack(st_C, axis=1), jnp.stack(st_n, axis=1), jnp.stack(st_m, axis=1),
            jnp.stack(st_ret, axis=1), jnp.stack(st_gla, axis=1))
```

```python
from functools import partial

import jax
import jax.numpy as jnp
from jax import lax
from jax.experimental import pallas as pl
from jax.experimental.pallas import tpu as pltpu

F32 = jnp.float32
BF16 = jnp.bfloat16

D_MODEL = 2048
DEPTH = 4
N_MOD = 9
D_FF = 5632
EPS = 1e-6
CHUNK = 64
CONV_W = 4
D_A = D_MODEL // 2
NB_A = 8
BS_A = D_A // NB_A
LRU_C = 8.0
H_B = 4
DK_B = D_MODEL // 8
DV_B = D_MODEL // 8
H_C = 4
DK_C = D_MODEL // 16
DV_C = D_MODEL // 8
H_D = 4
DK_D = D_MODEL // 16
DV_D = D_MODEL // 8
GLA_RANK = 16
GLA_TAU = 16.0
ROPE_BASE = 10000.0
GRID_W = 64

MOD_ROWS = 8
LANE = 128
SUBLANE = 8
SUB = 16
MLSTM_CHUNK = 256
RET_CHUNK = 256
GLA_CHUNK = 128
VMEM_LIMIT = 56 * 1024 * 1024
FFN_VMEM_LIMIT = 60 * 1024 * 1024
NEG_INF = float("-inf")
LOG2E = 1.4426950408889634
HIGHEST = lax.Precision.HIGHEST
NT = (((1,), (1,)), ((), ()))
TN = (((0,), (0,)), ((), ()))


def _params(sem, vmem_limit=VMEM_LIMIT):
    return pltpu.CompilerParams(dimension_semantics=sem, vmem_limit_bytes=vmem_limit)


def _silu(x):
    return x * jax.nn.sigmoid(x)


def _sigmoid(x):
    return 0.5 * jnp.tanh(0.5 * x) + 0.5


def _rows(i, n):
    return pl.ds(pl.multiple_of(i * n, n), n)


def _mod_kernel(c_ref, w_ref, b_ref, o_ref):
    s = _silu(c_ref[...]).astype(BF16)
    o_ref[...] = jnp.dot(s, w_ref[...].astype(BF16), preferred_element_type=F32) + b_ref[...]


def _modulation(cond, w_mod, b_mod, tn=1024):
    n = w_mod.shape[-1]
    return pl.pallas_call(
        _mod_kernel,
        grid=(DEPTH, n // tn),
        in_specs=[
            pl.BlockSpec((MOD_ROWS, D_MODEL), lambda l, j: (0, 0)),
            pl.BlockSpec((None, D_MODEL, tn), lambda l, j: (l, 0, j)),
            pl.BlockSpec((None, 1, tn), lambda l, j: (l, 0, j)),
        ],
        out_specs=pl.BlockSpec((None, MOD_ROWS, tn), lambda l, j: (l, 0, j)),
        out_shape=jax.ShapeDtypeStruct((DEPTH, MOD_ROWS, n), F32),
        compiler_params=_params(("arbitrary", "arbitrary")),
        name="modulation",
    )(cond, w_mod, b_mod.reshape(DEPTH, 1, n))


def _norm_mod(x, nw, shift, scale):
    ms = jnp.mean(x * x, axis=-1, keepdims=True)
    y = x * lax.rsqrt(ms + EPS) * nw
    return y * (1.0 + scale) + shift


def _mod_row_map(n_ctx_tiles, tiles_per_req):
    def row(i):
        return jnp.where(i < n_ctx_tiles, 0, 1 + (i - n_ctx_tiles) // tiles_per_req)
    return row


def _ffn_kernel(x_ref, mod_ref, nw_ref, wg_ref, wu_ref, wd_ref, o_ref, h_scr, *, k, nf):
    f = pl.program_id(1)

    @pl.when(f == 0)
    def _():
        h = _norm_mod(x_ref[...], nw_ref[...], mod_ref[3 * k:3 * k + 1, :], mod_ref[3 * k + 1:3 * k + 2, :])
        h_scr[...] = h.astype(BF16)

    def down_proj():
        h = h_scr[...]
        g = jnp.dot(h, wg_ref[...].astype(BF16), preferred_element_type=F32)
        u = jnp.dot(h, wu_ref[...].astype(BF16), preferred_element_type=F32)
        a = (_silu(g) * u).astype(BF16)
        return jnp.dot(a, wd_ref[...].astype(BF16), preferred_element_type=F32)

    @pl.when(f == 0)
    def _():
        o_ref[...] = down_proj()

    @pl.when(jnp.logical_and(f > 0, f < nf - 1))
    def _():
        o_ref[...] += down_proj()

    @pl.when(f == nf - 1)
    def _():
        o_ref[...] = x_ref[...] + (0.5 * mod_ref[3 * k + 2:3 * k + 3, :]) * (o_ref[...] + down_proj())


def _ffn(x, mod, nw, wg, wu, wd, l, j, row_of_tile, tm, tf):
    t = x.shape[0]
    nf = D_FF // tf
    assert nf >= 2
    k = 2 * j
    return pl.pallas_call(
        partial(_ffn_kernel, k=k, nf=nf),
        grid=(t // tm, nf),
        in_specs=[
            pl.BlockSpec((tm, D_MODEL), lambda i, f: (i, 0)),
            pl.BlockSpec((None, None, N_MOD, D_MODEL), lambda i, f: (l, row_of_tile(i), 0, 0)),
            pl.BlockSpec((None, None, 1, D_MODEL), lambda i, f: (l, k, 0, 0)),
            pl.BlockSpec((None, None, D_MODEL, tf), lambda i, f: (l, j, 0, f)),
            pl.BlockSpec((None, None, D_MODEL, tf), lambda i, f: (l, j, 0, f)),
            pl.BlockSpec((None, None, tf, D_MODEL), lambda i, f: (l, j, f, 0)),
        ],
        out_specs=pl.BlockSpec((tm, D_MODEL), lambda i, f: (i, 0)),
        out_shape=jax.ShapeDtypeStruct((t, D_MODEL), F32),
        scratch_shapes=[pltpu.VMEM((tm, D_MODEL), BF16)],
        compiler_params=_params(("arbitrary", "arbitrary"), FFN_VMEM_LIMIT),
        name="ffn",
    )(x, mod, nw, wg, wu, wd)


def _inproj_kernel(x_ref, mod_ref, nw_ref, w_ref, wt_ref, z_ref, zt_ref, h_scr):
    j = pl.program_id(1)

    @pl.when(j == 0)
    def _():
        h = _norm_mod(x_ref[...], nw_ref[...], mod_ref[3:4, :], mod_ref[4:5, :]).astype(BF16)
        h_scr[...] = h
        zt_ref[...] = jnp.dot(h, wt_ref[...], preferred_element_type=F32)
        z_ref[...] = jnp.dot(h, w_ref[...], preferred_element_type=F32)

    @pl.when(j > 0)
    def _():
        z_ref[...] = jnp.dot(h_scr[...], w_ref[...], preferred_element_type=F32)


def _inproj(x, mod, nw, w_in, l, e, n, w_tail, row_of_tile, tm, tn=1024):
    t = x.shape[0]
    tw = w_tail.shape[1]
    return pl.pallas_call(
        _inproj_kernel,
        grid=(t // tm, n // tn),
        in_specs=[
            pl.BlockSpec((tm, D_MODEL), lambda i, j: (i, 0)),
            pl.BlockSpec((None, None, N_MOD, D_MODEL), lambda i, j: (l, row_of_tile(i), 0, 0)),
            pl.BlockSpec((None, None, 1, D_MODEL), lambda i, j: (l, 1, 0, 0)),
            pl.BlockSpec((None, D_MODEL, tn), lambda i, j: (e, 0, j)),
            pl.BlockSpec((D_MODEL, tw), lambda i, j: (0, 0)),
        ],
        out_specs=[
            pl.BlockSpec((tm, tn), lambda i, j: (i, j)),
            pl.BlockSpec((tm, tw), lambda i, j: (i, 0)),
        ],
        out_shape=[jax.ShapeDtypeStruct((t, n), F32), jax.ShapeDtypeStruct((t, tw), F32)],
        scratch_shapes=[pltpu.VMEM((tm, D_MODEL), BF16)],
        compiler_params=_params(("arbitrary", "arbitrary")),
        name="inproj",
    )(x, mod, nw, w_in, w_tail)


def _outproj_kernel(x_ref, ap_ref, bp_ref, as_ref, bs_ref, mod_ref, wa_ref, wb_ref, o_ref, *, n_ctx_tiles):
    def emit(ma_ref, mb_ref):
        y = jnp.dot(ma_ref[...], wa_ref[...], preferred_element_type=F32)
        y = y + jnp.dot(mb_ref[...], wb_ref[...], preferred_element_type=F32)
        o_ref[...] = x_ref[...] + mod_ref[5:6, :] * y

    is_ctx = pl.program_id(0) < n_ctx_tiles
    pl.when(is_ctx)(lambda: emit(ap_ref, bp_ref))
    pl.when(jnp.logical_not(is_ctx))(lambda: emit(as_ref, bs_ref))


def _outproj(x, mix_ctx, mix_lat, mod, w_out, l, e, row_of_tile, tm):
    t = x.shape[0]
    half = D_MODEL // 2
    n_ctx_tiles = mix_ctx[0].shape[0] // tm
    n_lat_tiles = mix_lat[0].shape[0] // tm
    ctx_blk = lambda i: (jnp.minimum(i, n_ctx_tiles - 1), 0)
    lat_blk = lambda i: (jnp.clip(i - n_ctx_tiles, 0, n_lat_tiles - 1), 0)
    return pl.pallas_call(
        partial(_outproj_kernel, n_ctx_tiles=n_ctx_tiles),
        grid=(t // tm,),
        in_specs=[
            pl.BlockSpec((tm, D_MODEL), lambda i: (i, 0)),
            pl.BlockSpec((tm, half), ctx_blk),
            pl.BlockSpec((tm, half), ctx_blk),
            pl.BlockSpec((tm, half), lat_blk),
            pl.BlockSpec((tm, half), lat_blk),
            pl.BlockSpec((None, None, N_MOD, D_MODEL), lambda i: (l, row_of_tile(i), 0, 0)),
            pl.BlockSpec((None, half, D_MODEL), lambda i: (e, 0, 0)),
            pl.BlockSpec((None, half, D_MODEL), lambda i: (e, 1, 0)),
        ],
        out_specs=pl.BlockSpec((tm, D_MODEL), lambda i: (i, 0)),
        out_shape=jax.ShapeDtypeStruct((t, D_MODEL), F32),
        compiler_params=_params(("arbitrary",)),
        name="outproj",
    )(x, *mix_ctx, *mix_lat, mod, w_out, w_out)


def _final_norm_kernel(x_ref, w_ref, o_ref):
    x = x_ref[...]
    ms = jnp.mean(x * x, axis=-1, keepdims=True)
    o_ref[...] = x * lax.rsqrt(ms + EPS) * w_ref[...]


def _final_norm(x, w, row0, rows, tm=512):
    base = row0 // tm
    return pl.pallas_call(
        _final_norm_kernel,
        grid=(rows // tm,),
        in_specs=[pl.BlockSpec((tm, D_MODEL), lambda i: (base + i, 0)), pl.BlockSpec((1, D_MODEL), lambda i: (0, 0))],
        out_specs=pl.BlockSpec((tm, D_MODEL), lambda i: (i, 0)),
        out_shape=jax.ShapeDtypeStruct((rows, D_MODEL), F32),
        compiler_params=_params(("arbitrary",)),
        name="final_norm",
    )(x, w.reshape(1, D_MODEL))


def _finalize_heads(acc_scr, gate_ref, mix_ref, t, hg, dv):
    tile = min(t, 256)

    def body(i, carry):
        r = _rows(i, tile)
        for u in range(hg):
            cv = slice(u * dv, (u + 1) * dv)
            o = acc_scr[r, cv]
            d = o - jnp.mean(o, axis=-1, keepdims=True)
            var = jnp.mean(d * d, axis=-1, keepdims=True)
            mix_ref[r, cv] = (d * lax.rsqrt(var + EPS) * _silu(gate_ref[r, cv])).astype(BF16)
        return carry

    lax.fori_loop(0, t // tile, body, 0)


def _dwconv_to(dst_ref, src_ref, pad_scr, w_ref, b_ref, t, post):
    c = src_ref.shape[1]
    pad_scr[pl.ds(0, SUBLANE), :] = jnp.zeros((SUBLANE, c), F32)
    pad_scr[pl.ds(t + SUBLANE, SUBLANE), :] = jnp.zeros((SUBLANE, c), F32)
    pad_scr[pl.ds(SUBLANE, t), :] = src_ref[...]
    tile = min(t, 256)
    for r0 in range(0, t, tile):
        y = b_ref[...]
        for k in range(CONV_W):
            y = y + w_ref[k:k + 1, :] * pad_scr[pl.ds(r0 + SUBLANE - CONV_W // 2 + k, tile), :]
        dst_ref[pl.ds(r0, tile), :] = post(y).astype(dst_ref.dtype)


def _tri(reverse, n):
    row = lax.broadcasted_iota(jnp.int32, (n, n), 0)
    col = lax.broadcasted_iota(jnp.int32, (n, n), 1)
    return jnp.where((col >= row) if reverse else (col <= row), 1.0, 0.0).astype(F32)


def _scan_block(a, b, rowid, reverse):
    for s in (1, 2, 4):
        if reverse:
            valid = rowid < SUBLANE - s
            shift = SUBLANE - s
        else:
            valid = rowid >= s
            shift = s
        a_sh = jnp.where(valid, pltpu.roll(a, shift, 0), 1.0)
        b_sh = jnp.where(valid, pltpu.roll(b, shift, 0), 0.0)
        b = b + a * b_sh
        a = a * a_sh
    return a, b


def _lru_kernel(*refs, t, cb, has_init):
    it = iter(refs)
    x_ref, y_ref, cw_ref, cbias_ref, wa_ref, ba_ref, wi_ref, bi_ref, lam_ref = (next(it) for _ in range(9))
    h0_ref = next(it) if has_init else None
    out_ref, sout_ref, pad_scr, xc_scr, a_scr, b_scr = (next(it) for _ in range(6))
    nb = cb // BS_A

    _dwconv_to(xc_scr, x_ref, pad_scr, cw_ref, cbias_ref, t, lambda v: v)

    lam = lam_ref[...]
    sp = jnp.maximum(-lam, 0.0) + jnp.log1p(jnp.exp(-jnp.abs(lam)))
    tile = min(t, 256)

    def gates(i, carry):
        r = _rows(i, tile)
        xc = xc_scr[r, :]
        xcb = xc.astype(BF16)
        for d in range(2):
            ra = jnp.concatenate([jnp.dot(xcb[:, n * BS_A:(n + 1) * BS_A], wa_ref[d, n], preferred_element_type=F32)
                                  for n in range(nb)], axis=1) + ba_ref[d:d + 1, :]
            ia = jnp.concatenate([jnp.dot(xcb[:, n * BS_A:(n + 1) * BS_A], wi_ref[d, n], preferred_element_type=F32)
                                  for n in range(nb)], axis=1) + bi_ref[d:d + 1, :]
            log_a = -LRU_C * _sigmoid(ra) * sp[d:d + 1, :]
            th = jnp.tanh(log_a)
            mult = jnp.sqrt(-2.0 * th / (1.0 - th))
            a_scr[d, r, :] = jnp.exp(log_a)
            b_scr[d, r, :] = mult * _sigmoid(ia) * xc
        return carry

    lax.fori_loop(0, t // tile, gates, 0)

    rowid = lax.broadcasted_iota(jnp.int32, (SUBLANE, cb), 0)
    nblk = t // SUBLANE

    def scan(i, carry):
        hf, hb = carry
        rf = _rows(i, SUBLANE)
        rb = _rows(nblk - 1 - i, SUBLANE)
        af, bf = _scan_block(a_scr[0, rf, :], b_scr[0, rf, :], rowid, False)
        ab, bb = _scan_block(a_scr[1, rb, :], b_scr[1, rb, :], rowid, True)
        hf_blk = bf + af * hf
        hb_blk = bb + ab * hb
        a_scr[0, rf, :] = hf_blk
        a_scr[1, rb, :] = hb_blk
        return hf_blk[SUBLANE - 1:SUBLANE, :], hb_blk[0:1, :]

    if has_init:
        init = (h0_ref[0:1, :], h0_ref[1:2, :])
    else:
        init = (jnp.zeros((1, cb), F32), jnp.zeros((1, cb), F32))
    hf, hb = lax.fori_loop(0, nblk, scan, init, unroll=2)
    sout_ref[0:1, :] = hf
    sout_ref[1:2, :] = hb

    def fin(i, carry):
        r = _rows(i, tile)
        out_ref[r, :] = ((a_scr[0, r, :] + a_scr[1, r, :]) * jax.nn.gelu(y_ref[r, :])).astype(BF16)
        return carry

    lax.fori_loop(0, t // tile, fin, 0)


def _lru(z, base, nseq, t, h0, conv_w, conv_b, w_a, b_a, w_i, b_i, lam, cb=512):
    nj = D_A // cb
    nb = cb // BS_A
    has_init = h0 is not None
    in_specs = [
        pl.BlockSpec((t, cb), lambda s, j: (base + s, j)),
        pl.BlockSpec((t, cb), lambda s, j: (base + s, nj + j)),
        pl.BlockSpec((CONV_W, cb), lambda s, j: (0, j)),
        pl.BlockSpec((1, cb), lambda s, j: (0, j)),
        pl.BlockSpec((2, nb, BS_A, BS_A), lambda s, j: (0, j, 0, 0)),
        pl.BlockSpec((2, cb), lambda s, j: (0, j)),
        pl.BlockSpec((2, nb, BS_A, BS_A), lambda s, j: (0, j, 0, 0)),
        pl.BlockSpec((2, cb), lambda s, j: (0, j)),
        pl.BlockSpec((2, cb), lambda s, j: (0, j)),
    ]
    args = [z, z, conv_w, conv_b.reshape(1, D_A), w_a.astype(BF16), b_a, w_i.astype(BF16), b_i, lam]
    if has_init:
        in_specs.append(pl.BlockSpec((None, 2, cb), lambda s, j: (s, 0, j)))
        args.append(h0)
    return pl.pallas_call(
        partial(_lru_kernel, t=t, cb=cb, has_init=has_init),
        grid=(nseq, nj),
        in_specs=in_specs,
        out_specs=[
            pl.BlockSpec((t, cb), lambda s, j: (s, j)),
            pl.BlockSpec((None, 2, cb), lambda s, j: (s, 0, j)),
        ],
        out_shape=[jax.ShapeDtypeStruct((nseq * t, D_A), BF16), jax.ShapeDtypeStruct((nseq, 2, D_A), F32)],
        scratch_shapes=[
            pltpu.VMEM((t + 2 * SUBLANE, cb), F32),
            pltpu.VMEM((t, cb), F32),
            pltpu.VMEM((2, t, cb), F32),
            pltpu.VMEM((2, t, cb), F32),
        ],
        compiler_params=_params(("arbitrary", "arbitrary")),
        name="rglru",
    )(*args)


def _mlstm_kernel(*refs, t, hg, has_init):
    it = iter(refs)
    q_ref, k_ref, v_ref, og_ref, zt_ref, bt_ref, cwq_ref, cbq_ref, cwk_ref, cbk_ref = (next(it) for _ in range(10))
    if has_init:
        c0_ref, n0_ref, m0_ref = next(it), next(it), next(it)
    mix_ref, cout_ref, nout_ref, mout_ref = (next(it) for _ in range(4))
    pad_scr, q_scr, k_scr, gate_scr, acc_scr, c_scr, n_scr = (next(it) for _ in range(7))
    CHUNK = MLSTM_CHUNK
    nchunk = t // CHUNK

    _dwconv_to(q_scr, q_ref, pad_scr, cwq_ref, cbq_ref, t, lambda v: _silu(v) * (DK_B ** -0.5))
    _dwconv_to(k_scr, k_ref, pad_scr, cwk_ref, cbk_ref, t, _silu)

    tile = min(t, 256)
    lane_t = lax.broadcasted_iota(jnp.int32, (tile, hg * LANE), 1) % LANE

    def gates(i, carry):
        r = _rows(i, tile)
        x = zt_ref[r, :] + bt_ref[...]
        logsig = jnp.minimum(x, 0.0) - jnp.log1p(jnp.exp(-jnp.abs(x)))
        gate_scr[r, :] = jnp.where(lane_t < 2, x, logsig)
        acc_scr[r, :] = jnp.zeros((tile, hg * DV_B), F32)
        return carry

    lax.fori_loop(0, t // tile, gates, 0)

    if has_init:
        c_scr[...] = c0_ref[...]
        n_scr[...] = n0_ref[...]
        m_init = tuple(m0_ref[d, u, :, 0:1] for u in range(hg) for d in range(2))
    else:
        c_scr[...] = jnp.zeros_like(c_scr)
        n_scr[...] = jnp.zeros_like(n_scr)
        m_init = tuple(jnp.zeros((1, 1), F32) for _ in range(2 * hg))

    row = lax.broadcasted_iota(jnp.int32, (CHUNK, CHUNK), 0)
    col = lax.broadcasted_iota(jnp.int32, (CHUNK, CHUNK), 1)
    lane_c = lax.broadcasted_iota(jnp.int32, (CHUNK, LANE), 1)

    def chunk(cidx, u, d, m):
        reverse = d == 1
        li, lf = d, 2 + d
        last = 0 if reverse else CHUNK - 1
        r = _rows(cidx, CHUNK)
        ck = slice(u * DK_B, (u + 1) * DK_B)
        cv = slice(u * DV_B, (u + 1) * DV_B)
        gt = gate_scr[r, u * LANE:(u + 1) * LANE]
        fsum = jnp.dot(_tri(reverse, CHUNK), gt, precision=HIGHEST, preferred_element_type=F32)
        y = jnp.where(lane_c == lf, fsum, gt)
        yt = y.T
        f_col, i_col = y[:, lf:lf + 1], y[:, li:li + 1]
        f_row, i_row = yt[lf:lf + 1, :], yt[li:li + 1, :]
        b = f_col + m
        mask = (col >= row) if reverse else (col <= row)
        dlog = jnp.where(mask, f_col - f_row + i_row, NEG_INF)
        m_t = jnp.maximum(b, jnp.max(dlog, axis=-1, keepdims=True))
        w = jnp.exp(dlog - m_t)
        inter = jnp.exp(b - m_t)
        q = q_scr[r, ck]
        k32 = k_scr[r, ck]
        v = v_ref[r, cv].astype(BF16)
        s = lax.dot_general(q, k32.astype(BF16), NT, preferred_element_type=F32) * w
        c_old = c_scr[d, u]
        n_old = n_scr[d, u]
        num = jnp.dot(s.astype(BF16), v, preferred_element_type=F32)
        num = num + inter * jnp.dot(q, c_old.astype(BF16), preferred_element_type=F32)
        den = jnp.sum(s, axis=-1, keepdims=True) + inter * jnp.sum(q.astype(F32) * n_old, axis=-1, keepdims=True)
        acc_scr[r, cv] += num / jnp.maximum(jnp.abs(den), jnp.exp(-m_t))
        m_new = m_t[last:last + 1, :]
        f_last = f_col[last:last + 1, :]
        kw = jnp.exp(f_last - f_col + i_col - m_new)
        decay = jnp.exp(f_last + m - m_new)
        kk = k32 * kw
        c_scr[d, u] = decay * c_old + lax.dot_general(kk.astype(BF16), v, TN, preferred_element_type=F32)
        n_scr[d, u] = decay * n_old + jnp.sum(kk, axis=0, keepdims=True)
        return m_new

    def body(c, carry):
        out = []
        for u in range(hg):
            out.append(chunk(c, u, 0, carry[2 * u]))
            out.append(chunk(nchunk - 1 - c, u, 1, carry[2 * u + 1]))
        return tuple(out)

    m_fin = lax.fori_loop(0, nchunk, body, m_init)

    _finalize_heads(acc_scr, og_ref, mix_ref, t, hg, DV_B)
    cout_ref[...] = c_scr[...]
    nout_ref[...] = n_scr[...]
    for u in range(hg):
        for d in range(2):
            mout_ref[d, u] = jnp.broadcast_to(m_fin[2 * u + d], (1, LANE))


def _mlstm(z, zt, bias_t, base, nseq, t, init, conv_w, conv_b, hg):
    has_init = init is not None
    kw, vw = hg * DK_B, hg * DV_B
    qoff = 2 * D_A // kw
    koff = qoff + H_B // hg
    voff = koff + H_B // hg
    goff = voff + H_B // hg
    cb2 = conv_b.reshape(1, 2 * H_B * DK_B)
    in_specs = [
        pl.BlockSpec((t, kw), lambda s, h: (base + s, qoff + h)),
        pl.BlockSpec((t, kw), lambda s, h: (base + s, koff + h)),
        pl.BlockSpec((t, vw), lambda s, h: (base + s, voff + h)),
        pl.BlockSpec((t, vw), lambda s, h: (base + s, goff + h)),
        pl.BlockSpec((t, hg * LANE), lambda s, h: (base + s, h)),
        pl.BlockSpec((1, hg * LANE), lambda s, h: (0, h)),
        pl.BlockSpec((CONV_W, kw), lambda s, h: (0, h)),
        pl.BlockSpec((1, kw), lambda s, h: (0, h)),
        pl.BlockSpec((CONV_W, kw), lambda s, h: (0, H_B // hg + h)),
        pl.BlockSpec((1, kw), lambda s, h: (0, H_B // hg + h)),
    ]
    args = [z, z, z, z, zt, bias_t, conv_w, cb2, conv_w, cb2]
    if has_init:
        c0, n0, m0 = init
        in_specs += [
            pl.BlockSpec((None, 2, hg, DK_B, DV_B), lambda s, h: (s, 0, h, 0, 0)),
            pl.BlockSpec((None, 2, hg, 1, DK_B), lambda s, h: (s, 0, h, 0, 0)),
            pl.BlockSpec((None, 2, hg, 1, LANE), lambda s, h: (s, 0, h, 0, 0)),
        ]
        args += [c0, n0.reshape(nseq, 2, H_B, 1, DK_B),
                 jnp.broadcast_to(m0[..., None, None], (nseq, 2, H_B, 1, LANE))]
    return pl.pallas_call(
        partial(_mlstm_kernel, t=t, hg=hg, has_init=has_init),
        grid=(nseq, H_B // hg),
        in_specs=in_specs,
        out_specs=[
            pl.BlockSpec((t, vw), lambda s, h: (s, h)),
            pl.BlockSpec((None, 2, hg, DK_B, DV_B), lambda s, h: (s, 0, h, 0, 0)),
            pl.BlockSpec((None, 2, hg, 1, DK_B), lambda s, h: (s, 0, h, 0, 0)),
            pl.BlockSpec((None, 2, hg, 1, LANE), lambda s, h: (s, 0, h, 0, 0)),
        ],
        out_shape=[
            jax.ShapeDtypeStruct((nseq * t, H_B * DV_B), BF16),
            jax.ShapeDtypeStruct((nseq, 2, H_B, DK_B, DV_B), F32),
            jax.ShapeDtypeStruct((nseq, 2, H_B, 1, DK_B), F32),
            jax.ShapeDtypeStruct((nseq, 2, H_B, 1, LANE), F32),
        ],
        scratch_shapes=[
            pltpu.VMEM((t + 2 * SUBLANE, kw), F32),
            pltpu.VMEM((t, kw), BF16),
            pltpu.VMEM((t, kw), F32),
            pltpu.VMEM((t, hg * LANE), F32),
            pltpu.VMEM((t, vw), F32),
            pltpu.VMEM((2, hg, DK_B, DV_B), F32),
            pltpu.VMEM((2, hg, 1, DK_B), F32),
        ],
        compiler_params=_params(("arbitrary", "arbitrary")),
        name="mlstm",
    )(*args)


def _ret_kernel(*refs, t, hg, use_rope, has_init):
    it = iter(refs)
    lg_ref, q_ref, k_ref, v_ref, g_ref = (next(it) for _ in range(5))
    if use_rope:
        cos_ref, sin_ref = next(it), next(it)
    s0_ref = next(it) if has_init else None
    mix_ref, sout_ref, acc_scr, st_scr = (next(it) for _ in range(4))
    CHUNK = RET_CHUNK
    nchunk = t // CHUNK
    h0 = pl.program_id(1) * hg

    row = lax.broadcasted_iota(jnp.int32, (CHUNK, CHUNK), 0)
    col = lax.broadcasted_iota(jnp.int32, (CHUNK, CHUNK), 1)
    dist = (row - col).astype(F32)
    pos = lax.broadcasted_iota(jnp.int32, (CHUNK, DK_C), 0).astype(F32)
    consts = {}
    for u in range(hg):
        for d in range(2):
            lg = lg_ref[d, h0 + u]
            if d == 0:
                dm = jnp.where(dist >= 0, jnp.exp(jnp.maximum(dist, 0.0) * lg), 0.0)
                qin = jnp.exp((pos + 1.0) * lg)
                kout = jnp.exp((CHUNK - 1.0 - pos) * lg)
            else:
                dm = jnp.where(dist <= 0, jnp.exp(jnp.maximum(-dist, 0.0) * lg), 0.0)
                qin = jnp.exp((CHUNK - pos) * lg)
                kout = jnp.exp(pos * lg)
            gch = jnp.exp(jnp.full((1, DV_C), float(CHUNK), F32) * lg)
            consts[u, d] = (dm, qin, kout, gch)

    acc_scr[...] = jnp.zeros_like(acc_scr)
    if has_init:
        st_scr[...] = s0_ref[...]
    else:
        st_scr[...] = jnp.zeros_like(st_scr)

    def chunk(cidx, u, d):
        dm, qin, kout, gch = consts[u, d]
        r = _rows(cidx, CHUNK)
        ck = slice(u * DK_C, (u + 1) * DK_C)
        cv = slice(u * DV_C, (u + 1) * DV_C)
        q = q_ref[r, ck]
        k = k_ref[r, ck] * (DK_C ** -0.5)
        if use_rope:
            cs, sn = cos_ref[r, :], sin_ref[r, :]
            q = q * cs + pltpu.roll(q, DK_C // 2, 1) * sn
            k = k * cs + pltpu.roll(k, DK_C // 2, 1) * sn
        v = v_ref[r, cv].astype(BF16)
        st = st_scr[d, u]
        s = lax.dot_general(q.astype(BF16), k.astype(BF16), NT, preferred_element_type=F32) * dm
        o = jnp.dot(s.astype(BF16), v, preferred_element_type=F32)
        o = o + jnp.dot((q * qin).astype(BF16), st.astype(BF16), preferred_element_type=F32)
        st_scr[d, u] = gch * st + lax.dot_general((k * kout).astype(BF16), v, TN, preferred_element_type=F32)
        acc_scr[r, cv] += o

    def body(c, carry):
        for u in range(hg):
            chunk(c, u, 0)
            chunk(nchunk - 1 - c, u, 1)
        return carry

    lax.fori_loop(0, nchunk, body, 0)
    _finalize_heads(acc_scr, g_ref, mix_ref, t, hg, DV_C)
    sout_ref[...] = st_scr[...]


def _retention(z, base, nseq, t, s0, lg, rope, hg):
    has_init = s0 is not None
    use_rope = rope is not None
    kw, vw = hg * DK_C, hg * DV_C
    qoff, koff = 0, H_C * DK_C // kw
    voff = 2 * H_C * DK_C // vw
    goff = voff + H_C // hg
    in_specs = [
        pl.BlockSpec(memory_space=pltpu.SMEM),
        pl.BlockSpec((t, kw), lambda s, h: (base + s, qoff + h)),
        pl.BlockSpec((t, kw), lambda s, h: (base + s, koff + h)),
        pl.BlockSpec((t, vw), lambda s, h: (base + s, voff + h)),
        pl.BlockSpec((t, vw), lambda s, h: (base + s, goff + h)),
    ]
    args = [lg, z, z, z, z]
    if use_rope:
        in_specs += [pl.BlockSpec((t, DK_C), lambda s, h: (0, 0)), pl.BlockSpec((t, DK_C), lambda s, h: (0, 0))]
        args += list(rope)
    if has_init:
        in_specs.append(pl.BlockSpec((None, 2, hg, DK_C, DV_C), lambda s, h: (s, 0, h, 0, 0)))
        args.append(s0)
    return pl.pallas_call(
        partial(_ret_kernel, t=t, hg=hg, use_rope=use_rope, has_init=has_init),
        grid=(nseq, H_C // hg),
        in_specs=in_specs,
        out_specs=[
            pl.BlockSpec((t, vw), lambda s, h: (s, h)),
            pl.BlockSpec((None, 2, hg, DK_C, DV_C), lambda s, h: (s, 0, h, 0, 0)),
        ],
        out_shape=[
            jax.ShapeDtypeStruct((nseq * t, H_C * DV_C), BF16),
            jax.ShapeDtypeStruct((nseq, 2, H_C, DK_C, DV_C), F32),
        ],
        scratch_shapes=[pltpu.VMEM((t, vw), F32), pltpu.VMEM((2, hg, DK_C, DV_C), F32)],
        compiler_params=_params(("arbitrary", "arbitrary")),
        name="retention",
    )(*args)


def _gla_intra(q, k, g2, reverse):
    CHUNK = GLA_CHUNK
    nsub = CHUNK // SUB
    lane = lax.broadcasted_iota(jnp.int32, (SUB, CHUNK), 1)
    rowi = lax.broadcasted_iota(jnp.int32, (SUB, CHUNK), 0)
    out = []
    for jb in range(nsub):
        lo, hi = jb * SUB, (jb + 1) * SUB
        qj, gj = q[lo:hi], g2[lo:hi]
        if reverse and jb < nsub - 1:
            gref = g2[hi:hi + 1]
            kt = (k[hi:] * jnp.exp2(gref - g2[hi:])).astype(BF16)
            kt = jnp.concatenate([jnp.zeros((hi, DK_D), BF16), kt], axis=0)
        elif not reverse and jb > 0:
            gref = g2[lo - 1:lo]
            kt = (k[:lo] * jnp.exp2(gref - g2[:lo])).astype(BF16)
            kt = jnp.concatenate([kt, jnp.zeros((CHUNK - lo, DK_D), BF16)], axis=0)
        else:
            kt = None
        if kt is None:
            off = jnp.zeros((SUB, CHUNK), F32)
        else:
            qt = (qj * jnp.exp2(gj - gref)).astype(BF16)
            off = lax.dot_general(qt, kt, NT, preferred_element_type=F32)
        diag = jnp.zeros((SUB, CHUNK), F32)
        for i in range(lo, hi):
            col = jnp.sum(qj * jnp.exp2(gj - g2[i:i + 1]) * k[i:i + 1], axis=-1, keepdims=True)
            diag = jnp.where(lane == i, col, diag)
        mask = (rowi + lo <= lane) if reverse else (rowi + lo >= lane)
        out.append(off + jnp.where(mask, diag, 0.0))
    return jnp.concatenate(out, axis=0)


def _gla_kernel(*refs, t, hg, has_init):
    it = iter(refs)
    q_ref, k_ref, v_ref, r_ref, a_ref, wup_ref, gb_ref = (next(it) for _ in range(7))
    s0_ref = next(it) if has_init else None
    mix_ref, sout_ref, acc_scr, st_scr, g_scr = (next(it) for _ in range(5))
    CHUNK = GLA_CHUNK
    nchunk = t // CHUNK
    tile = min(t, 256)

    def gates(i, carry):
        r = _rows(i, tile)
        a = a_ref[r, :].astype(BF16)
        for u in range(hg):
            for d in range(2):
                x = jnp.dot(a, wup_ref[d, u], preferred_element_type=F32) + gb_ref[d, u]
                g_scr[d, u, r, :] = (jnp.minimum(x, 0.0) - jnp.log1p(jnp.exp(-jnp.abs(x)))) * (1.0 / GLA_TAU)
        acc_scr[r, :] = jnp.zeros((tile, hg * DV_D), F32)
        return carry

    lax.fori_loop(0, t // tile, gates, 0)

    for u in range(hg):
        for d in range(2):
            st_scr[d, u] = s0_ref[d, u].T if has_init else jnp.zeros((DV_D, DK_D), F32)

    def chunk(cidx, u, d):
        reverse = d == 1
        last = 0 if reverse else CHUNK - 1
        r = _rows(cidx, CHUNK)
        ck = slice(u * DK_D, (u + 1) * DK_D)
        cv = slice(u * DV_D, (u + 1) * DV_D)
        g2 = jnp.dot(_tri(reverse, CHUNK), g_scr[d, u, r, :], precision=HIGHEST, preferred_element_type=F32) * LOG2E
        g2_last = g2[last:last + 1, :]
        q = q_ref[r, ck] * (DK_D ** -0.5)
        k = k_ref[r, ck]
        v = v_ref[r, cv].astype(BF16)
        st = st_scr[d, u]
        s = _gla_intra(q, k, g2, reverse)
        o = jnp.dot(s.astype(BF16), v, preferred_element_type=F32)
        o = o + lax.dot_general((q * jnp.exp2(g2)).astype(BF16), st.astype(BF16), NT, preferred_element_type=F32)
        kd = (k * jnp.exp2(g2_last - g2)).astype(BF16)
        st_scr[d, u] = jnp.exp2(g2_last) * st + lax.dot_general(v, kd, TN, preferred_element_type=F32)
        acc_scr[r, cv] += o

    def body(c, carry):
        for u in range(hg):
            chunk(c, u, 0)
            chunk(nchunk - 1 - c, u, 1)
        return carry

    lax.fori_loop(0, nchunk, body, 0)
    _finalize_heads(acc_scr, r_ref, mix_ref, t, hg, DV_D)
    for u in range(hg):
        for d in range(2):
            sout_ref[d, u] = st_scr[d, u].T


def _gla(z, zt, base, nseq, t, s0, w_up, g_bias, hg):
    has_init = s0 is not None
    kw, vw = hg * DK_D, hg * DV_D
    cbase = 2 * H_C * DK_C + 2 * H_C * DV_C
    qoff = cbase // kw
    koff = qoff + H_D // hg
    voff = (cbase + 2 * H_D * DK_D) // vw
    roff = voff + H_D // hg
    in_specs = [
        pl.BlockSpec((t, kw), lambda s, h: (base + s, qoff + h)),
        pl.BlockSpec((t, kw), lambda s, h: (base + s, koff + h)),
        pl.BlockSpec((t, vw), lambda s, h: (base + s, voff + h)),
        pl.BlockSpec((t, vw), lambda s, h: (base + s, roff + h)),
        pl.BlockSpec((t, LANE), lambda s, h: (base + s, 0)),
        pl.BlockSpec((2, hg, LANE, DK_D), lambda s, h: (0, h, 0, 0)),
        pl.BlockSpec((2, hg, 1, DK_D), lambda s, h: (0, h, 0, 0)),
    ]
    args = [z, z, z, z, zt, w_up, g_bias]
    if has_init:
        in_specs.append(pl.BlockSpec((None, 2, hg, DK_D, DV_D), lambda s, h: (s, 0, h, 0, 0)))
        args.append(s0)
    return pl.pallas_call(
        partial(_gla_kernel, t=t, hg=hg, has_init=has_init),
        grid=(nseq, H_D // hg),
        in_specs=in_specs,
        out_specs=[
            pl.BlockSpec((t, vw), lambda s, h: (s, h)),
            pl.BlockSpec((None, 2, hg, DK_D, DV_D), lambda s, h: (s, 0, h, 0, 0)),
        ],
        out_shape=[
            jax.ShapeDtypeStruct((nseq * t, H_D * DV_D), BF16),
            jax.ShapeDtypeStruct((nseq, 2, H_D, DK_D, DV_D), F32),
        ],
        scratch_shapes=[
            pltpu.VMEM((t, vw), F32),
            pltpu.VMEM((2, hg, DV_D, DK_D), F32),
            pltpu.VMEM((2, hg, t, DK_D), F32),
        ],
        compiler_params=_params(("arbitrary", "arbitrary")),
        name="gla",
    )(*args)


def _rope_tables(t):
    pos = jnp.arange(t)
    row = (pos // GRID_W).astype(F32)
    col = (pos % GRID_W).astype(F32)
    nf = DK_C // 4
    freqs = ROPE_BASE ** (-jnp.arange(nf, dtype=F32) / nf)
    ang = jnp.concatenate([row[:, None] * freqs, col[:, None] * freqs], axis=-1)
    cos, sin = jnp.cos(ang), jnp.sin(ang)
    return jnp.concatenate([cos, cos], axis=-1), jnp.concatenate([-sin, sin], axis=-1)


def _ab_tail(w_in, i_bias, f_bias):
    n_main = w_in.shape[1] - 4 * H_B
    gates = w_in[:, n_main:].reshape(D_MODEL, 4, H_B)
    cols = jnp.swapaxes(gates, 1, 2)
    w_tail = jnp.pad(cols, ((0, 0), (0, 0), (0, LANE - 4))).reshape(D_MODEL, H_B * LANE)
    bias = jnp.concatenate([i_bias, f_bias], axis=0).T
    bias_t = jnp.pad(bias, ((0, 0), (0, LANE - 4))).reshape(1, H_B * LANE)
    return n_main, w_tail, bias_t


def _cd_tail(w_in, w_up, g_bias):
    n_main = w_in.shape[1] - 2 * GLA_RANK
    w_tail = jnp.pad(w_in[:, n_main:], ((0, 0), (0, LANE - 2 * GLA_RANK)))
    up = w_up.reshape(2, GLA_RANK, H_D, DK_D).transpose(0, 2, 1, 3)
    up = jnp.stack([jnp.pad(up[d], ((0, 0), (d * GLA_RANK, LANE - (d + 1) * GLA_RANK), (0, 0))) for d in range(2)])
    return n_main, w_tail, up.astype(BF16), g_bias.reshape(2, H_D, 1, DK_D)


def kernel(x_prompt, x_sample, c, state_lru, state_mlstm_C, state_mlstm_n, state_mlstm_m, state_ret, state_gla, c_ctx, w_mod, b_mod, norm_w, ffn_w_gate, ffn_w_up, ffn_w_down, w_in_ab, w_out_ab, lru_conv_w, lru_conv_b, lru_w_a, lru_b_a, lru_w_i, lru_b_i, lru_lambda, mlstm_conv_w, mlstm_conv_b, mlstm_i_bias, mlstm_f_bias, w_in_cd, w_out_cd, ret_decay_log, gla_w_up, gla_b, final_norm_w):
    Bp, Tp, D = x_prompt.shape
    Bs, Ts, _ = x_sample.shape
    n_ctx = Bp * Tp
    tm_ffn, tf_ffn, tm_in, tm_out = 1024, 256, 1024, 512
    assert all(n_ctx % t == 0 and Ts % t == 0 for t in (tm_ffn, tm_in, tm_out)) and 1 + Bs <= MOD_ROWS and n_ctx % Ts == 0
    row_ffn, row_in, row_out = (_mod_row_map(n_ctx // t, Ts // t) for t in (tm_ffn, tm_in, tm_out))
    groups = ((0, Bp, Tp), (n_ctx // Ts, Bs, Ts))
    hg_ctx = {"mlstm": 4, "ret": 4, "gla": 4}
    hg_lat = {"mlstm": 1, "ret": 2, "gla": 2}

    cond = jnp.concatenate([c_ctx[None], c, jnp.zeros((MOD_ROWS - 1 - Bs, D), F32)], axis=0)
    mod = _modulation(cond, w_mod, b_mod).reshape(DEPTH, MOD_ROWS, N_MOD, D)
    nw = norm_w.reshape(DEPTH, 3, 1, D)
    wg, wu, wd = ffn_w_gate, ffn_w_up, ffn_w_down
    w_in_ab_b, w_in_cd_b = w_in_ab.astype(BF16), w_in_cd.astype(BF16)
    w_out_ab_b, w_out_cd_b = w_out_ab.astype(BF16), w_out_cd.astype(BF16)

    x = jnp.concatenate([x_prompt.reshape(n_ctx, D), x_sample.reshape(Bs * Ts, D)], axis=0)
    rope = _rope_tables(Ts)

    st_lru, st_C, st_n, st_m, st_ret, st_gla = [], [], [], [], [], []
    for l in range(DEPTH):
        x = _ffn(x, mod, nw, wg, wu, wd, l, 0, row_ffn, tm_ffn, tf_ffn)
        (pb, pn, pt), (sb, sn, st) = groups
        if l % 2 == 0:
            e = l // 2
            n_main, w_tail, bias_t = _ab_tail(w_in_ab[e], mlstm_i_bias[e], mlstm_f_bias[e])
            z, zt = _inproj(x, mod, nw, w_in_ab_b, l, e, n_main, w_tail.astype(BF16), row_in, tm_in)
            lru_args = (lru_conv_w[e], lru_conv_b[e], lru_w_a[e], lru_b_a[e], lru_w_i[e], lru_b_i[e], lru_lambda[e])
            a_p, s_lru = _lru(z, pb, pn, pt, None, *lru_args)
            a_s, _ = _lru(z, sb, sn, st, state_lru[:, e], *lru_args)
            b_p, s_C, s_n, s_m = _mlstm(z, zt, bias_t, pb, pn, pt, None, mlstm_conv_w[e], mlstm_conv_b[e], hg_ctx["mlstm"])
            b_s, _, _, _ = _mlstm(z, zt, bias_t, sb, sn, st,
                                  (state_mlstm_C[:, e], state_mlstm_n[:, e], state_mlstm_m[:, e]),
                                  mlstm_conv_w[e], mlstm_conv_b[e], hg_lat["mlstm"])
            st_lru.append(s_lru)
            st_C.append(s_C)
            st_n.append(s_n.reshape(Bp, 2, H_B, DK_B))
            st_m.append(s_m[:, :, :, 0, 0])
            mix_ctx, mix_lat = (a_p, b_p), (a_s, b_s)
            w_out = w_out_ab_b
        else:
            e = l // 2
            n_main, w_tail, w_up, g_bias = _cd_tail(w_in_cd[e], gla_w_up[e], gla_b[e])
            z, zt = _inproj(x, mod, nw, w_in_cd_b, l, e, n_main, w_tail.astype(BF16), row_in, tm_in)
            c_p, s_ret = _retention(z, pb, pn, pt, None, ret_decay_log[e], None, hg_ctx["ret"])
            c_s, _ = _retention(z, sb, sn, st, state_ret[:, e], ret_decay_log[e], rope, hg_lat["ret"])
            d_p, s_gla = _gla(z, zt, pb, pn, pt, None, w_up, g_bias, hg_ctx["gla"])
            d_s, _ = _gla(z, zt, sb, sn, st, state_gla[:, e], w_up, g_bias, hg_lat["gla"])
            st_ret.append(s_ret)
            st_gla.append(s_gla)
            mix_ctx, mix_lat = (c_p, d_p), (c_s, d_s)
            w_out = w_out_cd_b
        x = _outproj(x, mix_ctx, mix_lat, mod, w_out, l, e, row_out, tm_out)
        x = _ffn(x, mod, nw, wg, wu, wd, l, 1, row_ffn, tm_ffn, tf_ffn)

    y_prompt = _final_norm(x, final_norm_w, 0, n_ctx).reshape(Bp, Tp, D)
    y_sample = _final_norm(x, final_norm_w, n_ctx, Bs * Ts).reshape(Bs, Ts, D)
    return (y_prompt, y_sample,
            jnp.stack(st_lru, axis=1), jnp.stack(st_C, axis=1), jnp.stack(st_n, axis=1), jnp.stack(st_m, axis=1),
            jnp.stack(st_ret, axis=1), jnp.stack(st_gla, axis=1))
```

```python
from functools import partial

import jax
import jax.numpy as jnp
from jax import lax
from jax.experimental import pallas as pl
from jax.experimental.pallas import tpu as pltpu

F32 = jnp.float32
BF16 = jnp.bfloat16

D_MODEL = 2048
DEPTH = 4
N_MOD = 9
N_EVEN = (DEPTH + 1) // 2
N_ODD = DEPTH // 2
D_FF = 5632
EPS = 1e-6
CHUNK = 64
CONV_W = 4
D_A = D_MODEL // 2
NB_A = 8
BS_A = D_A // NB_A
LRU_C = 8.0
H_B = 4
DK_B = D_MODEL // 8
DV_B = D_MODEL // 8
H_C = 4
DK_C = D_MODEL // 16
DV_C = D_MODEL // 8
H_D = 4
DK_D = D_MODEL // 16
DV_D = D_MODEL // 8
GLA_RANK = 16
GLA_TAU = 16.0
ROPE_BASE = 10000.0
GRID_W = 64

MOD_ROWS = 8
LANE = 128
SUBLANE = 8
SUB = 16
MLSTM_CHUNK = 256
RET_CHUNK = 256
GLA_CHUNK = 128
VMEM_LIMIT = 56 * 1024 * 1024
FFN_VMEM_LIMIT = 60 * 1024 * 1024
NEG_INF = float("-inf")
LOG2E = 1.4426950408889634
HIGHEST = lax.Precision.HIGHEST
NT = (((1,), (1,)), ((), ()))
TN = (((0,), (0,)), ((), ()))


def _params(sem, vmem_limit=VMEM_LIMIT):
    return pltpu.CompilerParams(dimension_semantics=sem, vmem_limit_bytes=vmem_limit)


def _silu(x):
    return x * jax.nn.sigmoid(x)


def _sigmoid(x):
    return 0.5 * jnp.tanh(0.5 * x) + 0.5


def _rows(i, n):
    return pl.ds(pl.multiple_of(i * n, n), n)


def _mod_kernel(c_ref, w_ref, b_ref, o_ref):
    s = _silu(c_ref[...]).astype(BF16)
    o_ref[...] = jnp.dot(s, w_ref[...].astype(BF16), preferred_element_type=F32) + b_ref[...]


def _modulation(cond, w_mod, b_mod, tn=1024):
    n = w_mod.shape[-1]
    return pl.pallas_call(
        _mod_kernel,
        grid=(DEPTH, n // tn),
        in_specs=[
            pl.BlockSpec((MOD_ROWS, D_MODEL), lambda l, j: (0, 0)),
            pl.BlockSpec((None, D_MODEL, tn), lambda l, j: (l, 0, j)),
            pl.BlockSpec((None, 1, tn), lambda l, j: (l, 0, j)),
        ],
        out_specs=pl.BlockSpec((None, MOD_ROWS, tn), lambda l, j: (l, 0, j)),
        out_shape=jax.ShapeDtypeStruct((DEPTH, MOD_ROWS, n), F32),
        compiler_params=_params(("arbitrary", "arbitrary")),
        name="modulation",
    )(cond, w_mod, b_mod.reshape(DEPTH, 1, n))


def _norm_mod(x, nw, shift, scale):
    ms = jnp.mean(x * x, axis=-1, keepdims=True)
    y = x * lax.rsqrt(ms + EPS) * nw
    return y * (1.0 + scale) + shift


def _mod_row_map(n_ctx_tiles, tiles_per_req):
    def row(i):
        return jnp.where(i < n_ctx_tiles, 0, 1 + (i - n_ctx_tiles) // tiles_per_req)
    return row


def _ffn_kernel(x_ref, mod_ref, nw_ref, wg_ref, wu_ref, wd_ref, o_ref, h_scr, *, k, nf):
    f = pl.program_id(1)

    @pl.when(f == 0)
    def _():
        h = _norm_mod(x_ref[...], nw_ref[...], mod_ref[3 * k:3 * k + 1, :], mod_ref[3 * k + 1:3 * k + 2, :])
        h_scr[...] = h.astype(BF16)

    def down_proj():
        h = h_scr[...]
        g = jnp.dot(h, wg_ref[...].astype(BF16), preferred_element_type=F32)
        u = jnp.dot(h, wu_ref[...].astype(BF16), preferred_element_type=F32)
        a = (_silu(g) * u).astype(BF16)
        return jnp.dot(a, wd_ref[...].astype(BF16), preferred_element_type=F32)

    @pl.when(f == 0)
    def _():
        o_ref[...] = down_proj()

    @pl.when(jnp.logical_and(f > 0, f < nf - 1))
    def _():
        o_ref[...] += down_proj()

    @pl.when(f == nf - 1)
    def _():
        o_ref[...] = x_ref[...] + (0.5 * mod_ref[3 * k + 2:3 * k + 3, :]) * (o_ref[...] + down_proj())


def _ffn(x, mod, nw, wg, wu, wd, l, j, row_of_tile, tm, tf):
    t = x.shape[0]
    nf = D_FF // tf
    assert nf >= 2
    k = 2 * j
    return pl.pallas_call(
        partial(_ffn_kernel, k=k, nf=nf),
        grid=(t // tm, nf),
        in_specs=[
            pl.BlockSpec((tm, D_MODEL), lambda i, f: (i, 0)),
            pl.BlockSpec((None, None, N_MOD, D_MODEL), lambda i, f: (l, row_of_tile(i), 0, 0)),
            pl.BlockSpec((None, None, 1, D_MODEL), lambda i, f: (l, k, 0, 0)),
            pl.BlockSpec((None, None, D_MODEL, tf), lambda i, f: (l, j, 0, f)),
            pl.BlockSpec((None, None, D_MODEL, tf), lambda i, f: (l, j, 0, f)),
            pl.BlockSpec((None, None, tf, D_MODEL), lambda i, f: (l, j, f, 0)),
        ],
        out_specs=pl.BlockSpec((tm, D_MODEL), lambda i, f: (i, 0)),
        out_shape=jax.ShapeDtypeStruct((t, D_MODEL), F32),
        scratch_shapes=[pltpu.VMEM((tm, D_MODEL), BF16)],
        compiler_params=_params(("arbitrary", "arbitrary"), FFN_VMEM_LIMIT),
        name="ffn",
    )(x, mod, nw, wg, wu, wd)


def _inproj_kernel(x_ref, mod_ref, nw_ref, w_ref, wt_ref, z_ref, zt_ref, h_scr):
    j = pl.program_id(1)

    @pl.when(j == 0)
    def _():
        h = _norm_mod(x_ref[...], nw_ref[...], mod_ref[3:4, :], mod_ref[4:5, :]).astype(BF16)
        h_scr[...] = h
        zt_ref[...] = jnp.dot(h, wt_ref[...], preferred_element_type=F32)
        z_ref[...] = jnp.dot(h, w_ref[...], preferred_element_type=F32)

    @pl.when(j > 0)
    def _():
        z_ref[...] = jnp.dot(h_scr[...], w_ref[...], preferred_element_type=F32)


def _inproj(x, mod, nw, w_in, l, e, n, w_tail, row_of_tile, tm, tn=1024):
    t = x.shape[0]
    tw = w_tail.shape[1]
    return pl.pallas_call(
        _inproj_kernel,
        grid=(t // tm, n // tn),
        in_specs=[
            pl.BlockSpec((tm, D_MODEL), lambda i, j: (i, 0)),
            pl.BlockSpec((None, None, N_MOD, D_MODEL), lambda i, j: (l, row_of_tile(i), 0, 0)),
            pl.BlockSpec((None, None, 1, D_MODEL), lambda i, j: (l, 1, 0, 0)),
            pl.BlockSpec((None, D_MODEL, tn), lambda i, j: (e, 0, j)),
            pl.BlockSpec((D_MODEL, tw), lambda i, j: (0, 0)),
        ],
        out_specs=[
            pl.BlockSpec((tm, tn), lambda i, j: (i, j)),
            pl.BlockSpec((tm, tw), lambda i, j: (i, 0)),
        ],
        out_shape=[jax.ShapeDtypeStruct((t, n), F32), jax.ShapeDtypeStruct((t, tw), F32)],
        scratch_shapes=[pltpu.VMEM((tm, D_MODEL), BF16)],
        compiler_params=_params(("arbitrary", "arbitrary")),
        name="inproj",
    )(x, mod, nw, w_in, w_tail)


def _outproj_kernel(x_ref, ap_ref, bp_ref, as_ref, bs_ref, mod_ref, wa_ref, wb_ref, o_ref, *, n_ctx_tiles):
    def emit(ma_ref, mb_ref):
        y = jnp.dot(ma_ref[...], wa_ref[...], preferred_element_type=F32)
        y = y + jnp.dot(mb_ref[...], wb_ref[...], preferred_element_type=F32)
        o_ref[...] = x_ref[...] + mod_ref[5:6, :] * y

    is_ctx = pl.program_id(0) < n_ctx_tiles
    pl.when(is_ctx)(lambda: emit(ap_ref, bp_ref))
    pl.when(jnp.logical_not(is_ctx))(lambda: emit(as_ref, bs_ref))


def _outproj(x, mix_ctx, mix_lat, mod, w_out, l, e, row_of_tile, tm):
    t = x.shape[0]
    half = D_MODEL // 2
    n_ctx_tiles = mix_ctx[0].shape[0] // tm
    n_lat_tiles = mix_lat[0].shape[0] // tm
    ctx_blk = lambda i: (jnp.minimum(i, n_ctx_tiles - 1), 0)
    lat_blk = lambda i: (jnp.clip(i - n_ctx_tiles, 0, n_lat_tiles - 1), 0)
    return pl.pallas_call(
        partial(_outproj_kernel, n_ctx_tiles=n_ctx_tiles),
        grid=(t // tm,),
        in_specs=[
            pl.BlockSpec((tm, D_MODEL), lambda i: (i, 0)),
            pl.BlockSpec((tm, half), ctx_blk),
            pl.BlockSpec((tm, half), ctx_blk),
            pl.BlockSpec((tm, half), lat_blk),
            pl.BlockSpec((tm, half), lat_blk),
            pl.BlockSpec((None, None, N_MOD, D_MODEL), lambda i: (l, row_of_tile(i), 0, 0)),
            pl.BlockSpec((None, half, D_MODEL), lambda i: (e, 0, 0)),
            pl.BlockSpec((None, half, D_MODEL), lambda i: (e, 1, 0)),
        ],
        out_specs=pl.BlockSpec((tm, D_MODEL), lambda i: (i, 0)),
        out_shape=jax.ShapeDtypeStruct((t, D_MODEL), F32),
        compiler_params=_params(("arbitrary",)),
        name="outproj",
    )(x, *mix_ctx, *mix_lat, mod, w_out, w_out)


def _final_norm_kernel(x_ref, w_ref, o_ref):
    x = x_ref[...]
    ms = jnp.mean(x * x, axis=-1, keepdims=True)
    o_ref[...] = x * lax.rsqrt(ms + EPS) * w_ref[...]


def _final_norm(x, w, row0, rows, tm=512):
    base = row0 // tm
    return pl.pallas_call(
        _final_norm_kernel,
        grid=(rows // tm,),
        in_specs=[pl.BlockSpec((tm, D_MODEL), lambda i: (base + i, 0)), pl.BlockSpec((1, D_MODEL), lambda i: (0, 0))],
        out_specs=pl.BlockSpec((tm, D_MODEL), lambda i: (i, 0)),
        out_shape=jax.ShapeDtypeStruct((rows, D_MODEL), F32),
        compiler_params=_params(("arbitrary",)),
        name="final_norm",
    )(x, w.reshape(1, D_MODEL))


def _finalize_heads(acc_scr, gate_ref, mix_ref, t, hg, dv):
    tile = min(t, 256)

    def body(i, carry):
        r = _rows(i, tile)
        for u in range(hg):
            cv = slice(u * dv, (u + 1) * dv)
            o = acc_scr[r, cv]
            d = o - jnp.mean(o, axis=-1, keepdims=True)
            var = jnp.mean(d * d, axis=-1, keepdims=True)
            mix_ref[r, cv] = (d * lax.rsqrt(var + EPS) * _silu(gate_ref[r, cv])).astype(BF16)
        return carry

    lax.fori_loop(0, t // tile, body, 0)


def _dwconv_to(dst_ref, src_ref, pad_scr, w_ref, b_ref, t, post):
    c = src_ref.shape[1]
    pad_scr[pl.ds(0, SUBLANE), :] = jnp.zeros((SUBLANE, c), F32)
    pad_scr[pl.ds(t + SUBLANE, SUBLANE), :] = jnp.zeros((SUBLANE, c), F32)
    pad_scr[pl.ds(SUBLANE, t), :] = src_ref[...]
    tile = min(t, 256)
    for r0 in range(0, t, tile):
        y = b_ref[...]
        for k in range(CONV_W):
            y = y + w_ref[k:k + 1, :] * pad_scr[pl.ds(r0 + SUBLANE - CONV_W // 2 + k, tile), :]
        dst_ref[pl.ds(r0, tile), :] = post(y).astype(dst_ref.dtype)


def _tri(reverse, n):
    row = lax.broadcasted_iota(jnp.int32, (n, n), 0)
    col = lax.broadcasted_iota(jnp.int32, (n, n), 1)
    return jnp.where((col >= row) if reverse else (col <= row), 1.0, 0.0).astype(F32)


def _scan_block(a, b, rowid, reverse):
    for s in (1, 2, 4):
        if reverse:
            valid = rowid < SUBLANE - s
            shift = SUBLANE - s
        else:
            valid = rowid >= s
            shift = s
        a_sh = jnp.where(valid, pltpu.roll(a, shift, 0), 1.0)
        b_sh = jnp.where(valid, pltpu.roll(b, shift, 0), 0.0)
        b = b + a * b_sh
        a = a * a_sh
    return a, b


def _lru_kernel(*refs, t, cb, has_init):
    it = iter(refs)
    x_ref, y_ref, cw_ref, cbias_ref, wa_ref, ba_ref, wi_ref, bi_ref, lam_ref = (next(it) for _ in range(9))
    h0_ref = next(it) if has_init else None
    out_ref, sout_ref, pad_scr, xc_scr, a_scr, b_scr = (next(it) for _ in range(6))
    nb = cb // BS_A

    _dwconv_to(xc_scr, x_ref, pad_scr, cw_ref, cbias_ref, t, lambda v: v)

    lam = lam_ref[...]
    sp = jnp.maximum(-lam, 0.0) + jnp.log1p(jnp.exp(-jnp.abs(lam)))
    tile = min(t, 256)

    def gates(i, carry):
        r = _rows(i, tile)
        xc = xc_scr[r, :]
        xcb = xc.astype(BF16)
        for d in range(2):
            ra = jnp.concatenate([jnp.dot(xcb[:, n * BS_A:(n + 1) * BS_A], wa_ref[d, n], preferred_element_type=F32)
                                  for n in range(nb)], axis=1) + ba_ref[d:d + 1, :]
            ia = jnp.concatenate([jnp.dot(xcb[:, n * BS_A:(n + 1) * BS_A], wi_ref[d, n], preferred_element_type=F32)
                                  for n in range(nb)], axis=1) + bi_ref[d:d + 1, :]
            log_a = -LRU_C * _sigmoid(ra) * sp[d:d + 1, :]
            th = jnp.tanh(log_a)
            mult = jnp.sqrt(-2.0 * th / (1.0 - th))
            a_scr[d, r, :] = jnp.exp(log_a)
            b_scr[d, r, :] = mult * _sigmoid(ia) * xc
        return carry

    lax.fori_loop(0, t // tile, gates, 0)

    rowid = lax.broadcasted_iota(jnp.int32, (SUBLANE, cb), 0)
    nblk = t // SUBLANE

    def scan(i, carry):
        hf, hb = carry
        rf = _rows(i, SUBLANE)
        rb = _rows(nblk - 1 - i, SUBLANE)
        af, bf = _scan_block(a_scr[0, rf, :], b_scr[0, rf, :], rowid, False)
        ab, bb = _scan_block(a_scr[1, rb, :], b_scr[1, rb, :], rowid, True)
        hf_blk = bf + af * hf
        hb_blk = bb + ab * hb
        a_scr[0, rf, :] = hf_blk
        a_scr[1, rb, :] = hb_blk
        return hf_blk[SUBLANE - 1:SUBLANE, :], hb_blk[0:1, :]

    if has_init:
        init = (h0_ref[0:1, :], h0_ref[1:2, :])
    else:
        init = (jnp.zeros((1, cb), F32), jnp.zeros((1, cb), F32))
    hf, hb = lax.fori_loop(0, nblk, scan, init, unroll=2)
    sout_ref[0:1, :] = hf
    sout_ref[1:2, :] = hb

    def fin(i, carry):
        r = _rows(i, tile)
        out_ref[r, :] = ((a_scr[0, r, :] + a_scr[1, r, :]) * jax.nn.gelu(y_ref[r, :])).astype(BF16)
        return carry

    lax.fori_loop(0, t // tile, fin, 0)


def _lru(z, base, nseq, t, h0, conv_w, conv_b, w_a, b_a, w_i, b_i, lam, cb=512):
    nj = D_A // cb
    nb = cb // BS_A
    has_init = h0 is not None
    in_specs = [
        pl.BlockSpec((t, cb), lambda s, j: (base + s, j)),
        pl.BlockSpec((t, cb), lambda s, j: (base + s, nj + j)),
        pl.BlockSpec((CONV_W, cb), lambda s, j: (0, j)),
        pl.BlockSpec((1, cb), lambda s, j: (0, j)),
        pl.BlockSpec((2, nb, BS_A, BS_A), lambda s, j: (0, j, 0, 0)),
        pl.BlockSpec((2, cb), lambda s, j: (0, j)),
        pl.BlockSpec((2, nb, BS_A, BS_A), lambda s, j: (0, j, 0, 0)),
        pl.BlockSpec((2, cb), lambda s, j: (0, j)),
        pl.BlockSpec((2, cb), lambda s, j: (0, j)),
    ]
    args = [z, z, conv_w, conv_b.reshape(1, D_A), w_a.astype(BF16), b_a, w_i.astype(BF16), b_i, lam]
    if has_init:
        in_specs.append(pl.BlockSpec((None, 2, cb), lambda s, j: (s, 0, j)))
        args.append(h0)
    return pl.pallas_call(
        partial(_lru_kernel, t=t, cb=cb, has_init=has_init),
        grid=(nseq, nj),
        in_specs=in_specs,
        out_specs=[
            pl.BlockSpec((t, cb), lambda s, j: (s, j)),
            pl.BlockSpec((None, 2, cb), lambda s, j: (s, 0, j)),
        ],
        out_shape=[jax.ShapeDtypeStruct((nseq * t, D_A), BF16), jax.ShapeDtypeStruct((nseq, 2, D_A), F32)],
        scratch_shapes=[
            pltpu.VMEM((t + 2 * SUBLANE, cb), F32),
            pltpu.VMEM((t, cb), F32),
            pltpu.VMEM((2, t, cb), F32),
            pltpu.VMEM((2, t, cb), F32),
        ],
        compiler_params=_params(("arbitrary", "arbitrary")),
        name="rglru",
    )(*args)


def _mlstm_kernel(*refs, t, hg, has_init, has_prev):
    it = iter(refs)
    q_ref, k_ref, v_ref, og_ref, zt_ref, bt_ref, cwq_ref, cbq_ref, cwk_ref, cbk_ref = (next(it) for _ in range(10))
    if has_init:
        c0_ref, n0_ref, m0_ref = next(it), next(it), next(it)
    if has_prev:
        next(it)
    mix_ref, cout_ref, nout_ref, mout_ref = (next(it) for _ in range(4))
    pad_scr, q_scr, k_scr, gate_scr, acc_scr, c_scr, n_scr = (next(it) for _ in range(7))
    CHUNK = MLSTM_CHUNK
    nchunk = t // CHUNK

    _dwconv_to(q_scr, q_ref, pad_scr, cwq_ref, cbq_ref, t, lambda v: _silu(v) * (DK_B ** -0.5))
    _dwconv_to(k_scr, k_ref, pad_scr, cwk_ref, cbk_ref, t, _silu)

    tile = min(t, 256)
    lane_t = lax.broadcasted_iota(jnp.int32, (tile, hg * LANE), 1) % LANE

    def gates(i, carry):
        r = _rows(i, tile)
        x = zt_ref[r, :] + bt_ref[...]
        logsig = jnp.minimum(x, 0.0) - jnp.log1p(jnp.exp(-jnp.abs(x)))
        gate_scr[r, :] = jnp.where(lane_t < 2, x, logsig)
        acc_scr[r, :] = jnp.zeros((tile, hg * DV_B), F32)
        return carry

    lax.fori_loop(0, t // tile, gates, 0)

    if has_init:
        c_scr[...] = c0_ref[...]
        n_scr[...] = n0_ref[...]
        m_init = tuple(m0_ref[d, u, :, 0:1] for u in range(hg) for d in range(2))
    else:
        c_scr[...] = jnp.zeros_like(c_scr)
        n_scr[...] = jnp.zeros_like(n_scr)
        m_init = tuple(jnp.zeros((1, 1), F32) for _ in range(2 * hg))

    row = lax.broadcasted_iota(jnp.int32, (CHUNK, CHUNK), 0)
    col = lax.broadcasted_iota(jnp.int32, (CHUNK, CHUNK), 1)
    lane_c = lax.broadcasted_iota(jnp.int32, (CHUNK, LANE), 1)

    def chunk(cidx, u, d, m):
        reverse = d == 1
        li, lf = d, 2 + d
        last = 0 if reverse else CHUNK - 1
        r = _rows(cidx, CHUNK)
        ck = slice(u * DK_B, (u + 1) * DK_B)
        cv = slice(u * DV_B, (u + 1) * DV_B)
        gt = gate_scr[r, u * LANE:(u + 1) * LANE]
        fsum = jnp.dot(_tri(reverse, CHUNK), gt, precision=HIGHEST, preferred_element_type=F32)
        y = jnp.where(lane_c == lf, fsum, gt)
        yt = y.T
        f_col, i_col = y[:, lf:lf + 1], y[:, li:li + 1]
        f_row, i_row = yt[lf:lf + 1, :], yt[li:li + 1, :]
        b = f_col + m
        mask = (col >= row) if reverse else (col <= row)
        dlog = jnp.where(mask, f_col - f_row + i_row, NEG_INF)
        m_t = jnp.maximum(b, jnp.max(dlog, axis=-1, keepdims=True))
        w = jnp.exp(dlog - m_t)
        inter = jnp.exp(b - m_t)
        q = q_scr[r, ck]
        k32 = k_scr[r, ck]
        v = v_ref[r, cv].astype(BF16)
        s = lax.dot_general(q, k32.astype(BF16), NT, preferred_element_type=F32) * w
        c_old = c_scr[d, u]
        n_old = n_scr[d, u]
        num = jnp.dot(s.astype(BF16), v, preferred_element_type=F32)
        num = num + inter * jnp.dot(q, c_old.astype(BF16), preferred_element_type=F32)
        den = jnp.sum(s, axis=-1, keepdims=True) + inter * jnp.sum(q.astype(F32) * n_old, axis=-1, keepdims=True)
        acc_scr[r, cv] += num / jnp.maximum(jnp.abs(den), jnp.exp(-m_t))
        m_new = m_t[last:last + 1, :]
        f_last = f_col[last:last + 1, :]
        kw = jnp.exp(f_last - f_col + i_col - m_new)
        decay = jnp.exp(f_last + m - m_new)
        kk = k32 * kw
        c_scr[d, u] = decay * c_old + lax.dot_general(kk.astype(BF16), v, TN, preferred_element_type=F32)
        n_scr[d, u] = decay * n_old + jnp.sum(kk, axis=0, keepdims=True)
        return m_new

    def body(c, carry):
        out = []
        for u in range(hg):
            out.append(chunk(c, u, 0, carry[2 * u]))
            out.append(chunk(nchunk - 1 - c, u, 1, carry[2 * u + 1]))
        return tuple(out)

    m_fin = lax.fori_loop(0, nchunk, body, m_init)

    _finalize_heads(acc_scr, og_ref, mix_ref, t, hg, DV_B)
    cout_ref[...] = c_scr[...]
    nout_ref[...] = n_scr[...]
    for u in range(hg):
        for d in range(2):
            mout_ref[d, u] = jnp.broadcast_to(m_fin[2 * u + d], (1, LANE))


def _mlstm(z, zt, bias_t, base, nseq, t, init, conv_w, conv_b, hg, stack=None):
    has_init = init is not None
    prev, slot, nslots = stack if stack is not None else (None, 0, 1)
    kw, vw = hg * DK_B, hg * DV_B
    qoff = 2 * D_A // kw
    koff = qoff + H_B // hg
    voff = koff + H_B // hg
    goff = voff + H_B // hg
    cb2 = conv_b.reshape(1, 2 * H_B * DK_B)
    in_specs = [
        pl.BlockSpec((t, kw), lambda s, h: (base + s, qoff + h)),
        pl.BlockSpec((t, kw), lambda s, h: (base + s, koff + h)),
        pl.BlockSpec((t, vw), lambda s, h: (base + s, voff + h)),
        pl.BlockSpec((t, vw), lambda s, h: (base + s, goff + h)),
        pl.BlockSpec((t, hg * LANE), lambda s, h: (base + s, h)),
        pl.BlockSpec((1, hg * LANE), lambda s, h: (0, h)),
        pl.BlockSpec((CONV_W, kw), lambda s, h: (0, h)),
        pl.BlockSpec((1, kw), lambda s, h: (0, h)),
        pl.BlockSpec((CONV_W, kw), lambda s, h: (0, H_B // hg + h)),
        pl.BlockSpec((1, kw), lambda s, h: (0, H_B // hg + h)),
    ]
    args = [z, z, z, z, zt, bias_t, conv_w, cb2, conv_w, cb2]
    if has_init:
        c0, n0, m0 = init
        in_specs += [
            pl.BlockSpec((None, 2, hg, DK_B, DV_B), lambda s, h: (s, 0, h, 0, 0)),
            pl.BlockSpec((None, 2, hg, 1, DK_B), lambda s, h: (s, 0, h, 0, 0)),
            pl.BlockSpec((None, 2, hg, 1, LANE), lambda s, h: (s, 0, h, 0, 0)),
        ]
        args += [c0, n0.reshape(nseq, 2, H_B, 1, DK_B),
                 jnp.broadcast_to(m0[..., None, None], (nseq, 2, H_B, 1, LANE))]
    aliases = {}
    if prev is not None:
        in_specs.append(pl.BlockSpec(memory_space=pl.ANY))
        args.append(prev)
        aliases = {len(args) - 1: 1}
    out = pl.pallas_call(
        partial(_mlstm_kernel, t=t, hg=hg, has_init=has_init, has_prev=prev is not None),
        grid=(nseq, H_B // hg),
        in_specs=in_specs,
        out_specs=[
            pl.BlockSpec((t, vw), lambda s, h: (s, h)),
            pl.BlockSpec((None, None, 2, hg, DK_B, DV_B), lambda s, h: (s, slot, 0, h, 0, 0)),
            pl.BlockSpec((None, 2, hg, 1, DK_B), lambda s, h: (s, 0, h, 0, 0)),
            pl.BlockSpec((None, 2, hg, 1, LANE), lambda s, h: (s, 0, h, 0, 0)),
        ],
        out_shape=[
            jax.ShapeDtypeStruct((nseq * t, H_B * DV_B), BF16),
            jax.ShapeDtypeStruct((nseq, nslots, 2, H_B, DK_B, DV_B), F32),
            jax.ShapeDtypeStruct((nseq, 2, H_B, 1, DK_B), F32),
            jax.ShapeDtypeStruct((nseq, 2, H_B, 1, LANE), F32),
        ],
        scratch_shapes=[
            pltpu.VMEM((t + 2 * SUBLANE, kw), F32),
            pltpu.VMEM((t, kw), BF16),
            pltpu.VMEM((t, kw), F32),
            pltpu.VMEM((t, hg * LANE), F32),
            pltpu.VMEM((t, vw), F32),
            pltpu.VMEM((2, hg, DK_B, DV_B), F32),
            pltpu.VMEM((2, hg, 1, DK_B), F32),
        ],
        input_output_aliases=aliases,
        compiler_params=_params(("arbitrary", "arbitrary")),
        name="mlstm",
    )(*args)
    return out if stack is not None else (out[0], out[1][:, 0], out[2], out[3])


def _ret_kernel(*refs, t, hg, use_rope, has_init, has_prev):
    it = iter(refs)
    lg_ref, q_ref, k_ref, v_ref, g_ref = (next(it) for _ in range(5))
    if use_rope:
        cos_ref, sin_ref = next(it), next(it)
    s0_ref = next(it) if has_init else None
    if has_prev:
        next(it)
    mix_ref, sout_ref, acc_scr, st_scr = (next(it) for _ in range(4))
    CHUNK = RET_CHUNK
    nchunk = t // CHUNK
    h0 = pl.program_id(1) * hg

    row = lax.broadcasted_iota(jnp.int32, (CHUNK, CHUNK), 0)
    col = lax.broadcasted_iota(jnp.int32, (CHUNK, CHUNK), 1)
    dist = (row - col).astype(F32)
    pos = lax.broadcasted_iota(jnp.int32, (CHUNK, DK_C), 0).astype(F32)
    consts = {}
    for u in range(hg):
        for d in range(2):
            lg = lg_ref[d, h0 + u]
            if d == 0:
                dm = jnp.where(dist >= 0, jnp.exp(jnp.maximum(dist, 0.0) * lg), 0.0)
                qin = jnp.exp((pos + 1.0) * lg)
                kout = jnp.exp((CHUNK - 1.0 - pos) * lg)
            else:
                dm = jnp.where(dist <= 0, jnp.exp(jnp.maximum(-dist, 0.0) * lg), 0.0)
                qin = jnp.exp((CHUNK - pos) * lg)
                kout = jnp.exp(pos * lg)
            gch = jnp.exp(jnp.full((1, DV_C), float(CHUNK), F32) * lg)
            consts[u, d] = (dm, qin, kout, gch)

    acc_scr[...] = jnp.zeros_like(acc_scr)
    if has_init:
        st_scr[...] = s0_ref[...]
    else:
        st_scr[...] = jnp.zeros_like(st_scr)

    def chunk(cidx, u, d):
        dm, qin, kout, gch = consts[u, d]
        r = _rows(cidx, CHUNK)
        ck = slice(u * DK_C, (u + 1) * DK_C)
        cv = slice(u * DV_C, (u + 1) * DV_C)
        q = q_ref[r, ck]
        k = k_ref[r, ck] * (DK_C ** -0.5)
        if use_rope:
            cs, sn = cos_ref[r, :], sin_ref[r, :]
            q = q * cs + pltpu.roll(q, DK_C // 2, 1) * sn
            k = k * cs + pltpu.roll(k, DK_C // 2, 1) * sn
        v = v_ref[r, cv].astype(BF16)
        st = st_scr[d, u]
        s = lax.dot_general(q.astype(BF16), k.astype(BF16), NT, preferred_element_type=F32) * dm
        o = jnp.dot(s.astype(BF16), v, preferred_element_type=F32)
        o = o + jnp.dot((q * qin).astype(BF16), st.astype(BF16), preferred_element_type=F32)
        st_scr[d, u] = gch * st + lax.dot_general((k * kout).astype(BF16), v, TN, preferred_element_type=F32)
        acc_scr[r, cv] += o

    def body(c, carry):
        for u in range(hg):
            chunk(c, u, 0)
            chunk(nchunk - 1 - c, u, 1)
        return carry

    lax.fori_loop(0, nchunk, body, 0)
    _finalize_heads(acc_scr, g_ref, mix_ref, t, hg, DV_C)
    sout_ref[...] = st_scr[...]


def _retention(z, base, nseq, t, s0, lg, rope, hg, stack=None):
    has_init = s0 is not None
    prev, slot, nslots = stack if stack is not None else (None, 0, 1)
    use_rope = rope is not None
    kw, vw = hg * DK_C, hg * DV_C
    qoff, koff = 0, H_C * DK_C // kw
    voff = 2 * H_C * DK_C // vw
    goff = voff + H_C // hg
    in_specs = [
        pl.BlockSpec(memory_space=pltpu.SMEM),
        pl.BlockSpec((t, kw), lambda s, h: (base + s, qoff + h)),
        pl.BlockSpec((t, kw), lambda s, h: (base + s, koff + h)),
        pl.BlockSpec((t, vw), lambda s, h: (base + s, voff + h)),
        pl.BlockSpec((t, vw), lambda s, h: (base + s, goff + h)),
    ]
    args = [lg, z, z, z, z]
    if use_rope:
        in_specs += [pl.BlockSpec((t, DK_C), lambda s, h: (0, 0)), pl.BlockSpec((t, DK_C), lambda s, h: (0, 0))]
        args += list(rope)
    if has_init:
        in_specs.append(pl.BlockSpec((None, 2, hg, DK_C, DV_C), lambda s, h: (s, 0, h, 0, 0)))
        args.append(s0)
    aliases = {}
    if prev is not None:
        in_specs.append(pl.BlockSpec(memory_space=pl.ANY))
        args.append(prev)
        aliases = {len(args) - 1: 1}
    out = pl.pallas_call(
        partial(_ret_kernel, t=t, hg=hg, use_rope=use_rope, has_init=has_init, has_prev=prev is not None),
        grid=(nseq, H_C // hg),
        in_specs=in_specs,
        out_specs=[
            pl.BlockSpec((t, vw), lambda s, h: (s, h)),
            pl.BlockSpec((None, None, 2, hg, DK_C, DV_C), lambda s, h: (s, slot, 0, h, 0, 0)),
        ],
        out_shape=[
            jax.ShapeDtypeStruct((nseq * t, H_C * DV_C), BF16),
            jax.ShapeDtypeStruct((nseq, nslots, 2, H_C, DK_C, DV_C), F32),
        ],
        scratch_shapes=[pltpu.VMEM((t, vw), F32), pltpu.VMEM((2, hg, DK_C, DV_C), F32)],
        input_output_aliases=aliases,
        compiler_params=_params(("arbitrary", "arbitrary")),
        name="retention",
    )(*args)
    return out if stack is not None else (out[0], out[1][:, 0])


def _gla_intra(q, k, g2, reverse):
    CHUNK = GLA_CHUNK
    nsub = CHUNK // SUB
    lane = lax.broadcasted_iota(jnp.int32, (SUB, CHUNK), 1)
    rowi = lax.broadcasted_iota(jnp.int32, (SUB, CHUNK), 0)
    out = []
    for jb in range(nsub):
        lo, hi = jb * SUB, (jb + 1) * SUB
        qj, gj = q[lo:hi], g2[lo:hi]
        if reverse and jb < nsub - 1:
            gref = g2[hi:hi + 1]
            kt = (k[hi:] * jnp.exp2(gref - g2[hi:])).astype(BF16)
            kt = jnp.concatenate([jnp.zeros((hi, DK_D), BF16), kt], axis=0)
        elif not reverse and jb > 0:
            gref = g2[lo - 1:lo]
            kt = (k[:lo] * jnp.exp2(gref - g2[:lo])).astype(BF16)
            kt = jnp.concatenate([kt, jnp.zeros((CHUNK - lo, DK_D), BF16)], axis=0)
        else:
            kt = None
        if kt is None:
            off = jnp.zeros((SUB, CHUNK), F32)
        else:
            qt = (qj * jnp.exp2(gj - gref)).astype(BF16)
            off = lax.dot_general(qt, kt, NT, preferred_element_type=F32)
        diag = jnp.zeros((SUB, CHUNK), F32)
        for i in range(lo, hi):
            col = jnp.sum(qj * jnp.exp2(gj - g2[i:i + 1]) * k[i:i + 1], axis=-1, keepdims=True)
            diag = jnp.where(lane == i, col, diag)
        mask = (rowi + lo <= lane) if reverse else (rowi + lo >= lane)
        out.append(off + jnp.where(mask, diag, 0.0))
    return jnp.concatenate(out, axis=0)


def _gla_kernel(*refs, t, hg, has_init, has_prev):
    it = iter(refs)
    q_ref, k_ref, v_ref, r_ref, a_ref, wup_ref, gb_ref = (next(it) for _ in range(7))
    s0_ref = next(it) if has_init else None
    if has_prev:
        next(it)
    mix_ref, sout_ref, acc_scr, st_scr, g_scr = (next(it) for _ in range(5))
    CHUNK = GLA_CHUNK
    nchunk = t // CHUNK
    tile = min(t, 256)

    def gates(i, carry):
        r = _rows(i, tile)
        a = a_ref[r, :].astype(BF16)
        for u in range(hg):
            for d in range(2):
                x = jnp.dot(a, wup_ref[d, u], preferred_element_type=F32) + gb_ref[d, u]
                g_scr[d, u, r, :] = (jnp.minimum(x, 0.0) - jnp.log1p(jnp.exp(-jnp.abs(x)))) * (1.0 / GLA_TAU)
        acc_scr[r, :] = jnp.zeros((tile, hg * DV_D), F32)
        return carry

    lax.fori_loop(0, t // tile, gates, 0)

    for u in range(hg):
        for d in range(2):
            st_scr[d, u] = s0_ref[d, u].T if has_init else jnp.zeros((DV_D, DK_D), F32)

    def chunk(cidx, u, d):
        reverse = d == 1
        last = 0 if reverse else CHUNK - 1
        r = _rows(cidx, CHUNK)
        ck = slice(u * DK_D, (u + 1) * DK_D)
        cv = slice(u * DV_D, (u + 1) * DV_D)
        g2 = jnp.dot(_tri(reverse, CHUNK), g_scr[d, u, r, :], precision=HIGHEST, preferred_element_type=F32) * LOG2E
        g2_last = g2[last:last + 1, :]
        q = q_ref[r, ck] * (DK_D ** -0.5)
        k = k_ref[r, ck]
        v = v_ref[r, cv].astype(BF16)
        st = st_scr[d, u]
        s = _gla_intra(q, k, g2, reverse)
        o = jnp.dot(s.astype(BF16), v, preferred_element_type=F32)
        o = o + lax.dot_general((q * jnp.exp2(g2)).astype(BF16), st.astype(BF16), NT, preferred_element_type=F32)
        kd = (k * jnp.exp2(g2_last - g2)).astype(BF16)
        st_scr[d, u] = jnp.exp2(g2_last) * st + lax.dot_general(v, kd, TN, preferred_element_type=F32)
        acc_scr[r, cv] += o

    def body(c, carry):
        for u in range(hg):
            chunk(c, u, 0)
            chunk(nchunk - 1 - c, u, 1)
        return carry

    lax.fori_loop(0, nchunk, body, 0)
    _finalize_heads(acc_scr, r_ref, mix_ref, t, hg, DV_D)
    for u in range(hg):
        for d in range(2):
            sout_ref[d, u] = st_scr[d, u].T


def _gla(z, zt, base, nseq, t, s0, w_up, g_bias, hg, stack=None):
    has_init = s0 is not None
    prev, slot, nslots = stack if stack is not None else (None, 0, 1)
    kw, vw = hg * DK_D, hg * DV_D
    cbase = 2 * H_C * DK_C + 2 * H_C * DV_C
    qoff = cbase // kw
    koff = qoff + H_D // hg
    voff = (cbase + 2 * H_D * DK_D) // vw
    roff = voff + H_D // hg
    in_specs = [
        pl.BlockSpec((t, kw), lambda s, h: (base + s, qoff + h)),
        pl.BlockSpec((t, kw), lambda s, h: (base + s, koff + h)),
        pl.BlockSpec((t, vw), lambda s, h: (base + s, voff + h)),
        pl.BlockSpec((t, vw), lambda s, h: (base + s, roff + h)),
        pl.BlockSpec((t, LANE), lambda s, h: (base + s, 0)),
        pl.BlockSpec((2, hg, LANE, DK_D), lambda s, h: (0, h, 0, 0)),
        pl.BlockSpec((2, hg, 1, DK_D), lambda s, h: (0, h, 0, 0)),
    ]
    args = [z, z, z, z, zt, w_up, g_bias]
    if has_init:
        in_specs.append(pl.BlockSpec((None, 2, hg, DK_D, DV_D), lambda s, h: (s, 0, h, 0, 0)))
        args.append(s0)
    aliases = {}
    if prev is not None:
        in_specs.append(pl.BlockSpec(memory_space=pl.ANY))
        args.append(prev)
        aliases = {len(args) - 1: 1}
    out = pl.pallas_call(
        partial(_gla_kernel, t=t, hg=hg, has_init=has_init, has_prev=prev is not None),
        grid=(nseq, H_D // hg),
        in_specs=in_specs,
        out_specs=[
            pl.BlockSpec((t, vw), lambda s, h: (s, h)),
            pl.BlockSpec((None, None, 2, hg, DK_D, DV_D), lambda s, h: (s, slot, 0, h, 0, 0)),
        ],
        out_shape=[
            jax.ShapeDtypeStruct((nseq * t, H_D * DV_D), BF16),
            jax.ShapeDtypeStruct((nseq, nslots, 2, H_D, DK_D, DV_D), F32),
        ],
        scratch_shapes=[
            pltpu.VMEM((t, vw), F32),
            pltpu.VMEM((2, hg, DV_D, DK_D), F32),
            pltpu.VMEM((2, hg, t, DK_D), F32),
        ],
        input_output_aliases=aliases,
        compiler_params=_params(("arbitrary", "arbitrary")),
        name="gla",
    )(*args)
    return out if stack is not None else (out[0], out[1][:, 0])


def _rope_tables(t):
    pos = jnp.arange(t)
    row = (pos // GRID_W).astype(F32)
    col = (pos % GRID_W).astype(F32)
    nf = DK_C // 4
    freqs = ROPE_BASE ** (-jnp.arange(nf, dtype=F32) / nf)
    ang = jnp.concatenate([row[:, None] * freqs, col[:, None] * freqs], axis=-1)
    cos, sin = jnp.cos(ang), jnp.sin(ang)
    return jnp.concatenate([cos, cos], axis=-1), jnp.concatenate([-sin, sin], axis=-1)


def _ab_tail(w_in, i_bias, f_bias):
    n_main = w_in.shape[1] - 4 * H_B
    gates = w_in[:, n_main:].reshape(D_MODEL, 4, H_B)
    cols = jnp.swapaxes(gates, 1, 2)
    w_tail = jnp.pad(cols, ((0, 0), (0, 0), (0, LANE - 4))).reshape(D_MODEL, H_B * LANE)
    bias = jnp.concatenate([i_bias, f_bias], axis=0).T
    bias_t = jnp.pad(bias, ((0, 0), (0, LANE - 4))).reshape(1, H_B * LANE)
    return n_main, w_tail, bias_t


def _cd_tail(w_in, w_up, g_bias):
    n_main = w_in.shape[1] - 2 * GLA_RANK
    w_tail = jnp.pad(w_in[:, n_main:], ((0, 0), (0, LANE - 2 * GLA_RANK)))
    up = w_up.reshape(2, GLA_RANK, H_D, DK_D).transpose(0, 2, 1, 3)
    up = jnp.stack([jnp.pad(up[d], ((0, 0), (d * GLA_RANK, LANE - (d + 1) * GLA_RANK), (0, 0))) for d in range(2)])
    return n_main, w_tail, up.astype(BF16), g_bias.reshape(2, H_D, 1, DK_D)


def kernel(x_prompt, x_sample, c, state_lru, state_mlstm_C, state_mlstm_n, state_mlstm_m, state_ret, state_gla, c_ctx, w_mod, b_mod, norm_w, ffn_w_gate, ffn_w_up, ffn_w_down, w_in_ab, w_out_ab, lru_conv_w, lru_conv_b, lru_w_a, lru_b_a, lru_w_i, lru_b_i, lru_lambda, mlstm_conv_w, mlstm_conv_b, mlstm_i_bias, mlstm_f_bias, w_in_cd, w_out_cd, ret_decay_log, gla_w_up, gla_b, final_norm_w):
    Bp, Tp, D = x_prompt.shape
    Bs, Ts, _ = x_sample.shape
    n_ctx = Bp * Tp
    tm_ffn, tf_ffn, tm_in, tm_out = 1024, 256, 1024, 512
    assert all(n_ctx % t == 0 and Ts % t == 0 for t in (tm_ffn, tm_in, tm_out)) and 1 + Bs <= MOD_ROWS and n_ctx % Ts == 0
    row_ffn, row_in, row_out = (_mod_row_map(n_ctx // t, Ts // t) for t in (tm_ffn, tm_in, tm_out))
    groups = ((0, Bp, Tp), (n_ctx // Ts, Bs, Ts))
    hg_ctx = {"mlstm": 4, "ret": 4, "gla": 4}
    hg_lat = {"mlstm": 1, "ret": 2, "gla": 2}

    cond = jnp.concatenate([c_ctx[None], c, jnp.zeros((MOD_ROWS - 1 - Bs, D), F32)], axis=0)
    mod = _modulation(cond, w_mod, b_mod).reshape(DEPTH, MOD_ROWS, N_MOD, D)
    nw = norm_w.reshape(DEPTH, 3, 1, D)
    wg, wu, wd = ffn_w_gate, ffn_w_up, ffn_w_down
    w_in_ab_b, w_in_cd_b = w_in_ab.astype(BF16), w_in_cd.astype(BF16)
    w_out_ab_b, w_out_cd_b = w_out_ab.astype(BF16), w_out_cd.astype(BF16)

    x = jnp.concatenate([x_prompt.reshape(n_ctx, D), x_sample.reshape(Bs * Ts, D)], axis=0)
    rope = _rope_tables(Ts)

    st_lru, st_n, st_m = [], [], []
    new_C = new_ret = new_gla = None
    for l in range(DEPTH):
        x = _ffn(x, mod, nw, wg, wu, wd, l, 0, row_ffn, tm_ffn, tf_ffn)
        (pb, pn, pt), (sb, sn, st) = groups
        if l % 2 == 0:
            e = l // 2
            n_main, w_tail, bias_t = _ab_tail(w_in_ab[e], mlstm_i_bias[e], mlstm_f_bias[e])
            z, zt = _inproj(x, mod, nw, w_in_ab_b, l, e, n_main, w_tail.astype(BF16), row_in, tm_in)
            lru_args = (lru_conv_w[e], lru_conv_b[e], lru_w_a[e], lru_b_a[e], lru_w_i[e], lru_b_i[e], lru_lambda[e])
            a_p, s_lru = _lru(z, pb, pn, pt, None, *lru_args)
            a_s, _ = _lru(z, sb, sn, st, state_lru[:, e], *lru_args)
            b_p, new_C, s_n, s_m = _mlstm(z, zt, bias_t, pb, pn, pt, None, mlstm_conv_w[e], mlstm_conv_b[e], hg_ctx["mlstm"],
                                          stack=(new_C, e, N_EVEN))
            b_s, _, _, _ = _mlstm(z, zt, bias_t, sb, sn, st,
                                  (state_mlstm_C[:, e], state_mlstm_n[:, e], state_mlstm_m[:, e]),
                                  mlstm_conv_w[e], mlstm_conv_b[e], hg_lat["mlstm"])
            st_lru.append(s_lru)
            st_n.append(s_n.reshape(Bp, 2, H_B, DK_B))
            st_m.append(s_m[:, :, :, 0, 0])
            mix_ctx, mix_lat = (a_p, b_p), (a_s, b_s)
            w_out = w_out_ab_b
        else:
            e = l // 2
            n_main, w_tail, w_up, g_bias = _cd_tail(w_in_cd[e], gla_w_up[e], gla_b[e])
            z, zt = _inproj(x, mod, nw, w_in_cd_b, l, e, n_main, w_tail.astype(BF16), row_in, tm_in)
            c_p, new_ret = _retention(z, pb, pn, pt, None, ret_decay_log[e], None, hg_ctx["ret"], stack=(new_ret, e, N_ODD))
            c_s, _ = _retention(z, sb, sn, st, state_ret[:, e], ret_decay_log[e], rope, hg_lat["ret"])
            d_p, new_gla = _gla(z, zt, pb, pn, pt, None, w_up, g_bias, hg_ctx["gla"], stack=(new_gla, e, N_ODD))
            d_s, _ = _gla(z, zt, sb, sn, st, state_gla[:, e], w_up, g_bias, hg_lat["gla"])
            mix_ctx, mix_lat = (c_p, d_p), (c_s, d_s)
            w_out = w_out_cd_b
        x = _outproj(x, mix_ctx, mix_lat, mod, w_out, l, e, row_out, tm_out)
        x = _ffn(x, mod, nw, wg, wu, wd, l, 1, row_ffn, tm_ffn, tf_ffn)

    y_prompt = _final_norm(x, final_norm_w, 0, n_ctx).reshape(Bp, Tp, D)
    y_sample = _final_norm(x, final_norm_w, n_ctx, Bs * Ts).reshape(Bs, Ts, D)
    return (y_prompt, y_sample,
            jnp.stack(st_lru, axis=1), new_C, jnp.stack(st_n, axis=1), jnp.stack(st_m, axis=1), new_ret, new_gla)
```

```python
from functools import partial

import jax
import jax.numpy as jnp
from jax import lax
from jax.experimental import pallas as pl
from jax.experimental.pallas import tpu as pltpu

F32 = jnp.float32
BF16 = jnp.bfloat16

D_MODEL = 2048
DEPTH = 4
N_MOD = 9
N_EVEN = (DEPTH + 1) // 2
N_ODD = DEPTH // 2
D_FF = 5632
EPS = 1e-6
CHUNK = 64
CONV_W = 4
D_A = D_MODEL // 2
NB_A = 8
BS_A = D_A // NB_A
LRU_C = 8.0
H_B = 4
DK_B = D_MODEL // 8
DV_B = D_MODEL // 8
H_C = 4
DK_C = D_MODEL // 16
DV_C = D_MODEL // 8
H_D = 4
DK_D = D_MODEL // 16
DV_D = D_MODEL // 8
GLA_RANK = 16
GLA_TAU = 16.0
ROPE_BASE = 10000.0
GRID_W = 64

MOD_ROWS = 8
LANE = 128
SUBLANE = 8
SUB = 16
MLSTM_CHUNK = 256
RET_CHUNK = 256
GLA_CHUNK = 128
VMEM_LIMIT = 56 * 1024 * 1024
FFN_VMEM_LIMIT = 60 * 1024 * 1024
NEG_INF = float("-inf")
LOG2E = 1.4426950408889634
HIGHEST = lax.Precision.HIGHEST
NT = (((1,), (1,)), ((), ()))
TN = (((0,), (0,)), ((), ()))


def _params(sem, vmem_limit=VMEM_LIMIT):
    return pltpu.CompilerParams(dimension_semantics=sem, vmem_limit_bytes=vmem_limit)


def _silu(x):
    return x * jax.nn.sigmoid(x)


def _sigmoid(x):
    return 0.5 * jnp.tanh(0.5 * x) + 0.5


def _rows(i, n):
    return pl.ds(pl.multiple_of(i * n, n), n)


def _mod_kernel(c_ref, w_ref, b_ref, o_ref):
    s = _silu(c_ref[...]).astype(BF16)
    o_ref[...] = jnp.dot(s, w_ref[...].astype(BF16), preferred_element_type=F32) + b_ref[...]


def _modulation(cond, w_mod, b_mod, tn=1024):
    n = w_mod.shape[-1]
    return pl.pallas_call(
        _mod_kernel,
        grid=(DEPTH, n // tn),
        in_specs=[
            pl.BlockSpec((MOD_ROWS, D_MODEL), lambda l, j: (0, 0)),
            pl.BlockSpec((None, D_MODEL, tn), lambda l, j: (l, 0, j)),
            pl.BlockSpec((None, 1, tn), lambda l, j: (l, 0, j)),
        ],
        out_specs=pl.BlockSpec((None, MOD_ROWS, tn), lambda l, j: (l, 0, j)),
        out_shape=jax.ShapeDtypeStruct((DEPTH, MOD_ROWS, n), F32),
        compiler_params=_params(("arbitrary", "arbitrary")),
        name="modulation",
    )(cond, w_mod, b_mod.reshape(DEPTH, 1, n))


def _norm_mod(x, nw, shift, scale):
    ms = jnp.mean(x * x, axis=-1, keepdims=True)
    y = x * lax.rsqrt(ms + EPS) * nw
    return y * (1.0 + scale) + shift


def _mod_row_map(n_ctx_tiles, tiles_per_req):
    def row(i):
        return jnp.where(i < n_ctx_tiles, 0, 1 + (i - n_ctx_tiles) // tiles_per_req)
    return row


def _ffn_kernel(x_ref, mod_ref, nw_ref, wg_ref, wu_ref, wd_ref, o_ref, h_scr, *, k, nf):
    f = pl.program_id(1)

    @pl.when(f == 0)
    def _():
        h = _norm_mod(x_ref[...], nw_ref[...], mod_ref[3 * k:3 * k + 1, :], mod_ref[3 * k + 1:3 * k + 2, :])
        h_scr[...] = h.astype(BF16)

    def down_proj():
        h = h_scr[...]
        g = jnp.dot(h, wg_ref[...].astype(BF16), preferred_element_type=F32)
        u = jnp.dot(h, wu_ref[...].astype(BF16), preferred_element_type=F32)
        a = (_silu(g) * u).astype(BF16)
        return jnp.dot(a, wd_ref[...].astype(BF16), preferred_element_type=F32)

    @pl.when(f == 0)
    def _():
        o_ref[...] = down_proj()

    @pl.when(jnp.logical_and(f > 0, f < nf - 1))
    def _():
        o_ref[...] += down_proj()

    @pl.when(f == nf - 1)
    def _():
        o_ref[...] = x_ref[...] + (0.5 * mod_ref[3 * k + 2:3 * k + 3, :]) * (o_ref[...] + down_proj())


def _ffn(x, mod, nw, wg, wu, wd, l, j, row_of_tile, tm, tf):
    t = x.shape[0]
    nf = D_FF // tf
    assert nf >= 2
    k = 2 * j
    return pl.pallas_call(
        partial(_ffn_kernel, k=k, nf=nf),
        grid=(t // tm, nf),
        in_specs=[
            pl.BlockSpec((tm, D_MODEL), lambda i, f: (i, 0)),
            pl.BlockSpec((None, None, N_MOD, D_MODEL), lambda i, f: (l, row_of_tile(i), 0, 0)),
            pl.BlockSpec((None, None, 1, D_MODEL), lambda i, f: (l, k, 0, 0)),
            pl.BlockSpec((None, None, D_MODEL, tf), lambda i, f: (l, j, 0, f)),
            pl.BlockSpec((None, None, D_MODEL, tf), lambda i, f: (l, j, 0, f)),
            pl.BlockSpec((None, None, tf, D_MODEL), lambda i, f: (l, j, f, 0)),
        ],
        out_specs=pl.BlockSpec((tm, D_MODEL), lambda i, f: (i, 0)),
        out_shape=jax.ShapeDtypeStruct((t, D_MODEL), F32),
        scratch_shapes=[pltpu.VMEM((tm, D_MODEL), BF16)],
        compiler_params=_params(("arbitrary", "arbitrary"), FFN_VMEM_LIMIT),
        name="ffn",
    )(x, mod, nw, wg, wu, wd)


def _inproj_kernel(x_ref, mod_ref, nw_ref, w_ref, wt_ref, z_ref, zt_ref, h_scr):
    j = pl.program_id(1)

    @pl.when(j == 0)
    def _():
        h = _norm_mod(x_ref[...], nw_ref[...], mod_ref[3:4, :], mod_ref[4:5, :]).astype(BF16)
        h_scr[...] = h
        zt_ref[...] = jnp.dot(h, wt_ref[...], preferred_element_type=F32)
        z_ref[...] = jnp.dot(h, w_ref[...], preferred_element_type=F32)

    @pl.when(j > 0)
    def _():
        z_ref[...] = jnp.dot(h_scr[...], w_ref[...], preferred_element_type=F32)


def _inproj(x, mod, nw, w_in, l, e, n, w_tail, row_of_tile, tm, tn=1024):
    t = x.shape[0]
    tw = w_tail.shape[1]
    return pl.pallas_call(
        _inproj_kernel,
        grid=(t // tm, n // tn),
        in_specs=[
            pl.BlockSpec((tm, D_MODEL), lambda i, j: (i, 0)),
            pl.BlockSpec((None, None, N_MOD, D_MODEL), lambda i, j: (l, row_of_tile(i), 0, 0)),
            pl.BlockSpec((None, None, 1, D_MODEL), lambda i, j: (l, 1, 0, 0)),
            pl.BlockSpec((None, D_MODEL, tn), lambda i, j: (e, 0, j)),
            pl.BlockSpec((D_MODEL, tw), lambda i, j: (0, 0)),
        ],
        out_specs=[
            pl.BlockSpec((tm, tn), lambda i, j: (i, j)),
            pl.BlockSpec((tm, tw), lambda i, j: (i, 0)),
        ],
        out_shape=[jax.ShapeDtypeStruct((t, n), F32), jax.ShapeDtypeStruct((t, tw), F32)],
        scratch_shapes=[pltpu.VMEM((tm, D_MODEL), BF16)],
        compiler_params=_params(("arbitrary", "arbitrary")),
        name="inproj",
    )(x, mod, nw, w_in, w_tail)


def _outproj_kernel(x_ref, ap_ref, bp_ref, as_ref, bs_ref, mod_ref, wa_ref, wb_ref, o_ref, *, n_ctx_tiles):
    def emit(ma_ref, mb_ref):
        y = jnp.dot(ma_ref[...], wa_ref[...], preferred_element_type=F32)
        y = y + jnp.dot(mb_ref[...], wb_ref[...], preferred_element_type=F32)
        o_ref[...] = x_ref[...] + mod_ref[5:6, :] * y

    is_ctx = pl.program_id(0) < n_ctx_tiles
    pl.when(is_ctx)(lambda: emit(ap_ref, bp_ref))
    pl.when(jnp.logical_not(is_ctx))(lambda: emit(as_ref, bs_ref))


def _outproj(x, mix_ctx, mix_lat, mod, w_out, l, e, row_of_tile, tm):
    t = x.shape[0]
    half = D_MODEL // 2
    n_ctx_tiles = mix_ctx[0].shape[0] // tm
    n_lat_tiles = mix_lat[0].shape[0] // tm
    ctx_blk = lambda i: (jnp.minimum(i, n_ctx_tiles - 1), 0)
    lat_blk = lambda i: (jnp.clip(i - n_ctx_tiles, 0, n_lat_tiles - 1), 0)
    return pl.pallas_call(
        partial(_outproj_kernel, n_ctx_tiles=n_ctx_tiles),
        grid=(t // tm,),
        in_specs=[
            pl.BlockSpec((tm, D_MODEL), lambda i: (i, 0)),
            pl.BlockSpec((tm, half), ctx_blk),
            pl.BlockSpec((tm, half), ctx_blk),
            pl.BlockSpec((tm, half), lat_blk),
            pl.BlockSpec((tm, half), lat_blk),
            pl.BlockSpec((None, None, N_MOD, D_MODEL), lambda i: (l, row_of_tile(i), 0, 0)),
            pl.BlockSpec((None, half, D_MODEL), lambda i: (e, 0, 0)),
            pl.BlockSpec((None, half, D_MODEL), lambda i: (e, 1, 0)),
        ],
        out_specs=pl.BlockSpec((tm, D_MODEL), lambda i: (i, 0)),
        out_shape=jax.ShapeDtypeStruct((t, D_MODEL), F32),
        compiler_params=_params(("arbitrary",)),
        name="outproj",
    )(x, *mix_ctx, *mix_lat, mod, w_out, w_out)


def _final_norm_kernel(x_ref, w_ref, o_ref):
    x = x_ref[...]
    ms = jnp.mean(x * x, axis=-1, keepdims=True)
    o_ref[...] = x * lax.rsqrt(ms + EPS) * w_ref[...]


def _final_norm(x, w, row0, rows, tm=512):
    base = row0 // tm
    return pl.pallas_call(
        _final_norm_kernel,
        grid=(rows // tm,),
        in_specs=[pl.BlockSpec((tm, D_MODEL), lambda i: (base + i, 0)), pl.BlockSpec((1, D_MODEL), lambda i: (0, 0))],
        out_specs=pl.BlockSpec((tm, D_MODEL), lambda i: (i, 0)),
        out_shape=jax.ShapeDtypeStruct((rows, D_MODEL), F32),
        compiler_params=_params(("arbitrary",)),
        name="final_norm",
    )(x, w.reshape(1, D_MODEL))


def _finalize_heads(acc_scr, gate_ref, mix_ref, t, hg, dv):
    tile = min(t, 256)

    def body(i, carry):
        r = _rows(i, tile)
        for u in range(hg):
            cv = slice(u * dv, (u + 1) * dv)
            o = acc_scr[r, cv]
            d = o - jnp.mean(o, axis=-1, keepdims=True)
            var = jnp.mean(d * d, axis=-1, keepdims=True)
            mix_ref[r, cv] = (d * lax.rsqrt(var + EPS) * _silu(gate_ref[r, cv])).astype(BF16)
        return carry

    lax.fori_loop(0, t // tile, body, 0)


def _state_target(out_ref, fill):
    if fill is None:
        return out_ref
    slot, nslots = fill
    for o in range(nslots):
        if o != slot:
            out_ref[o] = jnp.zeros(out_ref.shape[1:], out_ref.dtype)
    return out_ref.at[slot]


def _state_spec(fill, slot, hg, dk, dv):
    if fill is None:
        return pl.BlockSpec((None, None, 2, hg, dk, dv), lambda s, h: (s, slot, 0, h, 0, 0))
    return pl.BlockSpec((None, fill[1], 2, hg, dk, dv), lambda s, h: (s, 0, 0, h, 0, 0))


def _dwconv_to(dst_ref, src_ref, pad_scr, w_ref, b_ref, t, post):
    c = src_ref.shape[1]
    pad_scr[pl.ds(0, SUBLANE), :] = jnp.zeros((SUBLANE, c), F32)
    pad_scr[pl.ds(t + SUBLANE, SUBLANE), :] = jnp.zeros((SUBLANE, c), F32)
    pad_scr[pl.ds(SUBLANE, t), :] = src_ref[...]
    tile = min(t, 256)
    for r0 in range(0, t, tile):
        y = b_ref[...]
        for k in range(CONV_W):
            y = y + w_ref[k:k + 1, :] * pad_scr[pl.ds(r0 + SUBLANE - CONV_W // 2 + k, tile), :]
        dst_ref[pl.ds(r0, tile), :] = post(y).astype(dst_ref.dtype)


def _tri(reverse, n):
    row = lax.broadcasted_iota(jnp.int32, (n, n), 0)
    col = lax.broadcasted_iota(jnp.int32, (n, n), 1)
    return jnp.where((col >= row) if reverse else (col <= row), 1.0, 0.0).astype(F32)


def _scan_block(a, b, rowid, reverse):
    for s in (1, 2, 4):
        if reverse:
            valid = rowid < SUBLANE - s
            shift = SUBLANE - s
        else:
            valid = rowid >= s
            shift = s
        a_sh = jnp.where(valid, pltpu.roll(a, shift, 0), 1.0)
        b_sh = jnp.where(valid, pltpu.roll(b, shift, 0), 0.0)
        b = b + a * b_sh
        a = a * a_sh
    return a, b


def _lru_kernel(*refs, t, cb, has_init):
    it = iter(refs)
    x_ref, y_ref, cw_ref, cbias_ref, wa_ref, ba_ref, wi_ref, bi_ref, lam_ref = (next(it) for _ in range(9))
    h0_ref = next(it) if has_init else None
    out_ref, sout_ref, pad_scr, xc_scr, a_scr, b_scr = (next(it) for _ in range(6))
    nb = cb // BS_A

    _dwconv_to(xc_scr, x_ref, pad_scr, cw_ref, cbias_ref, t, lambda v: v)

    lam = lam_ref[...]
    sp = jnp.maximum(-lam, 0.0) + jnp.log1p(jnp.exp(-jnp.abs(lam)))
    tile = min(t, 256)

    def gates(i, carry):
        r = _rows(i, tile)
        xc = xc_scr[r, :]
        xcb = xc.astype(BF16)
        for d in range(2):
            ra = jnp.concatenate([jnp.dot(xcb[:, n * BS_A:(n + 1) * BS_A], wa_ref[d, n], preferred_element_type=F32)
                                  for n in range(nb)], axis=1) + ba_ref[d:d + 1, :]
            ia = jnp.concatenate([jnp.dot(xcb[:, n * BS_A:(n + 1) * BS_A], wi_ref[d, n], preferred_element_type=F32)
                                  for n in range(nb)], axis=1) + bi_ref[d:d + 1, :]
            log_a = -LRU_C * _sigmoid(ra) * sp[d:d + 1, :]
            th = jnp.tanh(log_a)
            mult = jnp.sqrt(-2.0 * th / (1.0 - th))
            a_scr[d, r, :] = jnp.exp(log_a)
            b_scr[d, r, :] = mult * _sigmoid(ia) * xc
        return carry

    lax.fori_loop(0, t // tile, gates, 0)

    rowid = lax.broadcasted_iota(jnp.int32, (SUBLANE, cb), 0)
    nblk = t // SUBLANE

    def scan(i, carry):
        hf, hb = carry
        rf = _rows(i, SUBLANE)
        rb = _rows(nblk - 1 - i, SUBLANE)
        af, bf = _scan_block(a_scr[0, rf, :], b_scr[0, rf, :], rowid, False)
        ab, bb = _scan_block(a_scr[1, rb, :], b_scr[1, rb, :], rowid, True)
        hf_blk = bf + af * hf
        hb_blk = bb + ab * hb
        a_scr[0, rf, :] = hf_blk
        a_scr[1, rb, :] = hb_blk
        return hf_blk[SUBLANE - 1:SUBLANE, :], hb_blk[0:1, :]

    if has_init:
        init = (h0_ref[0:1, :], h0_ref[1:2, :])
    else:
        init = (jnp.zeros((1, cb), F32), jnp.zeros((1, cb), F32))
    hf, hb = lax.fori_loop(0, nblk, scan, init, unroll=2)
    sout_ref[0:1, :] = hf
    sout_ref[1:2, :] = hb

    def fin(i, carry):
        r = _rows(i, tile)
        out_ref[r, :] = ((a_scr[0, r, :] + a_scr[1, r, :]) * jax.nn.gelu(y_ref[r, :])).astype(BF16)
        return carry

    lax.fori_loop(0, t // tile, fin, 0)


def _lru(z, base, nseq, t, h0, conv_w, conv_b, w_a, b_a, w_i, b_i, lam, cb=512):
    nj = D_A // cb
    nb = cb // BS_A
    has_init = h0 is not None
    in_specs = [
        pl.BlockSpec((t, cb), lambda s, j: (base + s, j)),
        pl.BlockSpec((t, cb), lambda s, j: (base + s, nj + j)),
        pl.BlockSpec((CONV_W, cb), lambda s, j: (0, j)),
        pl.BlockSpec((1, cb), lambda s, j: (0, j)),
        pl.BlockSpec((2, nb, BS_A, BS_A), lambda s, j: (0, j, 0, 0)),
        pl.BlockSpec((2, cb), lambda s, j: (0, j)),
        pl.BlockSpec((2, nb, BS_A, BS_A), lambda s, j: (0, j, 0, 0)),
        pl.BlockSpec((2, cb), lambda s, j: (0, j)),
        pl.BlockSpec((2, cb), lambda s, j: (0, j)),
    ]
    args = [z, z, conv_w, conv_b.reshape(1, D_A), w_a.astype(BF16), b_a, w_i.astype(BF16), b_i, lam]
    if has_init:
        in_specs.append(pl.BlockSpec((None, 2, cb), lambda s, j: (s, 0, j)))
        args.append(h0)
    return pl.pallas_call(
        partial(_lru_kernel, t=t, cb=cb, has_init=has_init),
        grid=(nseq, nj),
        in_specs=in_specs,
        out_specs=[
            pl.BlockSpec((t, cb), lambda s, j: (s, j)),
            pl.BlockSpec((None, 2, cb), lambda s, j: (s, 0, j)),
        ],
        out_shape=[jax.ShapeDtypeStruct((nseq * t, D_A), BF16), jax.ShapeDtypeStruct((nseq, 2, D_A), F32)],
        scratch_shapes=[
            pltpu.VMEM((t + 2 * SUBLANE, cb), F32),
            pltpu.VMEM((t, cb), F32),
            pltpu.VMEM((2, t, cb), F32),
            pltpu.VMEM((2, t, cb), F32),
        ],
        compiler_params=_params(("arbitrary", "arbitrary")),
        name="rglru",
    )(*args)


def _mlstm_kernel(*refs, t, hg, has_init, has_prev, fill):
    it = iter(refs)
    q_ref, k_ref, v_ref, og_ref, zt_ref, bt_ref, cwq_ref, cbq_ref, cwk_ref, cbk_ref = (next(it) for _ in range(10))
    if has_init:
        c0_ref, n0_ref, m0_ref = next(it), next(it), next(it)
    if has_prev:
        next(it)
    mix_ref, cout_ref, nout_ref, mout_ref = (next(it) for _ in range(4))
    pad_scr, q_scr, k_scr, gate_scr, acc_scr, c_scr, n_scr = (next(it) for _ in range(7))
    CHUNK = MLSTM_CHUNK
    nchunk = t // CHUNK

    _dwconv_to(q_scr, q_ref, pad_scr, cwq_ref, cbq_ref, t, lambda v: _silu(v) * (DK_B ** -0.5))
    _dwconv_to(k_scr, k_ref, pad_scr, cwk_ref, cbk_ref, t, _silu)

    tile = min(t, 256)
    lane_t = lax.broadcasted_iota(jnp.int32, (tile, hg * LANE), 1) % LANE

    def gates(i, carry):
        r = _rows(i, tile)
        x = zt_ref[r, :] + bt_ref[...]
        logsig = jnp.minimum(x, 0.0) - jnp.log1p(jnp.exp(-jnp.abs(x)))
        gate_scr[r, :] = jnp.where(lane_t < 2, x, logsig)
        acc_scr[r, :] = jnp.zeros((tile, hg * DV_B), F32)
        return carry

    lax.fori_loop(0, t // tile, gates, 0)

    if has_init:
        c_scr[...] = c0_ref[...]
        n_scr[...] = n0_ref[...]
        m_init = tuple(m0_ref[d, u, :, 0:1] for u in range(hg) for d in range(2))
    else:
        c_scr[...] = jnp.zeros_like(c_scr)
        n_scr[...] = jnp.zeros_like(n_scr)
        m_init = tuple(jnp.zeros((1, 1), F32) for _ in range(2 * hg))

    row = lax.broadcasted_iota(jnp.int32, (CHUNK, CHUNK), 0)
    col = lax.broadcasted_iota(jnp.int32, (CHUNK, CHUNK), 1)
    lane_c = lax.broadcasted_iota(jnp.int32, (CHUNK, LANE), 1)

    def chunk(cidx, u, d, m):
        reverse = d == 1
        li, lf = d, 2 + d
        last = 0 if reverse else CHUNK - 1
        r = _rows(cidx, CHUNK)
        ck = slice(u * DK_B, (u + 1) * DK_B)
        cv = slice(u * DV_B, (u + 1) * DV_B)
        gt = gate_scr[r, u * LANE:(u + 1) * LANE]
        fsum = jnp.dot(_tri(reverse, CHUNK), gt, precision=HIGHEST, preferred_element_type=F32)
        y = jnp.where(lane_c == lf, fsum, gt)
        yt = y.T
        f_col, i_col = y[:, lf:lf + 1], y[:, li:li + 1]
        f_row, i_row = yt[lf:lf + 1, :], yt[li:li + 1, :]
        b = f_col + m
        mask = (col >= row) if reverse else (col <= row)
        dlog = jnp.where(mask, f_col - f_row + i_row, NEG_INF)
        m_t = jnp.maximum(b, jnp.max(dlog, axis=-1, keepdims=True))
        w = jnp.exp(dlog - m_t)
        inter = jnp.exp(b - m_t)
        q = q_scr[r, ck]
        k32 = k_scr[r, ck]
        v = v_ref[r, cv].astype(BF16)
        s = lax.dot_general(q, k32.astype(BF16), NT, preferred_element_type=F32) * w
        c_old = c_scr[d, u]
        n_old = n_scr[d, u]
        num = jnp.dot(s.astype(BF16), v, preferred_element_type=F32)
        num = num + inter * jnp.dot(q, c_old.astype(BF16), preferred_element_type=F32)
        den = jnp.sum(s, axis=-1, keepdims=True) + inter * jnp.sum(q.astype(F32) * n_old, axis=-1, keepdims=True)
        acc_scr[r, cv] += num / jnp.maximum(jnp.abs(den), jnp.exp(-m_t))
        m_new = m_t[last:last + 1, :]
        f_last = f_col[last:last + 1, :]
        kw = jnp.exp(f_last - f_col + i_col - m_new)
        decay = jnp.exp(f_last + m - m_new)
        kk = k32 * kw
        c_scr[d, u] = decay * c_old + lax.dot_general(kk.astype(BF16), v, TN, preferred_element_type=F32)
        n_scr[d, u] = decay * n_old + jnp.sum(kk, axis=0, keepdims=True)
        return m_new

    def body(c, carry):
        out = []
        for u in range(hg):
            out.append(chunk(c, u, 0, carry[2 * u]))
            out.append(chunk(nchunk - 1 - c, u, 1, carry[2 * u + 1]))
        return tuple(out)

    m_fin = lax.fori_loop(0, nchunk, body, m_init)

    _finalize_heads(acc_scr, og_ref, mix_ref, t, hg, DV_B)
    _state_target(cout_ref, fill)[...] = c_scr[...]
    nout_ref[...] = n_scr[...]
    for u in range(hg):
        for d in range(2):
            mout_ref[d, u] = jnp.broadcast_to(m_fin[2 * u + d], (1, LANE))


def _mlstm(z, zt, bias_t, base, nseq, t, init, conv_w, conv_b, hg, stack=None):
    has_init = init is not None
    prev, slot, nslots = stack if stack is not None else (None, 0, 1)
    fill = (slot, nslots) if stack is not None and prev is None else None
    kw, vw = hg * DK_B, hg * DV_B
    qoff = 2 * D_A // kw
    koff = qoff + H_B // hg
    voff = koff + H_B // hg
    goff = voff + H_B // hg
    cb2 = conv_b.reshape(1, 2 * H_B * DK_B)
    in_specs = [
        pl.BlockSpec((t, kw), lambda s, h: (base + s, qoff + h)),
        pl.BlockSpec((t, kw), lambda s, h: (base + s, koff + h)),
        pl.BlockSpec((t, vw), lambda s, h: (base + s, voff + h)),
        pl.BlockSpec((t, vw), lambda s, h: (base + s, goff + h)),
        pl.BlockSpec((t, hg * LANE), lambda s, h: (base + s, h)),
        pl.BlockSpec((1, hg * LANE), lambda s, h: (0, h)),
        pl.BlockSpec((CONV_W, kw), lambda s, h: (0, h)),
        pl.BlockSpec((1, kw), lambda s, h: (0, h)),
        pl.BlockSpec((CONV_W, kw), lambda s, h: (0, H_B // hg + h)),
        pl.BlockSpec((1, kw), lambda s, h: (0, H_B // hg + h)),
    ]
    args = [z, z, z, z, zt, bias_t, conv_w, cb2, conv_w, cb2]
    if has_init:
        c0, n0, m0 = init
        in_specs += [
            pl.BlockSpec((None, 2, hg, DK_B, DV_B), lambda s, h: (s, 0, h, 0, 0)),
            pl.BlockSpec((None, 2, hg, 1, DK_B), lambda s, h: (s, 0, h, 0, 0)),
            pl.BlockSpec((None, 2, hg, 1, LANE), lambda s, h: (s, 0, h, 0, 0)),
        ]
        args += [c0, n0.reshape(nseq, 2, H_B, 1, DK_B),
                 jnp.broadcast_to(m0[..., None, None], (nseq, 2, H_B, 1, LANE))]
    aliases = {}
    if prev is not None:
        in_specs.append(pl.BlockSpec(memory_space=pl.ANY))
        args.append(prev)
        aliases = {len(args) - 1: 1}
    out = pl.pallas_call(
        partial(_mlstm_kernel, t=t, hg=hg, has_init=has_init, has_prev=prev is not None, fill=fill),
        grid=(nseq, H_B // hg),
        in_specs=in_specs,
        out_specs=[
            pl.BlockSpec((t, vw), lambda s, h: (s, h)),
            _state_spec(fill, slot, hg, DK_B, DV_B),
            pl.BlockSpec((None, 2, hg, 1, DK_B), lambda s, h: (s, 0, h, 0, 0)),
            pl.BlockSpec((None, 2, hg, 1, LANE), lambda s, h: (s, 0, h, 0, 0)),
        ],
        out_shape=[
            jax.ShapeDtypeStruct((nseq * t, H_B * DV_B), BF16),
            jax.ShapeDtypeStruct((nseq, nslots, 2, H_B, DK_B, DV_B), F32),
            jax.ShapeDtypeStruct((nseq, 2, H_B, 1, DK_B), F32),
            jax.ShapeDtypeStruct((nseq, 2, H_B, 1, LANE), F32),
        ],
        scratch_shapes=[
            pltpu.VMEM((t + 2 * SUBLANE, kw), F32),
            pltpu.VMEM((t, kw), BF16),
            pltpu.VMEM((t, kw), F32),
            pltpu.VMEM((t, hg * LANE), F32),
            pltpu.VMEM((t, vw), F32),
            pltpu.VMEM((2, hg, DK_B, DV_B), F32),
            pltpu.VMEM((2, hg, 1, DK_B), F32),
        ],
        input_output_aliases=aliases,
        compiler_params=_params(("arbitrary", "arbitrary")),
        name="mlstm",
    )(*args)
    return out if stack is not None else (out[0], out[1][:, 0], out[2], out[3])


def _ret_kernel(*refs, t, hg, use_rope, has_init, has_prev, fill):
    it = iter(refs)
    lg_ref, q_ref, k_ref, v_ref, g_ref = (next(it) for _ in range(5))
    if use_rope:
        cos_ref, sin_ref = next(it), next(it)
    s0_ref = next(it) if has_init else None
    if has_prev:
        next(it)
    mix_ref, sout_ref, acc_scr, st_scr = (next(it) for _ in range(4))
    CHUNK = RET_CHUNK
    nchunk = t // CHUNK
    h0 = pl.program_id(1) * hg

    row = lax.broadcasted_iota(jnp.int32, (CHUNK, CHUNK), 0)
    col = lax.broadcasted_iota(jnp.int32, (CHUNK, CHUNK), 1)
    dist = (row - col).astype(F32)
    pos = lax.broadcasted_iota(jnp.int32, (CHUNK, DK_C), 0).astype(F32)
    consts = {}
    for u in range(hg):
        for d in range(2):
            lg = lg_ref[d, h0 + u]
            if d == 0:
                dm = jnp.where(dist >= 0, jnp.exp(jnp.maximum(dist, 0.0) * lg), 0.0)
                qin = jnp.exp((pos + 1.0) * lg)
                kout = jnp.exp((CHUNK - 1.0 - pos) * lg)
            else:
                dm = jnp.where(dist <= 0, jnp.exp(jnp.maximum(-dist, 0.0) * lg), 0.0)
                qin = jnp.exp((CHUNK - pos) * lg)
                kout = jnp.exp(pos * lg)
            gch = jnp.exp(jnp.full((1, DV_C), float(CHUNK), F32) * lg)
            consts[u, d] = (dm, qin, kout, gch)

    acc_scr[...] = jnp.zeros_like(acc_scr)
    if has_init:
        st_scr[...] = s0_ref[...]
    else:
        st_scr[...] = jnp.zeros_like(st_scr)

    def chunk(cidx, u, d):
        dm, qin, kout, gch = consts[u, d]
        r = _rows(cidx, CHUNK)
        ck = slice(u * DK_C, (u + 1) * DK_C)
        cv = slice(u * DV_C, (u + 1) * DV_C)
        q = q_ref[r, ck]
        k = k_ref[r, ck] * (DK_C ** -0.5)
        if use_rope:
            cs, sn = cos_ref[r, :], sin_ref[r, :]
            q = q * cs + pltpu.roll(q, DK_C // 2, 1) * sn
            k = k * cs + pltpu.roll(k, DK_C // 2, 1) * sn
        v = v_ref[r, cv].astype(BF16)
        st = st_scr[d, u]
        s = lax.dot_general(q.astype(BF16), k.astype(BF16), NT, preferred_element_type=F32) * dm
        o = jnp.dot(s.astype(BF16), v, preferred_element_type=F32)
        o = o + jnp.dot((q * qin).astype(BF16), st.astype(BF16), preferred_element_type=F32)
        st_scr[d, u] = gch * st + lax.dot_general((k * kout).astype(BF16), v, TN, preferred_element_type=F32)
        acc_scr[r, cv] += o

    def body(c, carry):
        for u in range(hg):
            chunk(c, u, 0)
            chunk(nchunk - 1 - c, u, 1)
        return carry

    lax.fori_loop(0, nchunk, body, 0)
    _finalize_heads(acc_scr, g_ref, mix_ref, t, hg, DV_C)
    _state_target(sout_ref, fill)[...] = st_scr[...]


def _retention(z, base, nseq, t, s0, lg, rope, hg, stack=None):
    has_init = s0 is not None
    prev, slot, nslots = stack if stack is not None else (None, 0, 1)
    fill = (slot, nslots) if stack is not None and prev is None else None
    use_rope = rope is not None
    kw, vw = hg * DK_C, hg * DV_C
    qoff, koff = 0, H_C * DK_C // kw
    voff = 2 * H_C * DK_C // vw
    goff = voff + H_C // hg
    in_specs = [
        pl.BlockSpec(memory_space=pltpu.SMEM),
        pl.BlockSpec((t, kw), lambda s, h: (base + s, qoff + h)),
        pl.BlockSpec((t, kw), lambda s, h: (base + s, koff + h)),
        pl.BlockSpec((t, vw), lambda s, h: (base + s, voff + h)),
        pl.BlockSpec((t, vw), lambda s, h: (base + s, goff + h)),
    ]
    args = [lg, z, z, z, z]
    if use_rope:
        in_specs += [pl.BlockSpec((t, DK_C), lambda s, h: (0, 0)), pl.BlockSpec((t, DK_C), lambda s, h: (0, 0))]
        args += list(rope)
    if has_init:
        in_specs.append(pl.BlockSpec((None, 2, hg, DK_C, DV_C), lambda s, h: (s, 0, h, 0, 0)))
        args.append(s0)
    aliases = {}
    if prev is not None:
        in_specs.append(pl.BlockSpec(memory_space=pl.ANY))
        args.append(prev)
        aliases = {len(args) - 1: 1}
    out = pl.pallas_call(
        partial(_ret_kernel, t=t, hg=hg, use_rope=use_rope, has_init=has_init, has_prev=prev is not None, fill=fill),
        grid=(nseq, H_C // hg),
        in_specs=in_specs,
        out_specs=[
            pl.BlockSpec((t, vw), lambda s, h: (s, h)),
            _state_spec(fill, slot, hg, DK_C, DV_C),
        ],
        out_shape=[
            jax.ShapeDtypeStruct((nseq * t, H_C * DV_C), BF16),
            jax.ShapeDtypeStruct((nseq, nslots, 2, H_C, DK_C, DV_C), F32),
        ],
        scratch_shapes=[pltpu.VMEM((t, vw), F32), pltpu.VMEM((2, hg, DK_C, DV_C), F32)],
        input_output_aliases=aliases,
        compiler_params=_params(("arbitrary", "arbitrary")),
        name="retention",
    )(*args)
    return out if stack is not None else (out[0], out[1][:, 0])


def _gla_intra(q, k, g2, reverse):
    CHUNK = GLA_CHUNK
    nsub = CHUNK // SUB
    lane = lax.broadcasted_iota(jnp.int32, (SUB, CHUNK), 1)
    rowi = lax.broadcasted_iota(jnp.int32, (SUB, CHUNK), 0)
    out = []
    for jb in range(nsub):
        lo, hi = jb * SUB, (jb + 1) * SUB
        qj, gj = q[lo:hi], g2[lo:hi]
        if reverse and jb < nsub - 1:
            gref = g2[hi:hi + 1]
            kt = (k[hi:] * jnp.exp2(gref - g2[hi:])).astype(BF16)
            kt = jnp.concatenate([jnp.zeros((hi, DK_D), BF16), kt], axis=0)
        elif not reverse and jb > 0:
            gref = g2[lo - 1:lo]
            kt = (k[:lo] * jnp.exp2(gref - g2[:lo])).astype(BF16)
            kt = jnp.concatenate([kt, jnp.zeros((CHUNK - lo, DK_D), BF16)], axis=0)
        else:
            kt = None
        if kt is None:
            off = jnp.zeros((SUB, CHUNK), F32)
        else:
            qt = (qj * jnp.exp2(gj - gref)).astype(BF16)
            off = lax.dot_general(qt, kt, NT, preferred_element_type=F32)
        diag = jnp.zeros((SUB, CHUNK), F32)
        for i in range(lo, hi):
            col = jnp.sum(qj * jnp.exp2(gj - g2[i:i + 1]) * k[i:i + 1], axis=-1, keepdims=True)
            diag = jnp.where(lane == i, col, diag)
        mask = (rowi + lo <= lane) if reverse else (rowi + lo >= lane)
        out.append(off + jnp.where(mask, diag, 0.0))
    return jnp.concatenate(out, axis=0)


def _gla_kernel(*refs, t, hg, has_init, has_prev, fill):
    it = iter(refs)
    q_ref, k_ref, v_ref, r_ref, a_ref, wup_ref, gb_ref = (next(it) for _ in range(7))
    s0_ref = next(it) if has_init else None
    if has_prev:
        next(it)
    mix_ref, sout_ref, acc_scr, st_scr, g_scr = (next(it) for _ in range(5))
    CHUNK = GLA_CHUNK
    nchunk = t // CHUNK
    tile = min(t, 256)

    def gates(i, carry):
        r = _rows(i, tile)
        a = a_ref[r, :].astype(BF16)
        for u in range(hg):
            for d in range(2):
                x = jnp.dot(a, wup_ref[d, u], preferred_element_type=F32) + gb_ref[d, u]
                g_scr[d, u, r, :] = (jnp.minimum(x, 0.0) - jnp.log1p(jnp.exp(-jnp.abs(x)))) * (1.0 / GLA_TAU)
        acc_scr[r, :] = jnp.zeros((tile, hg * DV_D), F32)
        return carry

    lax.fori_loop(0, t // tile, gates, 0)

    for u in range(hg):
        for d in range(2):
            st_scr[d, u] = s0_ref[d, u].T if has_init else jnp.zeros((DV_D, DK_D), F32)

    def chunk(cidx, u, d):
        reverse = d == 1
        last = 0 if reverse else CHUNK - 1
        r = _rows(cidx, CHUNK)
        ck = slice(u * DK_D, (u + 1) * DK_D)
        cv = slice(u * DV_D, (u + 1) * DV_D)
        g2 = jnp.dot(_tri(reverse, CHUNK), g_scr[d, u, r, :], precision=HIGHEST, preferred_element_type=F32) * LOG2E
        g2_last = g2[last:last + 1, :]
        q = q_ref[r, ck] * (DK_D ** -0.5)
        k = k_ref[r, ck]
        v = v_ref[r, cv].astype(BF16)
        st = st_scr[d, u]
        s = _gla_intra(q, k, g2, reverse)
        o = jnp.dot(s.astype(BF16), v, preferred_element_type=F32)
        o = o + lax.dot_general((q * jnp.exp2(g2)).astype(BF16), st.astype(BF16), NT, preferred_element_type=F32)
        kd = (k * jnp.exp2(g2_last - g2)).astype(BF16)
        st_scr[d, u] = jnp.exp2(g2_last) * st + lax.dot_general(v, kd, TN, preferred_element_type=F32)
        acc_scr[r, cv] += o

    def body(c, carry):
        for u in range(hg):
            chunk(c, u, 0)
            chunk(nchunk - 1 - c, u, 1)
        return carry

    lax.fori_loop(0, nchunk, body, 0)
    _finalize_heads(acc_scr, r_ref, mix_ref, t, hg, DV_D)
    state_out = _state_target(sout_ref, fill)
    for u in range(hg):
        for d in range(2):
            state_out[d, u] = st_scr[d, u].T


def _gla(z, zt, base, nseq, t, s0, w_up, g_bias, hg, stack=None):
    has_init = s0 is not None
    prev, slot, nslots = stack if stack is not None else (None, 0, 1)
    fill = (slot, nslots) if stack is not None and prev is None else None
    kw, vw = hg * DK_D, hg * DV_D
    cbase = 2 * H_C * DK_C + 2 * H_C * DV_C
    qoff = cbase // kw
    koff = qoff + H_D // hg
    voff = (cbase + 2 * H_D * DK_D) // vw
    roff = voff + H_D // hg
    in_specs = [
        pl.BlockSpec((t, kw), lambda s, h: (base + s, qoff + h)),
        pl.BlockSpec((t, kw), lambda s, h: (base + s, koff + h)),
        pl.BlockSpec((t, vw), lambda s, h: (base + s, voff + h)),
        pl.BlockSpec((t, vw), lambda s, h: (base + s, roff + h)),
        pl.BlockSpec((t, LANE), lambda s, h: (base + s, 0)),
        pl.BlockSpec((2, hg, LANE, DK_D), lambda s, h: (0, h, 0, 0)),
        pl.BlockSpec((2, hg, 1, DK_D), lambda s, h: (0, h, 0, 0)),
    ]
    args = [z, z, z, z, zt, w_up, g_bias]
    if has_init:
        in_specs.append(pl.BlockSpec((None, 2, hg, DK_D, DV_D), lambda s, h: (s, 0, h, 0, 0)))
        args.append(s0)
    aliases = {}
    if prev is not None:
        in_specs.append(pl.BlockSpec(memory_space=pl.ANY))
        args.append(prev)
        aliases = {len(args) - 1: 1}
    out = pl.pallas_call(
        partial(_gla_kernel, t=t, hg=hg, has_init=has_init, has_prev=prev is not None, fill=fill),
        grid=(nseq, H_D // hg),
        in_specs=in_specs,
        out_specs=[
            pl.BlockSpec((t, vw), lambda s, h: (s, h)),
            _state_spec(fill, slot, hg, DK_D, DV_D),
        ],
        out_shape=[
            jax.ShapeDtypeStruct((nseq * t, H_D * DV_D), BF16),
            jax.ShapeDtypeStruct((nseq, nslots, 2, H_D, DK_D, DV_D), F32),
        ],
        scratch_shapes=[
            pltpu.VMEM((t, vw), F32),
            pltpu.VMEM((2, hg, DV_D, DK_D), F32),
            pltpu.VMEM((2, hg, t, DK_D), F32),
        ],
        input_output_aliases=aliases,
        compiler_params=_params(("arbitrary", "arbitrary")),
        name="gla",
    )(*args)
    return out if stack is not None else (out[0], out[1][:, 0])


def _rope_tables(t):
    pos = jnp.arange(t)
    row = (pos // GRID_W).astype(F32)
    col = (pos % GRID_W).astype(F32)
    nf = DK_C // 4
    freqs = ROPE_BASE ** (-jnp.arange(nf, dtype=F32) / nf)
    ang = jnp.concatenate([row[:, None] * freqs, col[:, None] * freqs], axis=-1)
    cos, sin = jnp.cos(ang), jnp.sin(ang)
    return jnp.concatenate([cos, cos], axis=-1), jnp.concatenate([-sin, sin], axis=-1)


def _ab_tail(w_in, i_bias, f_bias):
    n_main = w_in.shape[1] - 4 * H_B
    gates = w_in[:, n_main:].reshape(D_MODEL, 4, H_B)
    cols = jnp.swapaxes(gates, 1, 2)
    w_tail = jnp.pad(cols, ((0, 0), (0, 0), (0, LANE - 4))).reshape(D_MODEL, H_B * LANE)
    bias = jnp.concatenate([i_bias, f_bias], axis=0).T
    bias_t = jnp.pad(bias, ((0, 0), (0, LANE - 4))).reshape(1, H_B * LANE)
    return n_main, w_tail, bias_t


def _cd_tail(w_in, w_up, g_bias):
    n_main = w_in.shape[1] - 2 * GLA_RANK
    w_tail = jnp.pad(w_in[:, n_main:], ((0, 0), (0, LANE - 2 * GLA_RANK)))
    up = w_up.reshape(2, GLA_RANK, H_D, DK_D).transpose(0, 2, 1, 3)
    up = jnp.stack([jnp.pad(up[d], ((0, 0), (d * GLA_RANK, LANE - (d + 1) * GLA_RANK), (0, 0))) for d in range(2)])
    return n_main, w_tail, up.astype(BF16), g_bias.reshape(2, H_D, 1, DK_D)


def kernel(x_prompt, x_sample, c, state_lru, state_mlstm_C, state_mlstm_n, state_mlstm_m, state_ret, state_gla, c_ctx, w_mod, b_mod, norm_w, ffn_w_gate, ffn_w_up, ffn_w_down, w_in_ab, w_out_ab, lru_conv_w, lru_conv_b, lru_w_a, lru_b_a, lru_w_i, lru_b_i, lru_lambda, mlstm_conv_w, mlstm_conv_b, mlstm_i_bias, mlstm_f_bias, w_in_cd, w_out_cd, ret_decay_log, gla_w_up, gla_b, final_norm_w):
    Bp, Tp, D = x_prompt.shape
    Bs, Ts, _ = x_sample.shape
    n_ctx = Bp * Tp
    tm_ffn, tf_ffn, tm_in, tm_out = 1024, 256, 1024, 512
    assert all(n_ctx % t == 0 and Ts % t == 0 for t in (tm_ffn, tm_in, tm_out)) and 1 + Bs <= MOD_ROWS and n_ctx % Ts == 0
    row_ffn, row_in, row_out = (_mod_row_map(n_ctx // t, Ts // t) for t in (tm_ffn, tm_in, tm_out))
    groups = ((0, Bp, Tp), (n_ctx // Ts, Bs, Ts))
    hg_ctx = {"mlstm": 4, "ret": 4, "gla": 4}
    hg_lat = {"mlstm": 1, "ret": 2, "gla": 2}

    cond = jnp.concatenate([c_ctx[None], c, jnp.zeros((MOD_ROWS - 1 - Bs, D), F32)], axis=0)
    mod = _modulation(cond, w_mod, b_mod).reshape(DEPTH, MOD_ROWS, N_MOD, D)
    nw = norm_w.reshape(DEPTH, 3, 1, D)
    wg, wu, wd = ffn_w_gate, ffn_w_up, ffn_w_down
    w_in_ab_b, w_in_cd_b = w_in_ab.astype(BF16), w_in_cd.astype(BF16)
    w_out_ab_b, w_out_cd_b = w_out_ab.astype(BF16), w_out_cd.astype(BF16)

    x = jnp.concatenate([x_prompt.reshape(n_ctx, D), x_sample.reshape(Bs * Ts, D)], axis=0)
    rope = _rope_tables(Ts)

    st_lru, st_n, st_m = [], [], []
    new_C = new_ret = new_gla = None
    for l in range(DEPTH):
        x = _ffn(x, mod, nw, wg, wu, wd, l, 0, row_ffn, tm_ffn, tf_ffn)
        (pb, pn, pt), (sb, sn, st) = groups
        if l % 2 == 0:
            e = l // 2
            n_main, w_tail, bias_t = _ab_tail(w_in_ab[e], mlstm_i_bias[e], mlstm_f_bias[e])
            z, zt = _inproj(x, mod, nw, w_in_ab_b, l, e, n_main, w_tail.astype(BF16), row_in, tm_in)
            lru_args = (lru_conv_w[e], lru_conv_b[e], lru_w_a[e], lru_b_a[e], lru_w_i[e], lru_b_i[e], lru_lambda[e])
            a_p, s_lru = _lru(z, pb, pn, pt, None, *lru_args)
            a_s, _ = _lru(z, sb, sn, st, state_lru[:, e], *lru_args)
            b_p, new_C, s_n, s_m = _mlstm(z, zt, bias_t, pb, pn, pt, None, mlstm_conv_w[e], mlstm_conv_b[e], hg_ctx["mlstm"],
                                          stack=(new_C, e, N_EVEN))
            b_s, _, _, _ = _mlstm(z, zt, bias_t, sb, sn, st,
                                  (state_mlstm_C[:, e], state_mlstm_n[:, e], state_mlstm_m[:, e]),
                                  mlstm_conv_w[e], mlstm_conv_b[e], hg_lat["mlstm"])
            st_lru.append(s_lru)
            st_n.append(s_n.reshape(Bp, 2, H_B, DK_B))
            st_m.append(s_m[:, :, :, 0, 0])
            mix_ctx, mix_lat = (a_p, b_p), (a_s, b_s)
            w_out = w_out_ab_b
        else:
            e = l // 2
            n_main, w_tail, w_up, g_bias = _cd_tail(w_in_cd[e], gla_w_up[e], gla_b[e])
            z, zt = _inproj(x, mod, nw, w_in_cd_b, l, e, n_main, w_tail.astype(BF16), row_in, tm_in)
            c_p, new_ret = _retention(z, pb, pn, pt, None, ret_decay_log[e], None, hg_ctx["ret"], stack=(new_ret, e, N_ODD))
            c_s, _ = _retention(z, sb, sn, st, state_ret[:, e], ret_decay_log[e], rope, hg_lat["ret"])
            d_p, new_gla = _gla(z, zt, pb, pn, pt, None, w_up, g_bias, hg_ctx["gla"], stack=(new_gla, e, N_ODD))
            d_s, _ = _gla(z, zt, sb, sn, st, state_gla[:, e], w_up, g_bias, hg_lat["gla"])
            mix_ctx, mix_lat = (c_p, d_p), (c_s, d_s)
            w_out = w_out_cd_b
        x = _outproj(x, mix_ctx, mix_lat, mod, w_out, l, e, row_out, tm_out)
        x = _ffn(x, mod, nw, wg, wu, wd, l, 1, row_ffn, tm_ffn, tf_ffn)

    y_prompt = _final_norm(x, final_norm_w, 0, n_ctx).reshape(Bp, Tp, D)
    y_sample = _final_norm(x, final_norm_w, n_ctx, Bs * Ts).reshape(Bs, Ts, D)
    return (y_prompt, y_sample,
            jnp.stack(st_lru, axis=1), new_C, jnp.stack(st_n, axis=1), jnp.stack(st_m, axis=1), new_ret, new_gla)
```

```python
from functools import partial

import jax
import jax.numpy as jnp
from jax import lax
from jax.experimental import pallas as pl
from jax.experimental.pallas import tpu as pltpu

F32 = jnp.float32
BF16 = jnp.bfloat16

D_MODEL = 2048
DEPTH = 4
N_MOD = 9
N_EVEN = (DEPTH + 1) // 2
N_ODD = DEPTH // 2
D_FF = 5632
EPS = 1e-6
CHUNK = 64
CONV_W = 4
D_A = D_MODEL // 2
NB_A = 8
BS_A = D_A // NB_A
LRU_C = 8.0
H_B = 4
DK_B = D_MODEL // 8
DV_B = D_MODEL // 8
H_C = 4
DK_C = D_MODEL // 16
DV_C = D_MODEL // 8
H_D = 4
DK_D = D_MODEL // 16
DV_D = D_MODEL // 8
GLA_RANK = 16
GLA_TAU = 16.0
ROPE_BASE = 10000.0
GRID_W = 64

MOD_ROWS = 8
LANE = 128
SUBLANE = 8
SUB = 16
MLSTM_CHUNK = 256
RET_CHUNK = 256
GLA_CHUNK = 128
VMEM_LIMIT = 56 * 1024 * 1024
FFN_VMEM_LIMIT = 60 * 1024 * 1024
NEG_INF = float("-inf")
LOG2E = 1.4426950408889634
HIGHEST = lax.Precision.HIGHEST
NT = (((1,), (1,)), ((), ()))
TN = (((0,), (0,)), ((), ()))


def _params(sem, vmem_limit=VMEM_LIMIT):
    return pltpu.CompilerParams(dimension_semantics=sem, vmem_limit_bytes=vmem_limit)


def _silu(x):
    return x * jax.nn.sigmoid(x)


def _sigmoid(x):
    return 0.5 * jnp.tanh(0.5 * x) + 0.5


def _rows(i, n):
    return pl.ds(pl.multiple_of(i * n, n), n)


def _mod_kernel(c_ref, w_ref, b_ref, o_ref):
    s = _silu(c_ref[...]).astype(BF16)
    o_ref[...] = jnp.dot(s, w_ref[...].astype(BF16), preferred_element_type=F32) + b_ref[...]


def _modulation(cond, w_mod, b_mod, tn=1024):
    n = w_mod.shape[-1]
    return pl.pallas_call(
        _mod_kernel,
        grid=(DEPTH, n // tn),
        in_specs=[
            pl.BlockSpec((MOD_ROWS, D_MODEL), lambda l, j: (0, 0)),
            pl.BlockSpec((None, D_MODEL, tn), lambda l, j: (l, 0, j)),
            pl.BlockSpec((None, 1, tn), lambda l, j: (l, 0, j)),
        ],
        out_specs=pl.BlockSpec((None, MOD_ROWS, tn), lambda l, j: (l, 0, j)),
        out_shape=jax.ShapeDtypeStruct((DEPTH, MOD_ROWS, n), F32),
        compiler_params=_params(("arbitrary", "arbitrary")),
        name="modulation",
    )(cond, w_mod, b_mod.reshape(DEPTH, 1, n))


def _norm_mod(x, nw, shift, scale):
    ms = jnp.mean(x * x, axis=-1, keepdims=True)
    y = x * lax.rsqrt(ms + EPS) * nw
    return y * (1.0 + scale) + shift


def _mod_row_map(n_ctx_tiles, tiles_per_req):
    def row(i):
        return jnp.where(i < n_ctx_tiles, 0, 1 + (i - n_ctx_tiles) // tiles_per_req)
    return row


def _ffn_kernel(x_ref, mod_ref, nw_ref, wg_ref, wu_ref, wd_ref, o_ref, h_scr, *, k, nf):
    f = pl.program_id(1)

    @pl.when(f == 0)
    def _():
        h = _norm_mod(x_ref[...], nw_ref[...], mod_ref[3 * k:3 * k + 1, :], mod_ref[3 * k + 1:3 * k + 2, :])
        h_scr[...] = h.astype(BF16)

    def down_proj():
        h = h_scr[...]
        g = jnp.dot(h, wg_ref[...].astype(BF16), preferred_element_type=F32)
        u = jnp.dot(h, wu_ref[...].astype(BF16), preferred_element_type=F32)
        a = (_silu(g) * u).astype(BF16)
        return jnp.dot(a, wd_ref[...].astype(BF16), preferred_element_type=F32)

    @pl.when(f == 0)
    def _():
        o_ref[...] = down_proj()

    @pl.when(jnp.logical_and(f > 0, f < nf - 1))
    def _():
        o_ref[...] += down_proj()

    @pl.when(f == nf - 1)
    def _():
        o_ref[...] = x_ref[...] + (0.5 * mod_ref[3 * k + 2:3 * k + 3, :]) * (o_ref[...] + down_proj())


def _ffn(x, mod, nw, wg, wu, wd, l, j, row_of_tile, tm, tf):
    t = x.shape[0]
    nf = D_FF // tf
    assert nf >= 2
    k = 2 * j
    return pl.pallas_call(
        partial(_ffn_kernel, k=k, nf=nf),
        grid=(t // tm, nf),
        in_specs=[
            pl.BlockSpec((tm, D_MODEL), lambda i, f: (i, 0)),
            pl.BlockSpec((None, None, N_MOD, D_MODEL), lambda i, f: (l, row_of_tile(i), 0, 0)),
            pl.BlockSpec((None, None, 1, D_MODEL), lambda i, f: (l, k, 0, 0)),
            pl.BlockSpec((None, None, D_MODEL, tf), lambda i, f: (l, j, 0, f)),
            pl.BlockSpec((None, None, D_MODEL, tf), lambda i, f: (l, j, 0, f)),
            pl.BlockSpec((None, None, tf, D_MODEL), lambda i, f: (l, j, f, 0)),
        ],
        out_specs=pl.BlockSpec((tm, D_MODEL), lambda i, f: (i, 0)),
        out_shape=jax.ShapeDtypeStruct((t, D_MODEL), F32),
        scratch_shapes=[pltpu.VMEM((tm, D_MODEL), BF16)],
        compiler_params=_params(("arbitrary", "arbitrary"), FFN_VMEM_LIMIT),
        name="ffn",
    )(x, mod, nw, wg, wu, wd)


def _inproj_kernel(x_ref, mod_ref, nw_ref, w_ref, wt_ref, z_ref, zt_ref, h_scr):
    j = pl.program_id(1)

    @pl.when(j == 0)
    def _():
        h = _norm_mod(x_ref[...], nw_ref[...], mod_ref[3:4, :], mod_ref[4:5, :]).astype(BF16)
        h_scr[...] = h
        zt_ref[...] = jnp.dot(h, wt_ref[...], preferred_element_type=F32)
        z_ref[...] = jnp.dot(h, w_ref[...].astype(BF16), preferred_element_type=F32)

    @pl.when(j > 0)
    def _():
        z_ref[...] = jnp.dot(h_scr[...], w_ref[...].astype(BF16), preferred_element_type=F32)


def _inproj(x, mod, nw, w_in, l, e, n, w_tail, row_of_tile, tm, tn=1024):
    t = x.shape[0]
    tw = w_tail.shape[1]
    return pl.pallas_call(
        _inproj_kernel,
        grid=(t // tm, n // tn),
        in_specs=[
            pl.BlockSpec((tm, D_MODEL), lambda i, j: (i, 0)),
            pl.BlockSpec((None, None, N_MOD, D_MODEL), lambda i, j: (l, row_of_tile(i), 0, 0)),
            pl.BlockSpec((None, None, 1, D_MODEL), lambda i, j: (l, 1, 0, 0)),
            pl.BlockSpec((None, D_MODEL, tn), lambda i, j: (e, 0, j)),
            pl.BlockSpec((D_MODEL, tw), lambda i, j: (0, 0)),
        ],
        out_specs=[
            pl.BlockSpec((tm, tn), lambda i, j: (i, j)),
            pl.BlockSpec((tm, tw), lambda i, j: (i, 0)),
        ],
        out_shape=[jax.ShapeDtypeStruct((t, n), F32), jax.ShapeDtypeStruct((t, tw), F32)],
        scratch_shapes=[pltpu.VMEM((tm, D_MODEL), BF16)],
        compiler_params=_params(("arbitrary", "arbitrary")),
        name="inproj",
    )(x, mod, nw, w_in, w_tail)


def _outproj_kernel(x_ref, ap_ref, bp_ref, as_ref, bs_ref, mod_ref, wa_ref, wb_ref, o_ref, *, n_ctx_tiles):
    def emit(ma_ref, mb_ref):
        y = jnp.dot(ma_ref[...], wa_ref[...], preferred_element_type=F32)
        y = y + jnp.dot(mb_ref[...], wb_ref[...], preferred_element_type=F32)
        o_ref[...] = x_ref[...] + mod_ref[5:6, :] * y

    is_ctx = pl.program_id(0) < n_ctx_tiles
    pl.when(is_ctx)(lambda: emit(ap_ref, bp_ref))
    pl.when(jnp.logical_not(is_ctx))(lambda: emit(as_ref, bs_ref))


def _outproj(x, mix_ctx, mix_lat, mod, w_out, l, e, row_of_tile, tm):
    t = x.shape[0]
    half = D_MODEL // 2
    n_ctx_tiles = mix_ctx[0].shape[0] // tm
    n_lat_tiles = mix_lat[0].shape[0] // tm
    ctx_blk = lambda i: (jnp.minimum(i, n_ctx_tiles - 1), 0)
    lat_blk = lambda i: (jnp.clip(i - n_ctx_tiles, 0, n_lat_tiles - 1), 0)
    return pl.pallas_call(
        partial(_outproj_kernel, n_ctx_tiles=n_ctx_tiles),
        grid=(t // tm,),
        in_specs=[
            pl.BlockSpec((tm, D_MODEL), lambda i: (i, 0)),
            pl.BlockSpec((tm, half), ctx_blk),
            pl.BlockSpec((tm, half), ctx_blk),
            pl.BlockSpec((tm, half), lat_blk),
            pl.BlockSpec((tm, half), lat_blk),
            pl.BlockSpec((None, None, N_MOD, D_MODEL), lambda i: (l, row_of_tile(i), 0, 0)),
            pl.BlockSpec((None, half, D_MODEL), lambda i: (e, 0, 0)),
            pl.BlockSpec((None, half, D_MODEL), lambda i: (e, 1, 0)),
        ],
        out_specs=pl.BlockSpec((tm, D_MODEL), lambda i: (i, 0)),
        out_shape=jax.ShapeDtypeStruct((t, D_MODEL), F32),
        compiler_params=_params(("arbitrary",)),
        name="outproj",
    )(x, *mix_ctx, *mix_lat, mod, w_out, w_out)


def _final_norm_kernel(x_ref, w_ref, o_ref):
    x = x_ref[...]
    ms = jnp.mean(x * x, axis=-1, keepdims=True)
    o_ref[...] = x * lax.rsqrt(ms + EPS) * w_ref[...]


def _final_norm(x, w, row0, rows, tm=512):
    base = row0 // tm
    return pl.pallas_call(
        _final_norm_kernel,
        grid=(rows // tm,),
        in_specs=[pl.BlockSpec((tm, D_MODEL), lambda i: (base + i, 0)), pl.BlockSpec((1, D_MODEL), lambda i: (0, 0))],
        out_specs=pl.BlockSpec((tm, D_MODEL), lambda i: (i, 0)),
        out_shape=jax.ShapeDtypeStruct((rows, D_MODEL), F32),
        compiler_params=_params(("arbitrary",)),
        name="final_norm",
    )(x, w.reshape(1, D_MODEL))


def _finalize_heads(acc_scr, gate_ref, mix_ref, t, hg, dv):
    tile = min(t, 256)

    def body(i, carry):
        r = _rows(i, tile)
        for u in range(hg):
            cv = slice(u * dv, (u + 1) * dv)
            o = acc_scr[r, cv]
            d = o - jnp.mean(o, axis=-1, keepdims=True)
            var = jnp.mean(d * d, axis=-1, keepdims=True)
            mix_ref[r, cv] = (d * lax.rsqrt(var + EPS) * _silu(gate_ref[r, cv])).astype(BF16)
        return carry

    lax.fori_loop(0, t // tile, body, 0)


def _state_target(out_ref, fill):
    if fill is None:
        return out_ref
    slot, nslots = fill
    for o in range(nslots):
        if o != slot:
            out_ref[o] = jnp.zeros(out_ref.shape[1:], out_ref.dtype)
    return out_ref.at[slot]


def _state_spec(fill, slot, hg, dk, dv):
    if fill is None:
        return pl.BlockSpec((None, None, 2, hg, dk, dv), lambda s, h: (s, slot, 0, h, 0, 0))
    return pl.BlockSpec((None, fill[1], 2, hg, dk, dv), lambda s, h: (s, 0, 0, h, 0, 0))


def _dwconv_to(dst_ref, src_ref, pad_scr, w_ref, b_ref, t, post):
    c = src_ref.shape[1]
    pad_scr[pl.ds(0, SUBLANE), :] = jnp.zeros((SUBLANE, c), F32)
    pad_scr[pl.ds(t + SUBLANE, SUBLANE), :] = jnp.zeros((SUBLANE, c), F32)
    pad_scr[pl.ds(SUBLANE, t), :] = src_ref[...]
    tile = min(t, 256)
    for r0 in range(0, t, tile):
        win = pad_scr[pl.ds(r0, tile + 2 * SUBLANE), :]
        y = b_ref[...]
        for k in range(CONV_W):
            off = SUBLANE - CONV_W // 2 + k
            y = y + w_ref[k:k + 1, :] * win[off:off + tile]
        dst_ref[pl.ds(r0, tile), :] = post(y).astype(dst_ref.dtype)


def _tri(reverse, n):
    row = lax.broadcasted_iota(jnp.int32, (n, n), 0)
    col = lax.broadcasted_iota(jnp.int32, (n, n), 1)
    return jnp.where((col >= row) if reverse else (col <= row), 1.0, 0.0).astype(F32)


def _scan_block(a, b, rowid, reverse):
    for s in (1, 2, 4):
        if reverse:
            valid = rowid < SUBLANE - s
            shift = SUBLANE - s
        else:
            valid = rowid >= s
            shift = s
        a_sh = jnp.where(valid, pltpu.roll(a, shift, 0), 1.0)
        b_sh = jnp.where(valid, pltpu.roll(b, shift, 0), 0.0)
        b = b + a * b_sh
        a = a * a_sh
    return a, b


def _lru_kernel(*refs, t, cb, has_init):
    it = iter(refs)
    x_ref, y_ref, cw_ref, cbias_ref, wa_ref, ba_ref, wi_ref, bi_ref, lam_ref = (next(it) for _ in range(9))
    h0_ref = next(it) if has_init else None
    out_ref, sout_ref, pad_scr, xc_scr, a_scr, b_scr = (next(it) for _ in range(6))
    nb = cb // BS_A

    _dwconv_to(xc_scr, x_ref, pad_scr, cw_ref, cbias_ref, t, lambda v: v)

    lam = lam_ref[...]
    sp = jnp.maximum(-lam, 0.0) + jnp.log1p(jnp.exp(-jnp.abs(lam)))
    tile = min(t, 256)

    def gates(i, carry):
        r = _rows(i, tile)
        xc = xc_scr[r, :]
        xcb = xc.astype(BF16)
        for d in range(2):
            ra = jnp.concatenate([jnp.dot(xcb[:, n * BS_A:(n + 1) * BS_A], wa_ref[d, n], preferred_element_type=F32)
                                  for n in range(nb)], axis=1) + ba_ref[d:d + 1, :]
            ia = jnp.concatenate([jnp.dot(xcb[:, n * BS_A:(n + 1) * BS_A], wi_ref[d, n], preferred_element_type=F32)
                                  for n in range(nb)], axis=1) + bi_ref[d:d + 1, :]
            log_a = -LRU_C * _sigmoid(ra) * sp[d:d + 1, :]
            th = jnp.tanh(log_a)
            mult = jnp.sqrt(-2.0 * th / (1.0 - th))
            a_scr[d, r, :] = jnp.exp(log_a)
            b_scr[d, r, :] = mult * _sigmoid(ia) * xc
        return carry

    lax.fori_loop(0, t // tile, gates, 0)

    rowid = lax.broadcasted_iota(jnp.int32, (SUBLANE, cb), 0)
    nblk = t // SUBLANE

    def scan(i, carry):
        hf, hb = carry
        rf = _rows(i, SUBLANE)
        rb = _rows(nblk - 1 - i, SUBLANE)
        af, bf = _scan_block(a_scr[0, rf, :], b_scr[0, rf, :], rowid, False)
        ab, bb = _scan_block(a_scr[1, rb, :], b_scr[1, rb, :], rowid, True)
        hf_blk = bf + af * hf
        hb_blk = bb + ab * hb
        a_scr[0, rf, :] = hf_blk
        a_scr[1, rb, :] = hb_blk
        return hf_blk[SUBLANE - 1:SUBLANE, :], hb_blk[0:1, :]

    if has_init:
        init = (h0_ref[0:1, :], h0_ref[1:2, :])
    else:
        init = (jnp.zeros((1, cb), F32), jnp.zeros((1, cb), F32))
    hf, hb = lax.fori_loop(0, nblk, scan, init, unroll=2)
    sout_ref[0:1, :] = hf
    sout_ref[1:2, :] = hb

    def fin(i, carry):
        r = _rows(i, tile)
        out_ref[r, :] = ((a_scr[0, r, :] + a_scr[1, r, :]) * jax.nn.gelu(y_ref[r, :])).astype(BF16)
        return carry

    lax.fori_loop(0, t // tile, fin, 0)


def _lru(z, base, nseq, t, h0, conv_w, conv_b, w_a, b_a, w_i, b_i, lam, cb=512):
    nj = D_A // cb
    nb = cb // BS_A
    has_init = h0 is not None
    in_specs = [
        pl.BlockSpec((t, cb), lambda s, j: (base + s, j)),
        pl.BlockSpec((t, cb), lambda s, j: (base + s, nj + j)),
        pl.BlockSpec((CONV_W, cb), lambda s, j: (0, j)),
        pl.BlockSpec((1, cb), lambda s, j: (0, j)),
        pl.BlockSpec((2, nb, BS_A, BS_A), lambda s, j: (0, j, 0, 0)),
        pl.BlockSpec((2, cb), lambda s, j: (0, j)),
        pl.BlockSpec((2, nb, BS_A, BS_A), lambda s, j: (0, j, 0, 0)),
        pl.BlockSpec((2, cb), lambda s, j: (0, j)),
        pl.BlockSpec((2, cb), lambda s, j: (0, j)),
    ]
    args = [z, z, conv_w, conv_b.reshape(1, D_A), w_a.astype(BF16), b_a, w_i.astype(BF16), b_i, lam]
    if has_init:
        in_specs.append(pl.BlockSpec((None, 2, cb), lambda s, j: (s, 0, j)))
        args.append(h0)
    return pl.pallas_call(
        partial(_lru_kernel, t=t, cb=cb, has_init=has_init),
        grid=(nseq, nj),
        in_specs=in_specs,
        out_specs=[
            pl.BlockSpec((t, cb), lambda s, j: (s, j)),
            pl.BlockSpec((None, 2, cb), lambda s, j: (s, 0, j)),
        ],
        out_shape=[jax.ShapeDtypeStruct((nseq * t, D_A), BF16), jax.ShapeDtypeStruct((nseq, 2, D_A), F32)],
        scratch_shapes=[
            pltpu.VMEM((t + 2 * SUBLANE, cb), F32),
            pltpu.VMEM((t, cb), F32),
            pltpu.VMEM((2, t, cb), F32),
            pltpu.VMEM((2, t, cb), F32),
        ],
        compiler_params=_params(("arbitrary", "arbitrary")),
        name="rglru",
    )(*args)


def _mlstm_kernel(*refs, t, hg, has_init, has_prev, fill):
    it = iter(refs)
    q_ref, k_ref, v_ref, og_ref, zt_ref, bt_ref, cwq_ref, cbq_ref, cwk_ref, cbk_ref = (next(it) for _ in range(10))
    if has_init:
        c0_ref, n0_ref, m0_ref = next(it), next(it), next(it)
    if has_prev:
        next(it)
    mix_ref, cout_ref, nout_ref, mout_ref = (next(it) for _ in range(4))
    pad_scr, q_scr, k_scr, gate_scr, acc_scr, c_scr, n_scr = (next(it) for _ in range(7))
    CHUNK = MLSTM_CHUNK
    nchunk = t // CHUNK

    _dwconv_to(q_scr, q_ref, pad_scr, cwq_ref, cbq_ref, t, lambda v: _silu(v) * (DK_B ** -0.5))
    _dwconv_to(k_scr, k_ref, pad_scr, cwk_ref, cbk_ref, t, _silu)

    tile = min(t, 256)
    lane_t = lax.broadcasted_iota(jnp.int32, (tile, hg * LANE), 1) % LANE

    def gates(i, carry):
        r = _rows(i, tile)
        x = zt_ref[r, :] + bt_ref[...]
        logsig = jnp.minimum(x, 0.0) - jnp.log1p(jnp.exp(-jnp.abs(x)))
        gate_scr[r, :] = jnp.where(lane_t < 2, x, logsig)
        acc_scr[r, :] = jnp.zeros((tile, hg * DV_B), F32)
        return carry

    lax.fori_loop(0, t // tile, gates, 0)

    if has_init:
        c_scr[...] = c0_ref[...]
        n_scr[...] = n0_ref[...]
        m_init = tuple(m0_ref[d, u, :, 0:1] for u in range(hg) for d in range(2))
    else:
        c_scr[...] = jnp.zeros_like(c_scr)
        n_scr[...] = jnp.zeros_like(n_scr)
        m_init = tuple(jnp.zeros((1, 1), F32) for _ in range(2 * hg))

    row = lax.broadcasted_iota(jnp.int32, (CHUNK, CHUNK), 0)
    col = lax.broadcasted_iota(jnp.int32, (CHUNK, CHUNK), 1)
    lane_c = lax.broadcasted_iota(jnp.int32, (CHUNK, LANE), 1)

    def chunk(cidx, u, d, m):
        reverse = d == 1
        li, lf = d, 2 + d
        last = 0 if reverse else CHUNK - 1
        r = _rows(cidx, CHUNK)
        ck = slice(u * DK_B, (u + 1) * DK_B)
        cv = slice(u * DV_B, (u + 1) * DV_B)
        gt = gate_scr[r, u * LANE:(u + 1) * LANE]
        fsum = jnp.dot(_tri(reverse, CHUNK), gt, precision=HIGHEST, preferred_element_type=F32)
        y = jnp.where(lane_c == lf, fsum, gt)
        yt = y.T
        f_col, i_col = y[:, lf:lf + 1], y[:, li:li + 1]
        f_row, i_row = yt[lf:lf + 1, :], yt[li:li + 1, :]
        b = f_col + m
        mask = (col >= row) if reverse else (col <= row)
        dlog = jnp.where(mask, f_col - f_row + i_row, NEG_INF)
        m_t = jnp.maximum(b, jnp.max(dlog, axis=-1, keepdims=True))
        w = jnp.exp(dlog - m_t)
        inter = jnp.exp(b - m_t)
        q = q_scr[r, ck]
        k32 = k_scr[r, ck]
        v = v_ref[r, cv].astype(BF16)
        s = lax.dot_general(q, k32.astype(BF16), NT, preferred_element_type=F32) * w
        c_old = c_scr[d, u]
        n_old = n_scr[d, u]
        num = jnp.dot(s.astype(BF16), v, preferred_element_type=F32)
        num = num + inter * jnp.dot(q, c_old.astype(BF16), preferred_element_type=F32)
        den = jnp.sum(s, axis=-1, keepdims=True) + inter * jnp.sum(q.astype(F32) * n_old, axis=-1, keepdims=True)
        acc_scr[r, cv] += num / jnp.maximum(jnp.abs(den), jnp.exp(-m_t))
        m_new = m_t[last:last + 1, :]
        f_last = f_col[last:last + 1, :]
        kw = jnp.exp(f_last - f_col + i_col - m_new)
        decay = jnp.exp(f_last + m - m_new)
        kk = k32 * kw
        c_scr[d, u] = decay * c_old + lax.dot_general(kk.astype(BF16), v, TN, preferred_element_type=F32)
        n_scr[d, u] = decay * n_old + jnp.sum(kk, axis=0, keepdims=True)
        return m_new

    def body(c, carry):
        out = []
        for u in range(hg):
            out.append(chunk(c, u, 0, carry[2 * u]))
            out.append(chunk(nchunk - 1 - c, u, 1, carry[2 * u + 1]))
        return tuple(out)

    m_fin = lax.fori_loop(0, nchunk, body, m_init)

    _finalize_heads(acc_scr, og_ref, mix_ref, t, hg, DV_B)
    _state_target(cout_ref, fill)[...] = c_scr[...]
    nout_ref[...] = n_scr[...]
    for u in range(hg):
        for d in range(2):
            mout_ref[d, u] = jnp.broadcast_to(m_fin[2 * u + d], (1, LANE))


def _mlstm(z, zt, bias_t, base, nseq, t, init, conv_w, conv_b, hg, stack=None):
    has_init = init is not None
    prev, slot, nslots = stack if stack is not None else (None, 0, 1)
    fill = (slot, nslots) if stack is not None and prev is None else None
    kw, vw = hg * DK_B, hg * DV_B
    qoff = 2 * D_A // kw
    koff = qoff + H_B // hg
    voff = koff + H_B // hg
    goff = voff + H_B // hg
    cb2 = conv_b.reshape(1, 2 * H_B * DK_B)
    in_specs = [
        pl.BlockSpec((t, kw), lambda s, h: (base + s, qoff + h)),
        pl.BlockSpec((t, kw), lambda s, h: (base + s, koff + h)),
        pl.BlockSpec((t, vw), lambda s, h: (base + s, voff + h)),
        pl.BlockSpec((t, vw), lambda s, h: (base + s, goff + h)),
        pl.BlockSpec((t, hg * LANE), lambda s, h: (base + s, h)),
        pl.BlockSpec((1, hg * LANE), lambda s, h: (0, h)),
        pl.BlockSpec((CONV_W, kw), lambda s, h: (0, h)),
        pl.BlockSpec((1, kw), lambda s, h: (0, h)),
        pl.BlockSpec((CONV_W, kw), lambda s, h: (0, H_B // hg + h)),
        pl.BlockSpec((1, kw), lambda s, h: (0, H_B // hg + h)),
    ]
    args = [z, z, z, z, zt, bias_t, conv_w, cb2, conv_w, cb2]
    if has_init:
        c0, n0, m0 = init
        in_specs += [
            pl.BlockSpec((None, 2, hg, DK_B, DV_B), lambda s, h: (s, 0, h, 0, 0)),
            pl.BlockSpec((None, 2, hg, 1, DK_B), lambda s, h: (s, 0, h, 0, 0)),
            pl.BlockSpec((None, 2, hg, 1, LANE), lambda s, h: (s, 0, h, 0, 0)),
        ]
        args += [c0, n0.reshape(nseq, 2, H_B, 1, DK_B),
                 jnp.broadcast_to(m0[..., None, None], (nseq, 2, H_B, 1, LANE))]
    aliases = {}
    if prev is not None:
        in_specs.append(pl.BlockSpec(memory_space=pl.ANY))
        args.append(prev)
        aliases = {len(args) - 1: 1}
    out = pl.pallas_call(
        partial(_mlstm_kernel, t=t, hg=hg, has_init=has_init, has_prev=prev is not None, fill=fill),
        grid=(nseq, H_B // hg),
        in_specs=in_specs,
        out_specs=[
            pl.BlockSpec((t, vw), lambda s, h: (s, h)),
            _state_spec(fill, slot, hg, DK_B, DV_B),
            pl.BlockSpec((None, 2, hg, 1, DK_B), lambda s, h: (s, 0, h, 0, 0)),
            pl.BlockSpec((None, 2, hg, 1, LANE), lambda s, h: (s, 0, h, 0, 0)),
        ],
        out_shape=[
            jax.ShapeDtypeStruct((nseq * t, H_B * DV_B), BF16),
            jax.ShapeDtypeStruct((nseq, nslots, 2, H_B, DK_B, DV_B), F32),
            jax.ShapeDtypeStruct((nseq, 2, H_B, 1, DK_B), F32),
            jax.ShapeDtypeStruct((nseq, 2, H_B, 1, LANE), F32),
        ],
        scratch_shapes=[
            pltpu.VMEM((t + 2 * SUBLANE, kw), F32),
            pltpu.VMEM((t, kw), BF16),
            pltpu.VMEM((t, kw), F32),
            pltpu.VMEM((t, hg * LANE), F32),
            pltpu.VMEM((t, vw), F32),
            pltpu.VMEM((2, hg, DK_B, DV_B), F32),
            pltpu.VMEM((2, hg, 1, DK_B), F32),
        ],
        input_output_aliases=aliases,
        compiler_params=_params(("arbitrary", "arbitrary")),
        name="mlstm",
    )(*args)
    return out if stack is not None else (out[0], out[1][:, 0], out[2], out[3])


def _ret_kernel(*refs, t, hg, use_rope, has_init, has_prev, fill):
    it = iter(refs)
    lg_ref, q_ref, k_ref, v_ref, g_ref = (next(it) for _ in range(5))
    if use_rope:
        cos_ref, sin_ref = next(it), next(it)
    s0_ref = next(it) if has_init else None
    if has_prev:
        next(it)
    mix_ref, sout_ref, acc_scr, st_scr = (next(it) for _ in range(4))
    CHUNK = RET_CHUNK
    nchunk = t // CHUNK
    h0 = pl.program_id(1) * hg

    row = lax.broadcasted_iota(jnp.int32, (CHUNK, CHUNK), 0)
    col = lax.broadcasted_iota(jnp.int32, (CHUNK, CHUNK), 1)
    dist = (row - col).astype(F32)
    pos = lax.broadcasted_iota(jnp.int32, (CHUNK, DK_C), 0).astype(F32)
    consts = {}
    for u in range(hg):
        for d in range(2):
            lg = lg_ref[d, h0 + u]
            if d == 0:
                dm = jnp.where(dist >= 0, jnp.exp(jnp.maximum(dist, 0.0) * lg), 0.0)
                qin = jnp.exp((pos + 1.0) * lg)
                kout = jnp.exp((CHUNK - 1.0 - pos) * lg)
            else:
                dm = jnp.where(dist <= 0, jnp.exp(jnp.maximum(-dist, 0.0) * lg), 0.0)
                qin = jnp.exp((CHUNK - pos) * lg)
                kout = jnp.exp(pos * lg)
            gch = jnp.exp(jnp.full((1, DV_C), float(CHUNK), F32) * lg)
            consts[u, d] = (dm, qin, kout, gch)

    acc_scr[...] = jnp.zeros_like(acc_scr)
    if has_init:
        st_scr[...] = s0_ref[...]
    else:
        st_scr[...] = jnp.zeros_like(st_scr)

    def chunk(cidx, u, d):
        dm, qin, kout, gch = consts[u, d]
        r = _rows(cidx, CHUNK)
        ck = slice(u * DK_C, (u + 1) * DK_C)
        cv = slice(u * DV_C, (u + 1) * DV_C)
        q = q_ref[r, ck]
        k = k_ref[r, ck] * (DK_C ** -0.5)
        if use_rope:
            cs, sn = cos_ref[r, :], sin_ref[r, :]
            q = q * cs + pltpu.roll(q, DK_C // 2, 1) * sn
            k = k * cs + pltpu.roll(k, DK_C // 2, 1) * sn
        v = v_ref[r, cv].astype(BF16)
        st = st_scr[d, u]
        s = lax.dot_general(q.astype(BF16), k.astype(BF16), NT, preferred_element_type=F32) * dm
        o = jnp.dot(s.astype(BF16), v, preferred_element_type=F32)
        o = o + jnp.dot((q * qin).astype(BF16), st.astype(BF16), preferred_element_type=F32)
        st_scr[d, u] = gch * st + lax.dot_general((k * kout).astype(BF16), v, TN, preferred_element_type=F32)
        acc_scr[r, cv] += o

    def body(c, carry):
        for u in range(hg):
            chunk(c, u, 0)
            chunk(nchunk - 1 - c, u, 1)
        return carry

    lax.fori_loop(0, nchunk, body, 0)
    _finalize_heads(acc_scr, g_ref, mix_ref, t, hg, DV_C)
    _state_target(sout_ref, fill)[...] = st_scr[...]


def _retention(z, base, nseq, t, s0, lg, rope, hg, stack=None):
    has_init = s0 is not None
    prev, slot, nslots = stack if stack is not None else (None, 0, 1)
    fill = (slot, nslots) if stack is not None and prev is None else None
    use_rope = rope is not None
    kw, vw = hg * DK_C, hg * DV_C
    qoff, koff = 0, H_C * DK_C // kw
    voff = 2 * H_C * DK_C // vw
    goff = voff + H_C // hg
    in_specs = [
        pl.BlockSpec(memory_space=pltpu.SMEM),
        pl.BlockSpec((t, kw), lambda s, h: (base + s, qoff + h)),
        pl.BlockSpec((t, kw), lambda s, h: (base + s, koff + h)),
        pl.BlockSpec((t, vw), lambda s, h: (base + s, voff + h)),
        pl.BlockSpec((t, vw), lambda s, h: (base + s, goff + h)),
    ]
    args = [lg, z, z, z, z]
    if use_rope:
        in_specs += [pl.BlockSpec((t, DK_C), lambda s, h: (0, 0)), pl.BlockSpec((t, DK_C), lambda s, h: (0, 0))]
        args += list(rope)
    if has_init:
        in_specs.append(pl.BlockSpec((None, 2, hg, DK_C, DV_C), lambda s, h: (s, 0, h, 0, 0)))
        args.append(s0)
    aliases = {}
    if prev is not None:
        in_specs.append(pl.BlockSpec(memory_space=pl.ANY))
        args.append(prev)
        aliases = {len(args) - 1: 1}
    out = pl.pallas_call(
        partial(_ret_kernel, t=t, hg=hg, use_rope=use_rope, has_init=has_init, has_prev=prev is not None, fill=fill),
        grid=(nseq, H_C // hg),
        in_specs=in_specs,
        out_specs=[
            pl.BlockSpec((t, vw), lambda s, h: (s, h)),
            _state_spec(fill, slot, hg, DK_C, DV_C),
        ],
        out_shape=[
            jax.ShapeDtypeStruct((nseq * t, H_C * DV_C), BF16),
            jax.ShapeDtypeStruct((nseq, nslots, 2, H_C, DK_C, DV_C), F32),
        ],
        scratch_shapes=[pltpu.VMEM((t, vw), F32), pltpu.VMEM((2, hg, DK_C, DV_C), F32)],
        input_output_aliases=aliases,
        compiler_params=_params(("arbitrary", "arbitrary")),
        name="retention",
    )(*args)
    return out if stack is not None else (out[0], out[1][:, 0])


def _gla_intra(q, k, g2, reverse):
    CHUNK = GLA_CHUNK
    nsub = CHUNK // SUB
    lane = lax.broadcasted_iota(jnp.int32, (SUB, CHUNK), 1)
    rowi = lax.broadcasted_iota(jnp.int32, (SUB, CHUNK), 0)
    out = []
    for jb in range(nsub):
        lo, hi = jb * SUB, (jb + 1) * SUB
        qj, gj = q[lo:hi], g2[lo:hi]
        if reverse and jb < nsub - 1:
            gref = g2[hi:hi + 1]
            kt = (k[hi:] * jnp.exp2(gref - g2[hi:])).astype(BF16)
            kt = jnp.concatenate([jnp.zeros((hi, DK_D), BF16), kt], axis=0)
        elif not reverse and jb > 0:
            gref = g2[lo - 1:lo]
            kt = (k[:lo] * jnp.exp2(gref - g2[:lo])).astype(BF16)
            kt = jnp.concatenate([kt, jnp.zeros((CHUNK - lo, DK_D), BF16)], axis=0)
        else:
            kt = None
        if kt is None:
            off = jnp.zeros((SUB, CHUNK), F32)
        else:
            qt = (qj * jnp.exp2(gj - gref)).astype(BF16)
            off = lax.dot_general(qt, kt, NT, preferred_element_type=F32)
        diag = jnp.zeros((SUB, CHUNK), F32)
        for i in range(lo, hi):
            col = jnp.sum(qj * jnp.exp2(gj - g2[i:i + 1]) * k[i:i + 1], axis=-1, keepdims=True)
            diag = jnp.where(lane == i, col, diag)
        mask = (rowi + lo <= lane) if reverse else (rowi + lo >= lane)
        out.append(off + jnp.where(mask, diag, 0.0))
    return jnp.concatenate(out, axis=0)


def _gla_kernel(*refs, t, hg, has_init, has_prev, fill):
    it = iter(refs)
    q_ref, k_ref, v_ref, r_ref, a_ref, wup_ref, gb_ref = (next(it) for _ in range(7))
    s0_ref = next(it) if has_init else None
    if has_prev:
        next(it)
    mix_ref, sout_ref, acc_scr, st_scr, g_scr = (next(it) for _ in range(5))
    CHUNK = GLA_CHUNK
    nchunk = t // CHUNK
    tile = min(t, 256)

    def gates(i, carry):
        r = _rows(i, tile)
        a = a_ref[r, :].astype(BF16)
        for u in range(hg):
            for d in range(2):
                x = jnp.dot(a, wup_ref[d, u], preferred_element_type=F32) + gb_ref[d, u]
                g_scr[d, u, r, :] = (jnp.minimum(x, 0.0) - jnp.log1p(jnp.exp(-jnp.abs(x)))) * (1.0 / GLA_TAU)
        acc_scr[r, :] = jnp.zeros((tile, hg * DV_D), F32)
        return carry

    lax.fori_loop(0, t // tile, gates, 0)

    for u in range(hg):
        for d in range(2):
            st_scr[d, u] = s0_ref[d, u].T if has_init else jnp.zeros((DV_D, DK_D), F32)

    def chunk(cidx, u, d):
        reverse = d == 1
        last = 0 if reverse else CHUNK - 1
        r = _rows(cidx, CHUNK)
        ck = slice(u * DK_D, (u + 1) * DK_D)
        cv = slice(u * DV_D, (u + 1) * DV_D)
        g2 = jnp.dot(_tri(reverse, CHUNK), g_scr[d, u, r, :], precision=HIGHEST, preferred_element_type=F32) * LOG2E
        g2_last = g2[last:last + 1, :]
        q = q_ref[r, ck] * (DK_D ** -0.5)
        k = k_ref[r, ck]
        v = v_ref[r, cv].astype(BF16)
        st = st_scr[d, u]
        s = _gla_intra(q, k, g2, reverse)
        o = jnp.dot(s.astype(BF16), v, preferred_element_type=F32)
        o = o + lax.dot_general((q * jnp.exp2(g2)).astype(BF16), st.astype(BF16), NT, preferred_element_type=F32)
        kd = (k * jnp.exp2(g2_last - g2)).astype(BF16)
        st_scr[d, u] = jnp.exp2(g2_last) * st + lax.dot_general(v, kd, TN, preferred_element_type=F32)
        acc_scr[r, cv] += o

    def body(c, carry):
        for u in range(hg):
            chunk(c, u, 0)
            chunk(nchunk - 1 - c, u, 1)
        return carry

    lax.fori_loop(0, nchunk, body, 0)
    _finalize_heads(acc_scr, r_ref, mix_ref, t, hg, DV_D)
    state_out = _state_target(sout_ref, fill)
    for u in range(hg):
        for d in range(2):
            state_out[d, u] = st_scr[d, u].T


def _gla(z, zt, base, nseq, t, s0, w_up, g_bias, hg, stack=None):
    has_init = s0 is not None
    prev, slot, nslots = stack if stack is not None else (None, 0, 1)
    fill = (slot, nslots) if stack is not None and prev is None else None
    kw, vw = hg * DK_D, hg * DV_D
    cbase = 2 * H_C * DK_C + 2 * H_C * DV_C
    qoff = cbase // kw
    koff = qoff + H_D // hg
    voff = (cbase + 2 * H_D * DK_D) // vw
    roff = voff + H_D // hg
    in_specs = [
        pl.BlockSpec((t, kw), lambda s, h: (base + s, qoff + h)),
        pl.BlockSpec((t, kw), lambda s, h: (base + s, koff + h)),
        pl.BlockSpec((t, vw), lambda s, h: (base + s, voff + h)),
        pl.BlockSpec((t, vw), lambda s, h: (base + s, roff + h)),
        pl.BlockSpec((t, LANE), lambda s, h: (base + s, 0)),
        pl.BlockSpec((2, hg, LANE, DK_D), lambda s, h: (0, h, 0, 0)),
        pl.BlockSpec((2, hg, 1, DK_D), lambda s, h: (0, h, 0, 0)),
    ]
    args = [z, z, z, z, zt, w_up, g_bias]
    if has_init:
        in_specs.append(pl.BlockSpec((None, 2, hg, DK_D, DV_D), lambda s, h: (s, 0, h, 0, 0)))
        args.append(s0)
    aliases = {}
    if prev is not None:
        in_specs.append(pl.BlockSpec(memory_space=pl.ANY))
        args.append(prev)
        aliases = {len(args) - 1: 1}
    out = pl.pallas_call(
        partial(_gla_kernel, t=t, hg=hg, has_init=has_init, has_prev=prev is not None, fill=fill),
        grid=(nseq, H_D // hg),
        in_specs=in_specs,
        out_specs=[
            pl.BlockSpec((t, vw), lambda s, h: (s, h)),
            _state_spec(fill, slot, hg, DK_D, DV_D),
        ],
        out_shape=[
            jax.ShapeDtypeStruct((nseq * t, H_D * DV_D), BF16),
            jax.ShapeDtypeStruct((nseq, nslots, 2, H_D, DK_D, DV_D), F32),
        ],
        scratch_shapes=[
            pltpu.VMEM((t, vw), F32),
            pltpu.VMEM((2, hg, DV_D, DK_D), F32),
            pltpu.VMEM((2, hg, t, DK_D), F32),
        ],
        input_output_aliases=aliases,
        compiler_params=_params(("arbitrary", "arbitrary")),
        name="gla",
    )(*args)
    return out if stack is not None else (out[0], out[1][:, 0])


def _rope_tables(t):
    pos = jnp.arange(t)
    row = (pos // GRID_W).astype(F32)
    col = (pos % GRID_W).astype(F32)
    nf = DK_C // 4
    freqs = ROPE_BASE ** (-jnp.arange(nf, dtype=F32) / nf)
    ang = jnp.concatenate([row[:, None] * freqs, col[:, None] * freqs], axis=-1)
    cos, sin = jnp.cos(ang), jnp.sin(ang)
    return jnp.concatenate([cos, cos], axis=-1), jnp.concatenate([-sin, sin], axis=-1)


def _ab_tail(w_in, i_bias, f_bias):
    n_main = w_in.shape[1] - 4 * H_B
    gates = w_in[:, n_main:].reshape(D_MODEL, 4, H_B)
    cols = jnp.swapaxes(gates, 1, 2)
    w_tail = jnp.pad(cols, ((0, 0), (0, 0), (0, LANE - 4))).reshape(D_MODEL, H_B * LANE)
    bias = jnp.concatenate([i_bias, f_bias], axis=0).T
    bias_t = jnp.pad(bias, ((0, 0), (0, LANE - 4))).reshape(1, H_B * LANE)
    return n_main, w_tail, bias_t


def _cd_tail(w_in, w_up, g_bias):
    n_main = w_in.shape[1] - 2 * GLA_RANK
    w_tail = jnp.pad(w_in[:, n_main:], ((0, 0), (0, LANE - 2 * GLA_RANK)))
    up = w_up.reshape(2, GLA_RANK, H_D, DK_D).transpose(0, 2, 1, 3)
    up = jnp.stack([jnp.pad(up[d], ((0, 0), (d * GLA_RANK, LANE - (d + 1) * GLA_RANK), (0, 0))) for d in range(2)])
    return n_main, w_tail, up.astype(BF16), g_bias.reshape(2, H_D, 1, DK_D)


def kernel(x_prompt, x_sample, c, state_lru, state_mlstm_C, state_mlstm_n, state_mlstm_m, state_ret, state_gla, c_ctx, w_mod, b_mod, norm_w, ffn_w_gate, ffn_w_up, ffn_w_down, w_in_ab, w_out_ab, lru_conv_w, lru_conv_b, lru_w_a, lru_b_a, lru_w_i, lru_b_i, lru_lambda, mlstm_conv_w, mlstm_conv_b, mlstm_i_bias, mlstm_f_bias, w_in_cd, w_out_cd, ret_decay_log, gla_w_up, gla_b, final_norm_w):
    Bp, Tp, D = x_prompt.shape
    Bs, Ts, _ = x_sample.shape
    n_ctx = Bp * Tp
    tm_ffn, tf_ffn, tm_in, tm_out = 1024, 256, 1024, 512
    assert all(n_ctx % t == 0 and Ts % t == 0 for t in (tm_ffn, tm_in, tm_out)) and 1 + Bs <= MOD_ROWS and n_ctx % Ts == 0
    row_ffn, row_in, row_out = (_mod_row_map(n_ctx // t, Ts // t) for t in (tm_ffn, tm_in, tm_out))
    groups = ((0, Bp, Tp), (n_ctx // Ts, Bs, Ts))
    hg_ctx = {"mlstm": 4, "ret": 4, "gla": 4}
    hg_lat = {"mlstm": 1, "ret": 2, "gla": 2}

    cond = jnp.concatenate([c_ctx[None], c, jnp.zeros((MOD_ROWS - 1 - Bs, D), F32)], axis=0)
    mod = _modulation(cond, w_mod, b_mod).reshape(DEPTH, MOD_ROWS, N_MOD, D)
    nw = norm_w.reshape(DEPTH, 3, 1, D)
    wg, wu, wd = ffn_w_gate, ffn_w_up, ffn_w_down
    w_in_ab_b, w_in_cd_b = w_in_ab, w_in_cd
    w_out_ab_b, w_out_cd_b = w_out_ab.astype(BF16), w_out_cd.astype(BF16)

    x = jnp.concatenate([x_prompt.reshape(n_ctx, D), x_sample.reshape(Bs * Ts, D)], axis=0)
    rope = _rope_tables(Ts)

    st_lru, st_n, st_m = [], [], []
    new_C = new_ret = new_gla = None
    for l in range(DEPTH):
        x = _ffn(x, mod, nw, wg, wu, wd, l, 0, row_ffn, tm_ffn, tf_ffn)
        (pb, pn, pt), (sb, sn, st) = groups
        if l % 2 == 0:
            e = l // 2
            n_main, w_tail, bias_t = _ab_tail(w_in_ab[e], mlstm_i_bias[e], mlstm_f_bias[e])
            z, zt = _inproj(x, mod, nw, w_in_ab_b, l, e, n_main, w_tail.astype(BF16), row_in, tm_in)
            lru_args = (lru_conv_w[e], lru_conv_b[e], lru_w_a[e], lru_b_a[e], lru_w_i[e], lru_b_i[e], lru_lambda[e])
            a_p, s_lru = _lru(z, pb, pn, pt, None, *lru_args)
            a_s, _ = _lru(z, sb, sn, st, state_lru[:, e], *lru_args)
            b_p, new_C, s_n, s_m = _mlstm(z, zt, bias_t, pb, pn, pt, None, mlstm_conv_w[e], mlstm_conv_b[e], hg_ctx["mlstm"],
                                          stack=(new_C, e, N_EVEN))
            b_s, _, _, _ = _mlstm(z, zt, bias_t, sb, sn, st,
                                  (state_mlstm_C[:, e], state_mlstm_n[:, e], state_mlstm_m[:, e]),
                                  mlstm_conv_w[e], mlstm_conv_b[e], hg_lat["mlstm"])
            st_lru.append(s_lru)
            st_n.append(s_n.reshape(Bp, 2, H_B, DK_B))
            st_m.append(s_m[:, :, :, 0, 0])
            mix_ctx, mix_lat = (a_p, b_p), (a_s, b_s)
            w_out = w_out_ab_b
        else:
            e = l // 2
            n_main, w_tail, w_up, g_bias = _cd_tail(w_in_cd[e], gla_w_up[e], gla_b[e])
            z, zt = _inproj(x, mod, nw, w_in_cd_b, l, e, n_main, w_tail.astype(BF16), row_in, tm_in)
            c_p, new_ret = _retention(z, pb, pn, pt, None, ret_decay_log[e], None, hg_ctx["ret"], stack=(new_ret, e, N_ODD))
            c_s, _ = _retention(z, sb, sn, st, state_ret[:, e], ret_decay_log[e], rope, hg_lat["ret"])
            d_p, new_gla = _gla(z, zt, pb, pn, pt, None, w_up, g_bias, hg_ctx["gla"], stack=(new_gla, e, N_ODD))
            d_s, _ = _gla(z, zt, sb, sn, st, state_gla[:, e], w_up, g_bias, hg_lat["gla"])
            mix_ctx, mix_lat = (c_p, d_p), (c_s, d_s)
            w_out = w_out_cd_b
        x = _outproj(x, mix_ctx, mix_lat, mod, w_out, l, e, row_out, tm_out)
        x = _ffn(x, mod, nw, wg, wu, wd, l, 1, row_ffn, tm_ffn, tf_ffn)

    y_prompt = _final_norm(x, final_norm_w, 0, n_ctx).reshape(Bp, Tp, D)
    y_sample = _final_norm(x, final_norm_w, n_ctx, Bs * Ts).reshape(Bs, Ts, D)
    return (y_prompt, y_sample,
            jnp.stack(st_lru, axis=1), new_C, jnp.stack(st_n, axis=1), jnp.stack(st_m, axis=1), new_ret, new_gla)
```

```python
from functools import partial

import jax
import jax.numpy as jnp
from jax import lax
from jax.experimental import pallas as pl
from jax.experimental.pallas import tpu as pltpu

F32 = jnp.float32
BF16 = jnp.bfloat16

D_MODEL = 2048
DEPTH = 4
N_MOD = 9
N_EVEN = (DEPTH + 1) // 2
N_ODD = DEPTH // 2
D_FF = 5632
EPS = 1e-6
CHUNK = 64
CONV_W = 4
D_A = D_MODEL // 2
NB_A = 8
BS_A = D_A // NB_A
LRU_C = 8.0
H_B = 4
DK_B = D_MODEL // 8
DV_B = D_MODEL // 8
H_C = 4
DK_C = D_MODEL // 16
DV_C = D_MODEL // 8
H_D = 4
DK_D = D_MODEL // 16
DV_D = D_MODEL // 8
GLA_RANK = 16
GLA_TAU = 16.0
ROPE_BASE = 10000.0
GRID_W = 64

MOD_ROWS = 8
LANE = 128
SUBLANE = 8
SUB = 16
MLSTM_CHUNK = 256
RET_CHUNK = 256
GLA_CHUNK = 128
VMEM_LIMIT = 56 * 1024 * 1024
FFN_VMEM_LIMIT = 60 * 1024 * 1024
NEG_INF = float("-inf")
LOG2E = 1.4426950408889634
HIGHEST = lax.Precision.HIGHEST
NT = (((1,), (1,)), ((), ()))
TN = (((0,), (0,)), ((), ()))


def _params(sem, vmem_limit=VMEM_LIMIT):
    return pltpu.CompilerParams(dimension_semantics=sem, vmem_limit_bytes=vmem_limit)


def _silu(x):
    return x * jax.nn.sigmoid(x)


def _sigmoid(x):
    return 0.5 * jnp.tanh(0.5 * x) + 0.5


def _rows(i, n):
    return pl.ds(pl.multiple_of(i * n, n), n)


def _mod_kernel(c_ref, w_ref, b_ref, o_ref):
    s = _silu(c_ref[...]).astype(BF16)
    o_ref[...] = jnp.dot(s, w_ref[...].astype(BF16), preferred_element_type=F32) + b_ref[...]


def _modulation(cond, w_mod, b_mod, tn=1024):
    n = w_mod.shape[-1]
    return pl.pallas_call(
        _mod_kernel,
        grid=(DEPTH, n // tn),
        in_specs=[
            pl.BlockSpec((MOD_ROWS, D_MODEL), lambda l, j: (0, 0)),
            pl.BlockSpec((None, D_MODEL, tn), lambda l, j: (l, 0, j)),
            pl.BlockSpec((None, 1, tn), lambda l, j: (l, 0, j)),
        ],
        out_specs=pl.BlockSpec((None, MOD_ROWS, tn), lambda l, j: (l, 0, j)),
        out_shape=jax.ShapeDtypeStruct((DEPTH, MOD_ROWS, n), F32),
        compiler_params=_params(("arbitrary", "arbitrary")),
        name="modulation",
    )(cond, w_mod, b_mod.reshape(DEPTH, 1, n))


def _norm_mod(x, nw, shift, scale):
    ms = jnp.mean(x * x, axis=-1, keepdims=True)
    y = x * lax.rsqrt(ms + EPS) * nw
    return y * (1.0 + scale) + shift


def _mod_row_map(n_ctx_tiles, tiles_per_req):
    def row(i):
        return jnp.where(i < n_ctx_tiles, 0, 1 + (i - n_ctx_tiles) // tiles_per_req)
    return row


def _ffn_kernel(x_ref, mod_ref, nw_ref, wg_ref, wu_ref, wd_ref, o_ref, h_scr, *, k, nf):
    f = pl.program_id(1)

    @pl.when(f == 0)
    def _():
        h = _norm_mod(x_ref[...], nw_ref[...], mod_ref[3 * k:3 * k + 1, :], mod_ref[3 * k + 1:3 * k + 2, :])
        h_scr[...] = h.astype(BF16)

    def down_proj():
        h = h_scr[...]
        g = jnp.dot(h, wg_ref[...].astype(BF16), preferred_element_type=F32)
        u = jnp.dot(h, wu_ref[...].astype(BF16), preferred_element_type=F32)
        a = (_silu(g) * u).astype(BF16)
        return jnp.dot(a, wd_ref[...].astype(BF16), preferred_element_type=F32)

    @pl.when(f == 0)
    def _():
        o_ref[...] = down_proj()

    @pl.when(jnp.logical_and(f > 0, f < nf - 1))
    def _():
        o_ref[...] += down_proj()

    @pl.when(f == nf - 1)
    def _():
        o_ref[...] = x_ref[...] + (0.5 * mod_ref[3 * k + 2:3 * k + 3, :]) * (o_ref[...] + down_proj())


def _ffn(x, mod, nw, wg, wu, wd, l, j, row_of_tile, tm, tf):
    t = x.shape[0]
    nf = D_FF // tf
    assert nf >= 2
    k = 2 * j
    return pl.pallas_call(
        partial(_ffn_kernel, k=k, nf=nf),
        grid=(t // tm, nf),
        in_specs=[
            pl.BlockSpec((tm, D_MODEL), lambda i, f: (i, 0)),
            pl.BlockSpec((None, None, N_MOD, D_MODEL), lambda i, f: (l, row_of_tile(i), 0, 0)),
            pl.BlockSpec((None, None, 1, D_MODEL), lambda i, f: (l, k, 0, 0)),
            pl.BlockSpec((None, None, D_MODEL, tf), lambda i, f: (l, j, 0, f)),
            pl.BlockSpec((None, None, D_MODEL, tf), lambda i, f: (l, j, 0, f)),
            pl.BlockSpec((None, None, tf, D_MODEL), lambda i, f: (l, j, f, 0)),
        ],
        out_specs=pl.BlockSpec((tm, D_MODEL), lambda i, f: (i, 0)),
        out_shape=jax.ShapeDtypeStruct((t, D_MODEL), F32),
        scratch_shapes=[pltpu.VMEM((tm, D_MODEL), BF16)],
        compiler_params=_params(("arbitrary", "arbitrary"), FFN_VMEM_LIMIT),
        name="ffn",
    )(x, mod, nw, wg, wu, wd)


def _inproj_kernel(x_ref, mod_ref, nw_ref, w_ref, wt_ref, z_ref, zt_ref, h_scr):
    j = pl.program_id(1)

    @pl.when(j == 0)
    def _():
        h = _norm_mod(x_ref[...], nw_ref[...], mod_ref[3:4, :], mod_ref[4:5, :]).astype(BF16)
        h_scr[...] = h
        zt_ref[...] = jnp.dot(h, wt_ref[...], preferred_element_type=F32)
        z_ref[...] = jnp.dot(h, w_ref[...], preferred_element_type=F32)

    @pl.when(j > 0)
    def _():
        z_ref[...] = jnp.dot(h_scr[...], w_ref[...], preferred_element_type=F32)


def _inproj(x, mod, nw, w_in, l, e, n, w_tail, row_of_tile, tm, tn=1024):
    t = x.shape[0]
    tw = w_tail.shape[1]
    return pl.pallas_call(
        _inproj_kernel,
        grid=(t // tm, n // tn),
        in_specs=[
            pl.BlockSpec((tm, D_MODEL), lambda i, j: (i, 0)),
            pl.BlockSpec((None, None, N_MOD, D_MODEL), lambda i, j: (l, row_of_tile(i), 0, 0)),
            pl.BlockSpec((None, None, 1, D_MODEL), lambda i, j: (l, 1, 0, 0)),
            pl.BlockSpec((None, D_MODEL, tn), lambda i, j: (e, 0, j)),
            pl.BlockSpec((D_MODEL, tw), lambda i, j: (0, 0)),
        ],
        out_specs=[
            pl.BlockSpec((tm, tn), lambda i, j: (i, j)),
            pl.BlockSpec((tm, tw), lambda i, j: (i, 0)),
        ],
        out_shape=[jax.ShapeDtypeStruct((t, n), F32), jax.ShapeDtypeStruct((t, tw), F32)],
        scratch_shapes=[pltpu.VMEM((tm, D_MODEL), BF16)],
        compiler_params=_params(("arbitrary", "arbitrary")),
        name="inproj",
    )(x, mod, nw, w_in, w_tail)


def _outproj_kernel(x_ref, ap_ref, bp_ref, as_ref, bs_ref, mod_ref, wa_ref, wb_ref, o_ref, *, n_ctx_tiles):
    def emit(ma_ref, mb_ref):
        y = jnp.dot(ma_ref[...], wa_ref[...], preferred_element_type=F32)
        y = y + jnp.dot(mb_ref[...], wb_ref[...], preferred_element_type=F32)
        o_ref[...] = x_ref[...] + mod_ref[5:6, :] * y

    is_ctx = pl.program_id(0) < n_ctx_tiles
    pl.when(is_ctx)(lambda: emit(ap_ref, bp_ref))
    pl.when(jnp.logical_not(is_ctx))(lambda: emit(as_ref, bs_ref))


def _outproj(x, mix_ctx, mix_lat, mod, w_out, l, e, row_of_tile, tm):
    t = x.shape[0]
    half = D_MODEL // 2
    n_ctx_tiles = mix_ctx[0].shape[0] // tm
    n_lat_tiles = mix_lat[0].shape[0] // tm
    ctx_blk = lambda i: (jnp.minimum(i, n_ctx_tiles - 1), 0)
    lat_blk = lambda i: (jnp.clip(i - n_ctx_tiles, 0, n_lat_tiles - 1), 0)
    return pl.pallas_call(
        partial(_outproj_kernel, n_ctx_tiles=n_ctx_tiles),
        grid=(t // tm,),
        in_specs=[
            pl.BlockSpec((tm, D_MODEL), lambda i: (i, 0)),
            pl.BlockSpec((tm, half), ctx_blk),
            pl.BlockSpec((tm, half), ctx_blk),
            pl.BlockSpec((tm, half), lat_blk),
            pl.BlockSpec((tm, half), lat_blk),
            pl.BlockSpec((None, None, N_MOD, D_MODEL), lambda i: (l, row_of_tile(i), 0, 0)),
            pl.BlockSpec((None, half, D_MODEL), lambda i: (e, 0, 0)),
            pl.BlockSpec((None, half, D_MODEL), lambda i: (e, 1, 0)),
        ],
        out_specs=pl.BlockSpec((tm, D_MODEL), lambda i: (i, 0)),
        out_shape=jax.ShapeDtypeStruct((t, D_MODEL), F32),
        compiler_params=_params(("arbitrary",)),
        name="outproj",
    )(x, *mix_ctx, *mix_lat, mod, w_out, w_out)


def _final_norm_kernel(x_ref, w_ref, o_ref):
    x = x_ref[...]
    ms = jnp.mean(x * x, axis=-1, keepdims=True)
    o_ref[...] = x * lax.rsqrt(ms + EPS) * w_ref[...]


def _final_norm(x, w, row0, rows, tm=512):
    base = row0 // tm
    return pl.pallas_call(
        _final_norm_kernel,
        grid=(rows // tm,),
        in_specs=[pl.BlockSpec((tm, D_MODEL), lambda i: (base + i, 0)), pl.BlockSpec((1, D_MODEL), lambda i: (0, 0))],
        out_specs=pl.BlockSpec((tm, D_MODEL), lambda i: (i, 0)),
        out_shape=jax.ShapeDtypeStruct((rows, D_MODEL), F32),
        compiler_params=_params(("arbitrary",)),
        name="final_norm",
    )(x, w.reshape(1, D_MODEL))


def _finalize_heads(acc_scr, gate_ref, mix_ref, t, hg, dv):
    tile = min(t, 256)

    def body(i, carry):
        r = _rows(i, tile)
        for u in range(hg):
            cv = slice(u * dv, (u + 1) * dv)
            o = acc_scr[r, cv]
            d = o - jnp.mean(o, axis=-1, keepdims=True)
            var = jnp.mean(d * d, axis=-1, keepdims=True)
            mix_ref[r, cv] = (d * lax.rsqrt(var + EPS) * _silu(gate_ref[r, cv])).astype(BF16)
        return carry

    lax.fori_loop(0, t // tile, body, 0)


def _state_target(out_ref, fill):
    if fill is None:
        return out_ref
    slot, nslots = fill
    for o in range(nslots):
        if o != slot:
            out_ref[o] = jnp.zeros(out_ref.shape[1:], out_ref.dtype)
    return out_ref.at[slot]


def _state_spec(fill, slot, hg, dk, dv):
    if fill is None:
        return pl.BlockSpec((None, None, 2, hg, dk, dv), lambda s, h: (s, slot, 0, h, 0, 0))
    return pl.BlockSpec((None, fill[1], 2, hg, dk, dv), lambda s, h: (s, 0, 0, h, 0, 0))


def _dwconv_to(dst_ref, src_ref, pad_scr, w_ref, b_ref, t, post):
    c = src_ref.shape[1]
    pad_scr[pl.ds(0, SUBLANE), :] = jnp.zeros((SUBLANE, c), F32)
    pad_scr[pl.ds(t + SUBLANE, SUBLANE), :] = jnp.zeros((SUBLANE, c), F32)
    pad_scr[pl.ds(SUBLANE, t), :] = src_ref[...]
    tile = min(t, 256)
    for r0 in range(0, t, tile):
        win = pad_scr[pl.ds(r0, tile + 2 * SUBLANE), :]
        y = b_ref[...]
        for k in range(CONV_W):
            off = SUBLANE - CONV_W // 2 + k
            y = y + w_ref[k:k + 1, :] * win[off:off + tile]
        dst_ref[pl.ds(r0, tile), :] = post(y).astype(dst_ref.dtype)


def _tri(reverse, n):
    row = lax.broadcasted_iota(jnp.int32, (n, n), 0)
    col = lax.broadcasted_iota(jnp.int32, (n, n), 1)
    return jnp.where((col >= row) if reverse else (col <= row), 1.0, 0.0).astype(F32)


def _scan_block(a, b, rowid, reverse):
    for s in (1, 2, 4):
        if reverse:
            valid = rowid < SUBLANE - s
            shift = SUBLANE - s
        else:
            valid = rowid >= s
            shift = s
        a_sh = jnp.where(valid, pltpu.roll(a, shift, 0), 1.0)
        b_sh = jnp.where(valid, pltpu.roll(b, shift, 0), 0.0)
        b = b + a * b_sh
        a = a * a_sh
    return a, b


def _lru_kernel(*refs, t, cb, has_init):
    it = iter(refs)
    x_ref, y_ref, cw_ref, cbias_ref, wa_ref, ba_ref, wi_ref, bi_ref, lam_ref = (next(it) for _ in range(9))
    h0_ref = next(it) if has_init else None
    out_ref, sout_ref, pad_scr, xc_scr, a_scr, b_scr = (next(it) for _ in range(6))
    nb = cb // BS_A

    _dwconv_to(xc_scr, x_ref, pad_scr, cw_ref, cbias_ref, t, lambda v: v)

    lam = lam_ref[...]
    sp = jnp.maximum(-lam, 0.0) + jnp.log1p(jnp.exp(-jnp.abs(lam)))
    tile = min(t, 256)

    def gates(i, carry):
        r = _rows(i, tile)
        xc = xc_scr[r, :]
        xcb = xc.astype(BF16)
        for d in range(2):
            ra = jnp.concatenate([jnp.dot(xcb[:, n * BS_A:(n + 1) * BS_A], wa_ref[d, n], preferred_element_type=F32)
                                  for n in range(nb)], axis=1) + ba_ref[d:d + 1, :]
            ia = jnp.concatenate([jnp.dot(xcb[:, n * BS_A:(n + 1) * BS_A], wi_ref[d, n], preferred_element_type=F32)
                                  for n in range(nb)], axis=1) + bi_ref[d:d + 1, :]
            log_a = -LRU_C * _sigmoid(ra) * sp[d:d + 1, :]
            th = jnp.tanh(log_a)
            mult = jnp.sqrt(-2.0 * th / (1.0 - th))
            a_scr[d, r, :] = jnp.exp(log_a)
            b_scr[d, r, :] = mult * _sigmoid(ia) * xc
        return carry

    lax.fori_loop(0, t // tile, gates, 0)

    rowid = lax.broadcasted_iota(jnp.int32, (SUBLANE, cb), 0)
    nblk = t // SUBLANE

    def scan(i, carry):
        hf, hb = carry
        rf = _rows(i, SUBLANE)
        rb = _rows(nblk - 1 - i, SUBLANE)
        af, bf = _scan_block(a_scr[0, rf, :], b_scr[0, rf, :], rowid, False)
        ab, bb = _scan_block(a_scr[1, rb, :], b_scr[1, rb, :], rowid, True)
        hf_blk = bf + af * hf
        hb_blk = bb + ab * hb
        a_scr[0, rf, :] = hf_blk
        a_scr[1, rb, :] = hb_blk
        return hf_blk[SUBLANE - 1:SUBLANE, :], hb_blk[0:1, :]

    if has_init:
        init = (h0_ref[0:1, :], h0_ref[1:2, :])
    else:
        init = (jnp.zeros((1, cb), F32), jnp.zeros((1, cb), F32))
    hf, hb = lax.fori_loop(0, nblk, scan, init, unroll=2)
    sout_ref[0:1, :] = hf
    sout_ref[1:2, :] = hb

    def fin(i, carry):
        r = _rows(i, tile)
        out_ref[r, :] = ((a_scr[0, r, :] + a_scr[1, r, :]) * jax.nn.gelu(y_ref[r, :])).astype(BF16)
        return carry

    lax.fori_loop(0, t // tile, fin, 0)


def _lru(z, base, nseq, t, h0, conv_w, conv_b, w_a, b_a, w_i, b_i, lam, cb=512):
    nj = D_A // cb
    nb = cb // BS_A
    has_init = h0 is not None
    in_specs = [
        pl.BlockSpec((t, cb), lambda s, j: (base + s, j)),
        pl.BlockSpec((t, cb), lambda s, j: (base + s, nj + j)),
        pl.BlockSpec((CONV_W, cb), lambda s, j: (0, j)),
        pl.BlockSpec((1, cb), lambda s, j: (0, j)),
        pl.BlockSpec((2, nb, BS_A, BS_A), lambda s, j: (0, j, 0, 0)),
        pl.BlockSpec((2, cb), lambda s, j: (0, j)),
        pl.BlockSpec((2, nb, BS_A, BS_A), lambda s, j: (0, j, 0, 0)),
        pl.BlockSpec((2, cb), lambda s, j: (0, j)),
        pl.BlockSpec((2, cb), lambda s, j: (0, j)),
    ]
    args = [z, z, conv_w, conv_b.reshape(1, D_A), w_a.astype(BF16), b_a, w_i.astype(BF16), b_i, lam]
    if has_init:
        in_specs.append(pl.BlockSpec((None, 2, cb), lambda s, j: (s, 0, j)))
        args.append(h0)
    return pl.pallas_call(
        partial(_lru_kernel, t=t, cb=cb, has_init=has_init),
        grid=(nseq, nj),
        in_specs=in_specs,
        out_specs=[
            pl.BlockSpec((t, cb), lambda s, j: (s, j)),
            pl.BlockSpec((None, 2, cb), lambda s, j: (s, 0, j)),
        ],
        out_shape=[jax.ShapeDtypeStruct((nseq * t, D_A), BF16), jax.ShapeDtypeStruct((nseq, 2, D_A), F32)],
        scratch_shapes=[
            pltpu.VMEM((t + 2 * SUBLANE, cb), F32),
            pltpu.VMEM((t, cb), F32),
            pltpu.VMEM((2, t, cb), F32),
            pltpu.VMEM((2, t, cb), F32),
        ],
        compiler_params=_params(("arbitrary", "arbitrary")),
        name="rglru",
    )(*args)


def _mlstm_kernel(*refs, t, hg, has_init, has_prev, fill):
    it = iter(refs)
    q_ref, k_ref, v_ref, og_ref, zt_ref, bt_ref, cwq_ref, cbq_ref, cwk_ref, cbk_ref = (next(it) for _ in range(10))
    if has_init:
        c0_ref, n0_ref, m0_ref = next(it), next(it), next(it)
    if has_prev:
        next(it)
    mix_ref, cout_ref, nout_ref, mout_ref = (next(it) for _ in range(4))
    pad_scr, q_scr, k_scr, gate_scr, acc_scr, c_scr, n_scr = (next(it) for _ in range(7))
    CHUNK = MLSTM_CHUNK
    nchunk = t // CHUNK

    _dwconv_to(q_scr, q_ref, pad_scr, cwq_ref, cbq_ref, t, lambda v: _silu(v) * (DK_B ** -0.5))
    _dwconv_to(k_scr, k_ref, pad_scr, cwk_ref, cbk_ref, t, _silu)

    tile = min(t, 256)
    lane_t = lax.broadcasted_iota(jnp.int32, (tile, hg * LANE), 1) % LANE

    def gates(i, carry):
        r = _rows(i, tile)
        x = zt_ref[r, :] + bt_ref[...]
        logsig = jnp.minimum(x, 0.0) - jnp.log1p(jnp.exp(-jnp.abs(x)))
        gate_scr[r, :] = jnp.where(lane_t < 2, x, logsig)
        acc_scr[r, :] = jnp.zeros((tile, hg * DV_B), F32)
        return carry

    lax.fori_loop(0, t // tile, gates, 0)

    if has_init:
        c_scr[...] = c0_ref[...]
        n_scr[...] = n0_ref[...]
        m_init = tuple(m0_ref[d, u, :, 0:1] for u in range(hg) for d in range(2))
    else:
        c_scr[...] = jnp.zeros_like(c_scr)
        n_scr[...] = jnp.zeros_like(n_scr)
        m_init = tuple(jnp.zeros((1, 1), F32) for _ in range(2 * hg))

    row = lax.broadcasted_iota(jnp.int32, (CHUNK, CHUNK), 0)
    col = lax.broadcasted_iota(jnp.int32, (CHUNK, CHUNK), 1)
    lane_c = lax.broadcasted_iota(jnp.int32, (CHUNK, LANE), 1)

    def chunk(cidx, u, d, m):
        reverse = d == 1
        li, lf = d, 2 + d
        last = 0 if reverse else CHUNK - 1
        r = _rows(cidx, CHUNK)
        ck = slice(u * DK_B, (u + 1) * DK_B)
        cv = slice(u * DV_B, (u + 1) * DV_B)
        gt = gate_scr[r, u * LANE:(u + 1) * LANE]
        fsum = jnp.dot(_tri(reverse, CHUNK), gt, precision=HIGHEST, preferred_element_type=F32)
        y = jnp.where(lane_c == lf, fsum, gt)
        yt = y.T
        f_col, i_col = y[:, lf:lf + 1], y[:, li:li + 1]
        f_row, i_row = yt[lf:lf + 1, :], yt[li:li + 1, :]
        b = f_col + m
        mask = (col >= row) if reverse else (col <= row)
        dlog = jnp.where(mask, f_col - f_row + i_row, NEG_INF)
        m_t = jnp.maximum(b, jnp.max(dlog, axis=-1, keepdims=True))
        w = jnp.exp(dlog - m_t)
        inter = jnp.exp(b - m_t)
        q = q_scr[r, ck]
        k32 = k_scr[r, ck]
        v = v_ref[r, cv].astype(BF16)
        s = lax.dot_general(q, k32.astype(BF16), NT, preferred_element_type=F32) * w
        c_old = c_scr[d, u]
        n_old = n_scr[d, u]
        num = jnp.dot(s.astype(BF16), v, preferred_element_type=F32)
        num = num + inter * jnp.dot(q, c_old.astype(BF16), preferred_element_type=F32)
        den = jnp.sum(s, axis=-1, keepdims=True) + inter * jnp.sum(q.astype(F32) * n_old, axis=-1, keepdims=True)
        acc_scr[r, cv] += num / jnp.maximum(jnp.abs(den), jnp.exp(-m_t))
        m_new = m_t[last:last + 1, :]
        f_last = f_col[last:last + 1, :]
        kw = jnp.exp(f_last - f_col + i_col - m_new)
        decay = jnp.exp(f_last + m - m_new)
        kk = k32 * kw
        c_scr[d, u] = decay * c_old + lax.dot_general(kk.astype(BF16), v, TN, preferred_element_type=F32)
        n_scr[d, u] = decay * n_old + jnp.sum(kk, axis=0, keepdims=True)
        return m_new

    def body(c, carry):
        out = []
        for u in range(hg):
            out.append(chunk(c, u, 0, carry[2 * u]))
            out.append(chunk(nchunk - 1 - c, u, 1, carry[2 * u + 1]))
        return tuple(out)

    m_fin = lax.fori_loop(0, nchunk, body, m_init)

    _finalize_heads(acc_scr, og_ref, mix_ref, t, hg, DV_B)
    _state_target(cout_ref, fill)[...] = c_scr[...]
    nout_ref[...] = n_scr[...]
    for u in range(hg):
        for d in range(2):
            mout_ref[d, u] = jnp.broadcast_to(m_fin[2 * u + d], (1, LANE))


def _mlstm(z, zt, bias_t, base, nseq, t, init, conv_w, conv_b, hg, stack=None):
    has_init = init is not None
    prev, slot, nslots = stack if stack is not None else (None, 0, 1)
    fill = (slot, nslots) if stack is not None and prev is None else None
    kw, vw = hg * DK_B, hg * DV_B
    qoff = 2 * D_A // kw
    koff = qoff + H_B // hg
    voff = koff + H_B // hg
    goff = voff + H_B // hg
    cb2 = conv_b.reshape(1, 2 * H_B * DK_B)
    in_specs = [
        pl.BlockSpec((t, kw), lambda s, h: (base + s, qoff + h)),
        pl.BlockSpec((t, kw), lambda s, h: (base + s, koff + h)),
        pl.BlockSpec((t, vw), lambda s, h: (base + s, voff + h)),
        pl.BlockSpec((t, vw), lambda s, h: (base + s, goff + h)),
        pl.BlockSpec((t, hg * LANE), lambda s, h: (base + s, h)),
        pl.BlockSpec((1, hg * LANE), lambda s, h: (0, h)),
        pl.BlockSpec((CONV_W, kw), lambda s, h: (0, h)),
        pl.BlockSpec((1, kw), lambda s, h: (0, h)),
        pl.BlockSpec((CONV_W, kw), lambda s, h: (0, H_B // hg + h)),
        pl.BlockSpec((1, kw), lambda s, h: (0, H_B // hg + h)),
    ]
    args = [z, z, z, z, zt, bias_t, conv_w, cb2, conv_w, cb2]
    if has_init:
        c0, n0, m0 = init
        in_specs += [
            pl.BlockSpec((None, 2, hg, DK_B, DV_B), lambda s, h: (s, 0, h, 0, 0)),
            pl.BlockSpec((None, 2, hg, 1, DK_B), lambda s, h: (s, 0, h, 0, 0)),
            pl.BlockSpec((None, 2, hg, 1, LANE), lambda s, h: (s, 0, h, 0, 0)),
        ]
        args += [c0, n0.reshape(nseq, 2, H_B, 1, DK_B),
                 jnp.broadcast_to(m0[..., None, None], (nseq, 2, H_B, 1, LANE))]
    aliases = {}
    if prev is not None:
        in_specs.append(pl.BlockSpec(memory_space=pl.ANY))
        args.append(prev)
        aliases = {len(args) - 1: 1}
    out = pl.pallas_call(
        partial(_mlstm_kernel, t=t, hg=hg, has_init=has_init, has_prev=prev is not None, fill=fill),
        grid=(nseq, H_B // hg),
        in_specs=in_specs,
        out_specs=[
            pl.BlockSpec((t, vw), lambda s, h: (s, h)),
            _state_spec(fill, slot, hg, DK_B, DV_B),
            pl.BlockSpec((None, 2, hg, 1, DK_B), lambda s, h: (s, 0, h, 0, 0)),
            pl.BlockSpec((None, 2, hg, 1, LANE), lambda s, h: (s, 0, h, 0, 0)),
        ],
        out_shape=[
            jax.ShapeDtypeStruct((nseq * t, H_B * DV_B), BF16),
            jax.ShapeDtypeStruct((nseq, nslots, 2, H_B, DK_B, DV_B), F32),
            jax.ShapeDtypeStruct((nseq, 2, H_B, 1, DK_B), F32),
            jax.ShapeDtypeStruct((nseq, 2, H_B, 1, LANE), F32),
        ],
        scratch_shapes=[
            pltpu.VMEM((t + 2 * SUBLANE, kw), F32),
            pltpu.VMEM((t, kw), BF16),
            pltpu.VMEM((t, kw), F32),
            pltpu.VMEM((t, hg * LANE), F32),
            pltpu.VMEM((t, vw), F32),
            pltpu.VMEM((2, hg, DK_B, DV_B), F32),
            pltpu.VMEM((2, hg, 1, DK_B), F32),
        ],
        input_output_aliases=aliases,
        compiler_params=_params(("arbitrary", "arbitrary")),
        name="mlstm",
    )(*args)
    return out if stack is not None else (out[0], out[1][:, 0], out[2], out[3])


def _ret_kernel(*refs, t, hg, use_rope, has_init, has_prev, fill):
    it = iter(refs)
    lg_ref, q_ref, k_ref, v_ref, g_ref = (next(it) for _ in range(5))
    if use_rope:
        cos_ref, sin_ref = next(it), next(it)
    s0_ref = next(it) if has_init else None
    if has_prev:
        next(it)
    mix_ref, sout_ref, acc_scr, st_scr = (next(it) for _ in range(4))
    CHUNK = RET_CHUNK
    nchunk = t // CHUNK
    h0 = pl.program_id(1) * hg

    row = lax.broadcasted_iota(jnp.int32, (CHUNK, CHUNK), 0)
    col = lax.broadcasted_iota(jnp.int32, (CHUNK, CHUNK), 1)
    dist = (row - col).astype(F32)
    pos = lax.broadcasted_iota(jnp.int32, (CHUNK, DK_C), 0).astype(F32)
    consts = {}
    for u in range(hg):
        for d in range(2):
            lg = lg_ref[d, h0 + u]
            if d == 0:
                dm = jnp.where(dist >= 0, jnp.exp(jnp.maximum(dist, 0.0) * lg), 0.0)
                qin = jnp.exp((pos + 1.0) * lg)
                kout = jnp.exp((CHUNK - 1.0 - pos) * lg)
            else:
                dm = jnp.where(dist <= 0, jnp.exp(jnp.maximum(-dist, 0.0) * lg), 0.0)
                qin = jnp.exp((CHUNK - pos) * lg)
                kout = jnp.exp(pos * lg)
            gch = jnp.exp(jnp.full((1, DV_C), float(CHUNK), F32) * lg)
            consts[u, d] = (dm, qin, kout, gch)

    acc_scr[...] = jnp.zeros_like(acc_scr)
    if has_init:
        st_scr[...] = s0_ref[...]
    else:
        st_scr[...] = jnp.zeros_like(st_scr)

    def chunk(cidx, u, d):
        dm, qin, kout, gch = consts[u, d]
        r = _rows(cidx, CHUNK)
        ck = slice(u * DK_C, (u + 1) * DK_C)
        cv = slice(u * DV_C, (u + 1) * DV_C)
        q = q_ref[r, ck]
        k = k_ref[r, ck] * (DK_C ** -0.5)
        if use_rope:
            cs, sn = cos_ref[r, :], sin_ref[r, :]
            q = q * cs + pltpu.roll(q, DK_C // 2, 1) * sn
            k = k * cs + pltpu.roll(k, DK_C // 2, 1) * sn
        v = v_ref[r, cv].astype(BF16)
        st = st_scr[d, u]
        s = lax.dot_general(q.astype(BF16), k.astype(BF16), NT, preferred_element_type=F32) * dm
        o = jnp.dot(s.astype(BF16), v, preferred_element_type=F32)
        o = o + jnp.dot((q * qin).astype(BF16), st.astype(BF16), preferred_element_type=F32)
        st_scr[d, u] = gch * st + lax.dot_general((k * kout).astype(BF16), v, TN, preferred_element_type=F32)
        acc_scr[r, cv] += o

    def body(c, carry):
        for u in range(hg):
            chunk(c, u, 0)
            chunk(nchunk - 1 - c, u, 1)
        return carry

    lax.fori_loop(0, nchunk, body, 0)
    _finalize_heads(acc_scr, g_ref, mix_ref, t, hg, DV_C)
    _state_target(sout_ref, fill)[...] = st_scr[...]


def _retention(z, base, nseq, t, s0, lg, rope, hg, stack=None):
    has_init = s0 is not None
    prev, slot, nslots = stack if stack is not None else (None, 0, 1)
    fill = (slot, nslots) if stack is not None and prev is None else None
    use_rope = rope is not None
    kw, vw = hg * DK_C, hg * DV_C
    qoff, koff = 0, H_C * DK_C // kw
    voff = 2 * H_C * DK_C // vw
    goff = voff + H_C // hg
    in_specs = [
        pl.BlockSpec(memory_space=pltpu.SMEM),
        pl.BlockSpec((t, kw), lambda s, h: (base + s, qoff + h)),
        pl.BlockSpec((t, kw), lambda s, h: (base + s, koff + h)),
        pl.BlockSpec((t, vw), lambda s, h: (base + s, voff + h)),
        pl.BlockSpec((t, vw), lambda s, h: (base + s, goff + h)),
    ]
    args = [lg, z, z, z, z]
    if use_rope:
        in_specs += [pl.BlockSpec((t, DK_C), lambda s, h: (0, 0)), pl.BlockSpec((t, DK_C), lambda s, h: (0, 0))]
        args += list(rope)
    if has_init:
        in_specs.append(pl.BlockSpec((None, 2, hg, DK_C, DV_C), lambda s, h: (s, 0, h, 0, 0)))
        args.append(s0)
    aliases = {}
    if prev is not None:
        in_specs.append(pl.BlockSpec(memory_space=pl.ANY))
        args.append(prev)
        aliases = {len(args) - 1: 1}
    out = pl.pallas_call(
        partial(_ret_kernel, t=t, hg=hg, use_rope=use_rope, has_init=has_init, has_prev=prev is not None, fill=fill),
        grid=(nseq, H_C // hg),
        in_specs=in_specs,
        out_specs=[
            pl.BlockSpec((t, vw), lambda s, h: (s, h)),
            _state_spec(fill, slot, hg, DK_C, DV_C),
        ],
        out_shape=[
            jax.ShapeDtypeStruct((nseq * t, H_C * DV_C), BF16),
            jax.ShapeDtypeStruct((nseq, nslots, 2, H_C, DK_C, DV_C), F32),
        ],
        scratch_shapes=[pltpu.VMEM((t, vw), F32), pltpu.VMEM((2, hg, DK_C, DV_C), F32)],
        input_output_aliases=aliases,
        compiler_params=_params(("arbitrary", "arbitrary")),
        name="retention",
    )(*args)
    return out if stack is not None else (out[0], out[1][:, 0])


def _gla_intra(q, k, g2, reverse):
    CHUNK = GLA_CHUNK
    nsub = CHUNK // SUB
    lane = lax.broadcasted_iota(jnp.int32, (SUB, CHUNK), 1)
    rowi = lax.broadcasted_iota(jnp.int32, (SUB, CHUNK), 0)
    out = []
    for jb in range(nsub):
        lo, hi = jb * SUB, (jb + 1) * SUB
        qj, gj = q[lo:hi], g2[lo:hi]
        if reverse and jb < nsub - 1:
            gref = g2[hi:hi + 1]
            kt = (k[hi:] * jnp.exp2(gref - g2[hi:])).astype(BF16)
            kt = jnp.concatenate([jnp.zeros((hi, DK_D), BF16), kt], axis=0)
        elif not reverse and jb > 0:
            gref = g2[lo - 1:lo]
            kt = (k[:lo] * jnp.exp2(gref - g2[:lo])).astype(BF16)
            kt = jnp.concatenate([kt, jnp.zeros((CHUNK - lo, DK_D), BF16)], axis=0)
        else:
            kt = None
        if kt is None:
            off = jnp.zeros((SUB, CHUNK), F32)
        else:
            qt = (qj * jnp.exp2(gj - gref)).astype(BF16)
            off = lax.dot_general(qt, kt, NT, preferred_element_type=F32)
        diag = jnp.zeros((SUB, CHUNK), F32)
        for i in range(lo, hi):
            col = jnp.sum(qj * jnp.exp2(gj - g2[i:i + 1]) * k[i:i + 1], axis=-1, keepdims=True)
            diag = jnp.where(lane == i, col, diag)
        mask = (rowi + lo <= lane) if reverse else (rowi + lo >= lane)
        out.append(off + jnp.where(mask, diag, 0.0))
    return jnp.concatenate(out, axis=0)


def _gla_kernel(*refs, t, hg, has_init, has_prev, fill):
    it = iter(refs)
    q_ref, k_ref, v_ref, r_ref, a_ref, wup_ref, gb_ref = (next(it) for _ in range(7))
    s0_ref = next(it) if has_init else None
    if has_prev:
        next(it)
    mix_ref, sout_ref, acc_scr, st_scr, g_scr = (next(it) for _ in range(5))
    CHUNK = GLA_CHUNK
    nchunk = t // CHUNK
    tile = min(t, 256)

    def gates(i, carry):
        r = _rows(i, tile)
        a = a_ref[r, :].astype(BF16)
        for u in range(hg):
            for d in range(2):
                x = jnp.dot(a, wup_ref[d, u], preferred_element_type=F32) + gb_ref[d, u]
                g_scr[d, u, r, :] = (jnp.minimum(x, 0.0) - jnp.log1p(jnp.exp(-jnp.abs(x)))) * (1.0 / GLA_TAU)
        acc_scr[r, :] = jnp.zeros((tile, hg * DV_D), F32)
        return carry

    lax.fori_loop(0, t // tile, gates, 0)

    for u in range(hg):
        for d in range(2):
            st_scr[d, u] = s0_ref[d, u].T if has_init else jnp.zeros((DV_D, DK_D), F32)

    def chunk(cidx, u, d):
        reverse = d == 1
        last = 0 if reverse else CHUNK - 1
        r = _rows(cidx, CHUNK)
        ck = slice(u * DK_D, (u + 1) * DK_D)
        cv = slice(u * DV_D, (u + 1) * DV_D)
        g2 = jnp.dot(_tri(reverse, CHUNK), g_scr[d, u, r, :], precision=HIGHEST, preferred_element_type=F32) * LOG2E
        g2_last = g2[last:last + 1, :]
        q = q_ref[r, ck] * (DK_D ** -0.5)
        k = k_ref[r, ck]
        v = v_ref[r, cv].astype(BF16)
        st = st_scr[d, u]
        s = _gla_intra(q, k, g2, reverse)
        o = jnp.dot(s.astype(BF16), v, preferred_element_type=F32)
        o = o + lax.dot_general((q * jnp.exp2(g2)).astype(BF16), st.astype(BF16), NT, preferred_element_type=F32)
        kd = (k * jnp.exp2(g2_last - g2)).astype(BF16)
        st_scr[d, u] = jnp.exp2(g2_last) * st + lax.dot_general(v, kd, TN, preferred_element_type=F32)
        acc_scr[r, cv] += o

    def body(c, carry):
        for u in range(hg):
            chunk(c, u, 0)
            chunk(nchunk - 1 - c, u, 1)
        return carry

    lax.fori_loop(0, nchunk, body, 0)
    _finalize_heads(acc_scr, r_ref, mix_ref, t, hg, DV_D)
    state_out = _state_target(sout_ref, fill)
    for u in range(hg):
        for d in range(2):
            state_out[d, u] = st_scr[d, u].T


def _gla(z, zt, base, nseq, t, s0, w_up, g_bias, hg, stack=None):
    has_init = s0 is not None
    prev, slot, nslots = stack if stack is not None else (None, 0, 1)
    fill = (slot, nslots) if stack is not None and prev is None else None
    kw, vw = hg * DK_D, hg * DV_D
    cbase = 2 * H_C * DK_C + 2 * H_C * DV_C
    qoff = cbase // kw
    koff = qoff + H_D // hg
    voff = (cbase + 2 * H_D * DK_D) // vw
    roff = voff + H_D // hg
    in_specs = [
        pl.BlockSpec((t, kw), lambda s, h: (base + s, qoff + h)),
        pl.BlockSpec((t, kw), lambda s, h: (base + s, koff + h)),
        pl.BlockSpec((t, vw), lambda s, h: (base + s, voff + h)),
        pl.BlockSpec((t, vw), lambda s, h: (base + s, roff + h)),
        pl.BlockSpec((t, LANE), lambda s, h: (base + s, 0)),
        pl.BlockSpec((2, hg, LANE, DK_D), lambda s, h: (0, h, 0, 0)),
        pl.BlockSpec((2, hg, 1, DK_D), lambda s, h: (0, h, 0, 0)),
    ]
    args = [z, z, z, z, zt, w_up, g_bias]
    if has_init:
        in_specs.append(pl.BlockSpec((None, 2, hg, DK_D, DV_D), lambda s, h: (s, 0, h, 0, 0)))
        args.append(s0)
    aliases = {}
    if prev is not None:
        in_specs.append(pl.BlockSpec(memory_space=pl.ANY))
        args.append(prev)
        aliases = {len(args) - 1: 1}
    out = pl.pallas_call(
        partial(_gla_kernel, t=t, hg=hg, has_init=has_init, has_prev=prev is not None, fill=fill),
        grid=(nseq, H_D // hg),
        in_specs=in_specs,
        out_specs=[
            pl.BlockSpec((t, vw), lambda s, h: (s, h)),
            _state_spec(fill, slot, hg, DK_D, DV_D),
        ],
        out_shape=[
            jax.ShapeDtypeStruct((nseq * t, H_D * DV_D), BF16),
            jax.ShapeDtypeStruct((nseq, nslots, 2, H_D, DK_D, DV_D), F32),
        ],
        scratch_shapes=[
            pltpu.VMEM((t, vw), F32),
            pltpu.VMEM((2, hg, DV_D, DK_D), F32),
            pltpu.VMEM((2, hg, t, DK_D), F32),
        ],
        input_output_aliases=aliases,
        compiler_params=_params(("arbitrary", "arbitrary")),
        name="gla",
    )(*args)
    return out if stack is not None else (out[0], out[1][:, 0])


def _rope_tables(t):
    pos = jnp.arange(t)
    row = (pos // GRID_W).astype(F32)
    col = (pos % GRID_W).astype(F32)
    nf = DK_C // 4
    freqs = ROPE_BASE ** (-jnp.arange(nf, dtype=F32) / nf)
    ang = jnp.concatenate([row[:, None] * freqs, col[:, None] * freqs], axis=-1)
    cos, sin = jnp.cos(ang), jnp.sin(ang)
    return jnp.concatenate([cos, cos], axis=-1), jnp.concatenate([-sin, sin], axis=-1)


def _ab_tail(w_in, i_bias, f_bias):
    n_main = w_in.shape[1] - 4 * H_B
    gates = w_in[:, n_main:].reshape(D_MODEL, 4, H_B)
    cols = jnp.swapaxes(gates, 1, 2)
    w_tail = jnp.pad(cols, ((0, 0), (0, 0), (0, LANE - 4))).reshape(D_MODEL, H_B * LANE)
    bias = jnp.concatenate([i_bias, f_bias], axis=0).T
    bias_t = jnp.pad(bias, ((0, 0), (0, LANE - 4))).reshape(1, H_B * LANE)
    return n_main, w_tail, bias_t


def _cd_tail(w_in, w_up, g_bias):
    n_main = w_in.shape[1] - 2 * GLA_RANK
    w_tail = jnp.pad(w_in[:, n_main:], ((0, 0), (0, LANE - 2 * GLA_RANK)))
    up = w_up.reshape(2, GLA_RANK, H_D, DK_D).transpose(0, 2, 1, 3)
    up = jnp.stack([jnp.pad(up[d], ((0, 0), (d * GLA_RANK, LANE - (d + 1) * GLA_RANK), (0, 0))) for d in range(2)])
    return n_main, w_tail, up.astype(BF16), g_bias.reshape(2, H_D, 1, DK_D)


def kernel(x_prompt, x_sample, c, state_lru, state_mlstm_C, state_mlstm_n, state_mlstm_m, state_ret, state_gla, c_ctx, w_mod, b_mod, norm_w, ffn_w_gate, ffn_w_up, ffn_w_down, w_in_ab, w_out_ab, lru_conv_w, lru_conv_b, lru_w_a, lru_b_a, lru_w_i, lru_b_i, lru_lambda, mlstm_conv_w, mlstm_conv_b, mlstm_i_bias, mlstm_f_bias, w_in_cd, w_out_cd, ret_decay_log, gla_w_up, gla_b, final_norm_w):
    Bp, Tp, D = x_prompt.shape
    Bs, Ts, _ = x_sample.shape
    n_ctx = Bp * Tp
    tm_ffn, tf_ffn, tm_in, tm_out = 1024, 256, 1024, 512
    assert all(n_ctx % t == 0 and Ts % t == 0 for t in (tm_ffn, tm_in, tm_out)) and 1 + Bs <= MOD_ROWS and n_ctx % Ts == 0
    row_ffn, row_in, row_out = (_mod_row_map(n_ctx // t, Ts // t) for t in (tm_ffn, tm_in, tm_out))
    groups = ((0, Bp, Tp), (n_ctx // Ts, Bs, Ts))
    hg_ctx = {"mlstm": 4, "ret": 4, "gla": 4}
    hg_lat = {"mlstm": 1, "ret": 2, "gla": 2}

    cond = jnp.concatenate([c_ctx[None], c, jnp.zeros((MOD_ROWS - 1 - Bs, D), F32)], axis=0)
    mod = _modulation(cond, w_mod, b_mod).reshape(DEPTH, MOD_ROWS, N_MOD, D)
    nw = norm_w.reshape(DEPTH, 3, 1, D)
    wg, wu, wd = ffn_w_gate, ffn_w_up, ffn_w_down
    w_in_ab_b, w_in_cd_b = w_in_ab.astype(BF16), w_in_cd.astype(BF16)
    w_out_ab_b, w_out_cd_b = w_out_ab.astype(BF16), w_out_cd.astype(BF16)

    x = jnp.concatenate([x_prompt.reshape(n_ctx, D), x_sample.reshape(Bs * Ts, D)], axis=0)
    rope = _rope_tables(Ts)

    st_lru, st_n, st_m = [], [], []
    new_C = new_ret = new_gla = None
    for l in range(DEPTH):
        x = _ffn(x, mod, nw, wg, wu, wd, l, 0, row_ffn, tm_ffn, tf_ffn)
        (pb, pn, pt), (sb, sn, st) = groups
        if l % 2 == 0:
            e = l // 2
            n_main, w_tail, bias_t = _ab_tail(w_in_ab[e], mlstm_i_bias[e], mlstm_f_bias[e])
            z, zt = _inproj(x, mod, nw, w_in_ab_b, l, e, n_main, w_tail.astype(BF16), row_in, tm_in)
            lru_args = (lru_conv_w[e], lru_conv_b[e], lru_w_a[e], lru_b_a[e], lru_w_i[e], lru_b_i[e], lru_lambda[e])
            a_p, s_lru = _lru(z, pb, pn, pt, None, *lru_args)
            a_s, _ = _lru(z, sb, sn, st, state_lru[:, e], *lru_args)
            b_p, new_C, s_n, s_m = _mlstm(z, zt, bias_t, pb, pn, pt, None, mlstm_conv_w[e], mlstm_conv_b[e], hg_ctx["mlstm"],
                                          stack=(new_C, e, N_EVEN))
            b_s, _, _, _ = _mlstm(z, zt, bias_t, sb, sn, st,
                                  (state_mlstm_C[:, e], state_mlstm_n[:, e], state_mlstm_m[:, e]),
                                  mlstm_conv_w[e], mlstm_conv_b[e], hg_lat["mlstm"])
            st_lru.append(s_lru)
            st_n.append(s_n.reshape(Bp, 2, H_B, DK_B))
            st_m.append(s_m[:, :, :, 0, 0])
            mix_ctx, mix_lat = (a_p, b_p), (a_s, b_s)
            w_out = w_out_ab_b
        else:
            e = l // 2
            n_main, w_tail, w_up, g_bias = _cd_tail(w_in_cd[e], gla_w_up[e], gla_b[e])
            z, zt = _inproj(x, mod, nw, w_in_cd_b, l, e, n_main, w_tail.astype(BF16), row_in, tm_in)
            c_p, new_ret = _retention(z, pb, pn, pt, None, ret_decay_log[e], None, hg_ctx["ret"], stack=(new_ret, e, N_ODD))
            c_s, _ = _retention(z, sb, sn, st, state_ret[:, e], ret_decay_log[e], rope, hg_lat["ret"])
            d_p, new_gla = _gla(z, zt, pb, pn, pt, None, w_up, g_bias, hg_ctx["gla"], stack=(new_gla, e, N_ODD))
            d_s, _ = _gla(z, zt, sb, sn, st, state_gla[:, e], w_up, g_bias, hg_lat["gla"])
            mix_ctx, mix_lat = (c_p, d_p), (c_s, d_s)
            w_out = w_out_cd_b
        x = _outproj(x, mix_ctx, mix_lat, mod, w_out, l, e, row_out, tm_out)
        x = _ffn(x, mod, nw, wg, wu, wd, l, 1, row_ffn, tm_ffn, tf_ffn)

    y_prompt = _final_norm(x, final_norm_w, 0, n_ctx).reshape(Bp, Tp, D)
    y_sample = _final_norm(x, final_norm_w, n_ctx, Bs * Ts).reshape(Bs, Ts, D)
    return (y_prompt, y_sample,
            jnp.stack(st_lru, axis=1), new_C, jnp.stack(st_n, axis=1), jnp.stack(st_m, axis=1), new_ret, new_gla)
```
